```python
import math
import jax
import jax.numpy as jnp
from jax import lax
import numpy as np

D_MODEL = 1024
BATCH = 8
SEQ = 2048
DEPTH = 2

GRID_W = 64
CTX_LEN = 256
EPS = 1e-6
ROPE_BASE = 10000.0
N_MOD = 9
N_MOD_CTX_LAST = 5
N_BRANCH = 4
BRANCH_W = 512
FFN_HIDDEN = 2816

MLA_HEADS = 4
MLA_NOPE = 128
MLA_ROPE = 64
MLA_V = 128
MLA_Q_RANK = 256
MLA_KV_RANK = 256
MLA_BLOCK = 128
HY_W = 512
HY_EMB = 33
HY_FILT = 64
HY_SHORT = 3
HY_FAST_PCT = 0.3
HY_SLOW_PCT = 1.5
HY_TARGET = 1e-2
S5_W = 512
S5_H = 16
S5_G = S5_W // S5_H
S5_P = 64
NA_HEADS = 8
NA_DIM = 64
NA_WIN_R = 8
NA_WIN_C = 16

IN_SPLITS = (MLA_KV_RANK, MLA_ROPE, NA_HEADS * NA_DIM, NA_HEADS * NA_DIM, S5_W,
             MLA_Q_RANK, NA_HEADS * NA_DIM, 3 * HY_W, N_BRANCH * D_MODEL)
N_KV_SIDE = 5
N_IN = sum(IN_SPLITS)
N_CTX_COLS = sum(IN_SPLITS[:N_KV_SIDE])

kernel_name = 'hybrid_mla_hyena_s5_natten_block'


def rms_norm(x, g):
    xf = x.astype(jnp.float32)
    y = xf * lax.rsqrt(jnp.mean(xf * xf, axis=-1, keepdims=True) + EPS)
    return (y * g.astype(jnp.float32)).astype(x.dtype)


def split_cols(z, sizes):
    return jnp.split(z, np.cumsum(sizes)[:-1].tolist(), axis=-1)


def modulated_norm(x, g, shift, scale):
    return rms_norm(x, g) * (1 + scale) + shift


def swiglu(h, w_in, w_out):
    a, b = jnp.split(h @ w_in, 2, axis=-1)
    return (jax.nn.silu(a) * b) @ w_out


def ffn_sublayer(x, mods, base, g_pre, g_post, w_in, w_out):
    h = modulated_norm(x, g_pre, mods[base], mods[base + 1])
    return x + 0.5 * mods[base + 2] * rms_norm(swiglu(h, w_in, w_out), g_post)


def axial_rope(x):
    n_tok, r = x.shape[1], x.shape[-1]
    q = r // 4
    t = jnp.arange(n_tok)
    pos = jnp.stack([t // GRID_W, t % GRID_W], axis=-1).astype(jnp.float32)
    inv = ROPE_BASE ** (-jnp.arange(q, dtype=jnp.float32) / q)
    ang = pos[:, :, None] * inv
    cos, sin = jnp.cos(ang)[:, None], jnp.sin(ang)[:, None]
    xr = x.astype(jnp.float32).reshape(x.shape[:-1] + (2, 2, q))
    x1, x2 = xr[..., 0, :], xr[..., 1, :]
    out = jnp.stack([x1 * cos - x2 * sin, x2 * cos + x1 * sin], axis=-2)
    return out.reshape(x.shape).astype(x.dtype)


def attend(q, k, v, scale):
    s = jnp.einsum('bqhe,bkhe->bhqk', q, k, preferred_element_type=jnp.float32) * scale
    p = jax.nn.softmax(s, axis=-1).astype(v.dtype)
    return jnp.einsum('bhqk,bkhe->bqhe', p, v)


def blocked_attention(q, k, v, scale):
    b, n_tok, h, e = q.shape
    nb = n_tok // MLA_BLOCK
    qb = q.reshape(b, nb, MLA_BLOCK, h, e).transpose(1, 0, 2, 3, 4)
    ob = lax.map(lambda qq: attend(qq, k, v, scale), qb)
    return ob.transpose(1, 0, 2, 3, 4).reshape(b, n_tok, h * v.shape[-1])


def mla_q(c_q, g_q, w_uq, rope):
    q = jnp.einsum('blr,rhe->blhe', rms_norm(c_q, g_q), w_uq)
    if not rope:
        return q
    return jnp.concatenate([q[..., :MLA_NOPE], axial_rope(q[..., MLA_NOPE:])], axis=-1)


def mla_kv(c_kv, k_rope, g_kv, w_ukv, rope):
    kv = jnp.einsum('blr,rhe->blhe', rms_norm(c_kv, g_kv), w_ukv)
    kr = k_rope[:, :, None, :]
    if rope:
        kr = axial_rope(kr)
    kr = jnp.broadcast_to(kr, kv.shape[:3] + (MLA_ROPE,))
    return jnp.concatenate([kv[..., :MLA_NOPE], kr], axis=-1), kv[..., MLA_NOPE:]


def neighbourhood_attention(q, k, v, k_c, v_c, rpb):
    b, n_tok, h, e = q.shape
    rows = n_tok // GRID_W
    kr = min(NA_WIN_R, rows)
    r = jnp.arange(rows)
    row_idx = jnp.clip(r - kr // 2, 0, rows - kr)[:, None] + jnp.arange(kr)
    col = jnp.arange(GRID_W)
    c0 = jnp.clip(col - NA_WIN_C // 2, 0, GRID_W - NA_WIN_C)
    in_win = (col[None, :] >= c0[:, None]) & (col[None, :] < c0[:, None] + NA_WIN_C)
    dr = row_idx - r[:, None] + NA_WIN_R - 1
    dc = jnp.clip(col[None, :] - col[:, None] + NA_WIN_C - 1, 0, 2 * NA_WIN_C - 2)
    bias = rpb[:, dr[:, None, :, None], dc[None, :, None, :]].astype(jnp.float32)
    bias = jnp.where(in_win[None, None, :, None, :], bias, -jnp.inf).reshape(h, rows, GRID_W, kr * GRID_W)
    qg = q.reshape(b, rows, GRID_W, h, e)
    kg = k.reshape(b, rows, GRID_W, h, e)[:, row_idx].reshape(b, rows, kr * GRID_W, h, e)
    vg = v.reshape(b, rows, GRID_W, h, e)[:, row_idx].reshape(b, rows, kr * GRID_W, h, e)
    scale = e ** -0.5
    s_ctx = jnp.einsum('brqhe,bkhe->bhrqk', qg, k_c, preferred_element_type=jnp.float32) * scale
    s_loc = jnp.einsum('brqhe,brkhe->bhrqk', qg, kg, preferred_element_type=jnp.float32) * scale + bias
    p = jax.nn.softmax(jnp.concatenate([s_ctx, s_loc], axis=-1), axis=-1).astype(v.dtype)
    n_c = k_c.shape[1]
    o = (jnp.einsum('bhrqk,bkhe->brqhe', p[..., :n_c], v_c)
         + jnp.einsum('bhrqk,brkhe->brqhe', p[..., n_c:], vg))
    return o.reshape(b, n_tok, h * e)


def hyena_filters(n_tok, w1, b1, f1, w2, b2, f2, w3):
    bands = (HY_EMB - 1) // 2
    t = jnp.arange(n_tok, dtype=jnp.float32)
    t01 = jnp.linspace(0.0, 1.0, n_tok, dtype=jnp.float32)[:, None]
    ang = (2.0 * math.pi * t / n_tok)[:, None] * jnp.linspace(1e-4, bands - 1, bands, dtype=jnp.float32)
    z = jnp.concatenate([t01, jnp.cos(ang), -jnp.sin(ang)], axis=-1)
    hf = jnp.sin(f1 * (z @ w1 + b1))
    hf = jnp.sin(f2 * (hf @ w2 + b2))
    hf = (hf @ w3).astype(jnp.float32)
    max_decay = math.log(HY_TARGET) / HY_FAST_PCT
    min_decay = math.log(HY_TARGET) / HY_SLOW_PCT
    deltas = jnp.abs(jnp.linspace(min_decay, max_decay, HY_W, dtype=jnp.float32))
    decay = jnp.exp(-t01 * deltas)
    return hf * jnp.concatenate([decay, decay], axis=-1)


def hyena_stream(z, conv_w, conv_b, bias_d, filt):
    b, n_tok, ch = z.shape
    pad = HY_SHORT // 2
    u = lax.conv_general_dilated(z, conv_w[:, None, :], (1,), [(pad, HY_SHORT - 1 - pad)],
                                 dimension_numbers=('NWC', 'WIO', 'NWC'), feature_group_count=ch) + conv_b
    v, x1, x0 = jnp.split(u, 3, axis=-1)
    s = (v * x1).astype(jnp.float32)
    n_fft = 2 * n_tok
    spec_h = (jnp.fft.rfft(filt[:, :HY_W], n=n_fft, axis=0)
              + jnp.conj(jnp.fft.rfft(filt[:, HY_W:], n=n_fft, axis=0)))
    y = jnp.fft.irfft(jnp.fft.rfft(s, n=n_fft, axis=1) * spec_h, n=n_fft, axis=1)[:, :n_tok]
    y = y + s * bias_d.astype(jnp.float32)
    return (x0.astype(jnp.float32) * y).astype(z.dtype)


def s5_discretize(lam_re, lam_im, log_dt, b_re, b_im):
    lam = lax.complex(jnp.minimum(lam_re.astype(jnp.float32), -1e-4), lam_im.astype(jnp.float32))
    lam_dt = lam * jnp.exp(log_dt.astype(jnp.float32))[..., None]
    lam_bar = jnp.exp(lam_dt)
    b_bar = ((lam_bar - 1.0) / lam)[..., None] * lax.complex(b_re.astype(jnp.float32), b_im.astype(jnp.float32))
    return lam_dt, lam_bar, b_bar


def _lin_combine(e1, e2):
    a1, b1 = e1
    a2, b2 = e2
    return a1 * a2, a2 * b1 + b2


def s5_scan(u, lam_dt, lam_bar, b_bar, s0, reverse):
    bu = jnp.einsum('blgh,gph->blgp', u.astype(jnp.float32).astype(jnp.complex64), b_bar)
    a = jnp.broadcast_to(lam_bar, bu.shape)
    _, xs = lax.associative_scan(_lin_combine, (a, bu), reverse=reverse, axis=1)
    if s0 is not None:
        n_tok = u.shape[1]
        steps = (n_tok - jnp.arange(n_tok) if reverse else jnp.arange(n_tok) + 1).astype(jnp.float32)
        xs = xs + jnp.exp(lam_dt * steps[:, None, None])[None] * s0[:, None]
    return xs


def s5_readout(u, xf, xb, c_re, c_im, d, w_glu, b_glu):
    cm = lax.complex(c_re.astype(jnp.float32), c_im.astype(jnp.float32))
    y = jnp.real(jnp.einsum('blgp,ghp->blgh', xf, cm[0]) + jnp.einsum('blgp,ghp->blgh', xb, cm[1]))
    b, n_tok = u.shape[:2]
    y = y.reshape(b, n_tok, S5_W) + d.astype(jnp.float32) * u.reshape(b, n_tok, S5_W).astype(jnp.float32)
    g = jax.nn.gelu(y).astype(u.dtype)
    ga, gb = jnp.split(g @ w_glu + b_glu, 2, axis=-1)
    return ga * jax.nn.sigmoid(gb)


def merge_branches(branches, gate_pre, w_branch, w_out):
    b, n_tok, _ = gate_pre.shape
    g = jax.nn.sigmoid(gate_pre.reshape(b, n_tok, N_BRANCH, D_MODEL))
    proj = jnp.einsum('blnw,nwd->blnd', jnp.stack(branches, axis=2), w_branch)
    return jnp.sum(g * proj, axis=2) @ w_out


def token_mixer(hc, hl, p, ctx_out):
    b, n_lat, _ = hl.shape
    n_ctx = hc.shape[1]
    zl = split_cols(hl @ p['w_in'], IN_SPLITS)
    if ctx_out:
        zc = split_cols(hc @ p['w_in'], IN_SPLITS)
    else:
        zc = split_cols(hc @ p['w_in'][:, :N_CTX_COLS], IN_SPLITS[:N_KV_SIDE])
    ckv_l, kr_l, nk_l, nv_l, u_l, cq_l, nq_l, hy_l, gt_l = zl
    ckv_c, kr_c, nk_c, nv_c, u_c = zc[:N_KV_SIDE]
    heads = lambda t, n: t.reshape(t.shape[:2] + (n, -1))

    mla_scale = (MLA_NOPE + MLA_ROPE) ** -0.5
    k_c, v_c = mla_kv(ckv_c, kr_c, p['mla_g_kv'], p['mla_w_ukv'], False)
    k_l, v_l = mla_kv(ckv_l, kr_l, p['mla_g_kv'], p['mla_w_ukv'], True)
    q_l = mla_q(cq_l, p['mla_g_q'], p['mla_w_uq'], True)
    a_l = blocked_attention(q_l, jnp.concatenate([k_c, k_l], axis=1),
                            jnp.concatenate([v_c, v_l], axis=1), mla_scale)

    hyp = (p['hy_w1'], p['hy_b1'], p['hy_freq1'], p['hy_w2'], p['hy_b2'], p['hy_freq2'], p['hy_w3'])
    b_l = hyena_stream(hy_l, p['hy_conv_w'], p['hy_conv_b'], p['hy_bias'], hyena_filters(n_lat, *hyp))

    lam_dt, lam_bar, b_bar = s5_discretize(p['s5_lam_re'], p['s5_lam_im'], p['s5_log_dt'],
                                           p['s5_b_re'], p['s5_b_im'])
    uc = u_c.reshape(b, n_ctx, S5_G, S5_H)
    ul = u_l.reshape(b, n_lat, S5_G, S5_H)
    xf_c = s5_scan(uc, lam_dt[0], lam_bar[0], b_bar[0], None, False)
    xb_c = s5_scan(uc, lam_dt[1], lam_bar[1], b_bar[1], None, True)
    xf_l = s5_scan(ul, lam_dt[0], lam_bar[0], b_bar[0], xf_c[:, -1], False)
    xb_l = s5_scan(ul, lam_dt[1], lam_bar[1], b_bar[1], xb_c[:, 0], True)
    s5p = (p['s5_c_re'], p['s5_c_im'], p['s5_d'], p['s5_w_glu'], p['s5_b_glu'])
    c_l = s5_readout(ul, xf_l, xb_l, *s5p)

    nk_c, nv_c = heads(nk_c, NA_HEADS), heads(nv_c, NA_HEADS)
    d_l = neighbourhood_attention(heads(nq_l, NA_HEADS), heads(nk_l, NA_HEADS), heads(nv_l, NA_HEADS),
                                  nk_c, nv_c, p['na_rpb'])

    y_l = merge_branches([a_l, b_l, c_l, d_l], gt_l, p['w_branch'], p['w_out'])
    if not ctx_out:
        return None, y_l

    cq_c, nq_c, hy_c, gt_c = zc[N_KV_SIDE:]
    a_c = attend(mla_q(cq_c, p['mla_g_q'], p['mla_w_uq'], False), k_c, v_c, mla_scale).reshape(b, n_ctx, -1)
    b_c = hyena_stream(hy_c, p['hy_conv_w'], p['hy_conv_b'], p['hy_bias'], hyena_filters(n_ctx, *hyp))
    c_c = s5_readout(uc, xf_c, xb_c, *s5p)
    d_c = attend(heads(nq_c, NA_HEADS), nk_c, nv_c, NA_DIM ** -0.5).reshape(b, n_ctx, -1)
    y_c = merge_branches([a_c, b_c, c_c, d_c], gt_c, p['w_branch'], p['w_out'])
    return y_c, y_l


def hybrid_layer(xc, xl, mod_c, mod_l, p, ctx_out):
    g = p['norm_g']
    mc = jnp.split(mod_c, mod_c.shape[-1] // D_MODEL, axis=-1)
    ml = jnp.split(mod_l, N_MOD, axis=-1)
    fi, fo = p['ffn_w_in'], p['ffn_w_out']
    xl = ffn_sublayer(xl, ml, 0, g[0], g[1], fi[0], fo[0])
    xc = ffn_sublayer(xc, mc, 0, g[0], g[1], fi[0], fo[0])
    hl = modulated_norm(xl, g[2], ml[3], ml[4])
    hc = modulated_norm(xc, g[2], mc[3], mc[4])
    yc, yl = token_mixer(hc, hl, p, ctx_out)
    xl = xl + ml[5] * rms_norm(yl, g[3])
    xl = ffn_sublayer(xl, ml, 6, g[4], g[5], fi[1], fo[1])
    if not ctx_out:
        return None, xl
    xc = xc + mc[5] * rms_norm(yc, g[3])
    xc = ffn_sublayer(xc, mc, 6, g[4], g[5], fi[1], fo[1])
    return xc, xl


def setup_inputs(seed: int = 0) -> dict:
    key = jax.random.key(seed)
    ks = iter(jax.random.split(key, 48))
    nrm = lambda shape, scale: jax.random.normal(next(ks), shape, jnp.float32) * scale
    gain = lambda shape, s=0.05: 1.0 + nrm(shape, s)
    d, nl, f = D_MODEL, DEPTH, FFN_HIDDEN
    n_idx = jnp.arange(S5_P, dtype=jnp.float32)
    return {
        'x': nrm((BATCH, SEQ, d), 1.0),
        'c': nrm((BATCH, d), 1.0),
        'ctx': nrm((BATCH, CTX_LEN, d), 1.0),
        'c_ctx': nrm((d,), 1.0),
        'w_mod': nrm((nl, d, N_MOD * d), 0.5 * d ** -0.5),
        'b_mod': nrm((nl, N_MOD * d), 0.01),
        'norm_g': gain((nl, 6, d)),
        'ffn_w_in': nrm((nl, 2, d, 2 * f), d ** -0.5),
        'ffn_w_out': nrm((nl, 2, f, d), f ** -0.5),
        'w_in': nrm((nl, d, N_IN), d ** -0.5),
        'mla_g_q': gain((nl, MLA_Q_RANK)),
        'mla_g_kv': gain((nl, MLA_KV_RANK)),
        'mla_w_uq': nrm((nl, MLA_Q_RANK, MLA_HEADS, MLA_NOPE + MLA_ROPE), MLA_Q_RANK ** -0.5),
        'mla_w_ukv': nrm((nl, MLA_KV_RANK, MLA_HEADS, MLA_NOPE + MLA_V), MLA_KV_RANK ** -0.5),
        'na_rpb': nrm((nl, NA_HEADS, 2 * NA_WIN_R - 1, 2 * NA_WIN_C - 1), 0.2),
        'hy_conv_w': nrm((nl, HY_SHORT, 3 * HY_W), HY_SHORT ** -0.5),
        'hy_conv_b': nrm((nl, 3 * HY_W), 0.01),
        'hy_bias': nrm((nl, HY_W), 0.5),
        'hy_w1': nrm((nl, HY_EMB, HY_FILT), HY_EMB ** -0.5),
        'hy_b1': nrm((nl, HY_FILT), 0.1),
        'hy_freq1': gain((nl, HY_FILT), 0.1),
        'hy_w2': nrm((nl, HY_FILT, HY_FILT), HY_FILT ** -0.5),
        'hy_b2': nrm((nl, HY_FILT), 0.1),
        'hy_freq2': gain((nl, HY_FILT), 0.1),
        'hy_w3': nrm((nl, HY_FILT, 2 * HY_W), 0.1 * HY_FILT ** -0.5),
        's5_lam_re': -0.5 + nrm((nl, 2, S5_G, S5_P), 0.01),
        's5_lam_im': math.pi * n_idx + nrm((nl, 2, S5_G, S5_P), 0.01),
        's5_log_dt': jax.random.uniform(next(ks), (nl, 2, S5_G), jnp.float32, math.log(1e-3), math.log(1e-1)),
        's5_b_re': nrm((nl, 2, S5_G, S5_P, S5_H), (2 * S5_H) ** -0.5),
        's5_b_im': nrm((nl, 2, S5_G, S5_P, S5_H), (2 * S5_H) ** -0.5),
        's5_c_re': nrm((nl, 2, S5_G, S5_H, S5_P), 2.0 * S5_P ** -0.5),
        's5_c_im': nrm((nl, 2, S5_G, S5_H, S5_P), 2.0 * S5_P ** -0.5),
        's5_d': nrm((nl, S5_W), 0.5),
        's5_w_glu': nrm((nl, S5_W, 2 * S5_W), S5_W ** -0.5),
        's5_b_glu': nrm((nl, 2 * S5_W), 0.01),
        'w_branch': nrm((nl, N_BRANCH, BRANCH_W, d), BRANCH_W ** -0.5),
        'w_out': nrm((nl, d, d), d ** -0.5),
    }


def reference(x, c, ctx, c_ctx, w_mod, b_mod, norm_g, ffn_w_in, ffn_w_out, w_in,
              mla_g_q, mla_g_kv, mla_w_uq, mla_w_ukv, na_rpb,
              hy_conv_w, hy_conv_b, hy_bias, hy_w1, hy_b1, hy_freq1, hy_w2, hy_b2, hy_freq2, hy_w3,
              s5_lam_re, s5_lam_im, s5_log_dt, s5_b_re, s5_b_im, s5_c_re, s5_c_im, s5_d, s5_w_glu, s5_b_glu,
              w_branch, w_out):
    act_l = jax.nn.silu(c)
    act_c = jax.nn.silu(c_ctx)
    xc, xl = ctx, x
    for i in range(DEPTH):
        ctx_out = i < DEPTH - 1
        n_c = (N_MOD if ctx_out else N_MOD_CTX_LAST) * D_MODEL
        mod_l = (act_l @ w_mod[i] + b_mod[i])[:, None, :]
        mod_c = (act_c @ w_mod[i][:, :n_c] + b_mod[i][:n_c])[None, None, :]
        p = {
            'norm_g': norm_g[i], 'ffn_w_in': ffn_w_in[i], 'ffn_w_out': ffn_w_out[i], 'w_in': w_in[i],
            'mla_g_q': mla_g_q[i], 'mla_g_kv': mla_g_kv[i], 'mla_w_uq': mla_w_uq[i], 'mla_w_ukv': mla_w_ukv[i],
            'na_rpb': na_rpb[i],
            'hy_conv_w': hy_conv_w[i], 'hy_conv_b': hy_conv_b[i], 'hy_bias': hy_bias[i],
            'hy_w1': hy_w1[i], 'hy_b1': hy_b1[i], 'hy_freq1': hy_freq1[i],
            'hy_w2': hy_w2[i], 'hy_b2': hy_b2[i], 'hy_freq2': hy_freq2[i], 'hy_w3': hy_w3[i],
            's5_lam_re': s5_lam_re[i], 's5_lam_im': s5_lam_im[i], 's5_log_dt': s5_log_dt[i],
            's5_b_re': s5_b_re[i], 's5_b_im': s5_b_im[i], 's5_c_re': s5_c_re[i], 's5_c_im': s5_c_im[i],
            's5_d': s5_d[i], 's5_w_glu': s5_w_glu[i], 's5_b_glu': s5_b_glu[i],
            'w_branch': w_branch[i], 'w_out': w_out[i],
        }
        xc, xl = hybrid_layer(xc, xl, mod_c, mod_l, p, ctx_out)
    return xl
```

```python
import functools
import math

import jax
import jax.numpy as jnp
from jax import lax
from jax.experimental import pallas as pl
from jax.experimental.pallas import tpu as pltpu

F32 = jnp.float32
BF16 = jnp.bfloat16

EPS = 1e-6
GRID_W = 64
ROPE_BASE = 10000.0
N_MOD = 9
MLA_HEADS, MLA_NOPE, MLA_ROPE, MLA_V = 4, 128, 64, 128
MLA_RANK = 256
NA_HEADS, NA_DIM, NA_WIN_R, NA_WIN_C = 8, 64, 8, 16
HY_W, HY_EMB, HY_FILT = 512, 33, 64
HY_FAST_PCT, HY_SLOW_PCT, HY_TARGET = 0.3, 1.5, 1e-2
S5_H, S5_P = 16, 64
S5_CHUNK = 16
IN_SPLITS = (256, 64, 512, 512, 512, 256, 512, 1536, 4096)
FFN_CHUNK = 256
VMEM_LIMIT = 56 * 1024 * 1024


def _cparams(*sem):
    return pltpu.CompilerParams(dimension_semantics=sem, vmem_limit_bytes=VMEM_LIMIT)


def _const_spec(shape):
    nd = len(shape)
    return pl.BlockSpec(shape, lambda *_: (0,) * nd, pipeline_mode=pl.Buffered(1))


def _dot(a, b):
    return jnp.dot(a, b, preferred_element_type=F32)


def _dot_nt(a, b):
    return lax.dot_general(a, b, (((1,), (1,)), ((), ())), preferred_element_type=F32)


def _sigmoid(x):
    return 1.0 / (1.0 + jnp.exp(-x))


def _rms(x, g):
    return x * lax.rsqrt(jnp.mean(x * x, axis=-1, keepdims=True) + EPS) * g


def _modnorm(x, g, shift, scale):
    return _rms(x, g) * (1.0 + scale) + shift


def _mod_kernel(act_ref, w_ref, b_ref, o_ref):
    a = act_ref[...]
    a = a * _sigmoid(a)
    a_hi = a.astype(BF16)
    a_lo = (a - a_hi.astype(F32)).astype(BF16)
    w = w_ref[0]
    w_hi = w.astype(BF16)
    w_lo = (w - w_hi.astype(F32)).astype(BF16)
    o_ref[0] = _dot(a_hi, w_hi) + _dot(a_lo, w_hi) + _dot(a_hi, w_lo) + b_ref[0]


def _mod_call(acts, w_mod, b_mod):
    nl, d, nd = w_mod.shape
    rows = acts.shape[0]
    tn = 1152
    return pl.pallas_call(
        _mod_kernel,
        out_shape=jax.ShapeDtypeStruct((nl, rows, nd), F32),
        grid=(nl, nd // tn),
        in_specs=[pl.BlockSpec((rows, d), lambda l, j: (0, 0)),
                  pl.BlockSpec((1, d, tn), lambda l, j: (l, 0, j)),
                  pl.BlockSpec((1, 1, tn), lambda l, j: (l, 0, j))],
        out_specs=pl.BlockSpec((1, rows, tn), lambda l, j: (l, 0, j)),
        compiler_params=_cparams("parallel", "parallel"),
        name="mod",
    )(acts, w_mod, b_mod.reshape(nl, 1, nd))


def _ffn_kernel(x_ref, m_ref, g_ref, wi_ref, wo_ref, o_ref, acc_ref, *, base, gidx, n_chunks):
    x = x_ref[0]
    shift, scale, gate = (m_ref[0, base + k:base + k + 1, :] for k in range(3))
    h = _modnorm(x, g_ref[gidx:gidx + 1, :], shift, scale).astype(BF16)
    for j in range(n_chunks):
        a = _dot(h, wi_ref[0, j])
        b = _dot(h, wi_ref[1, j])
        act = (a * _sigmoid(a) * b).astype(BF16)
        part = _dot(act, wo_ref[j])
        if j == 0:
            acc_ref[...] = part
        else:
            acc_ref[...] += part
    y = acc_ref[...]
    o_ref[0] = x + 0.5 * gate * _rms(y, g_ref[gidx + 1:gidx + 2, :])


def _ffn_call(x, mods, mod_row, g, wi, wo, base, gidx):
    b, t, d = x.shape
    tm = min(512, t)
    n_chunks = wi.shape[1]
    mrow = (lambda bi, ti: (bi, 0, 0)) if mod_row is None else (lambda bi, ti: (mod_row, 0, 0))
    return pl.pallas_call(
        functools.partial(_ffn_kernel, base=base, gidx=gidx, n_chunks=n_chunks),
        out_shape=jax.ShapeDtypeStruct(x.shape, F32),
        grid=(b, t // tm),
        in_specs=[pl.BlockSpec((1, tm, d), lambda bi, ti: (bi, ti, 0)),
                  pl.BlockSpec((1, N_MOD, d), mrow),
                  _const_spec(g.shape), _const_spec(wi.shape), _const_spec(wo.shape)],
        out_specs=pl.BlockSpec((1, tm, d), lambda bi, ti: (bi, ti, 0)),
        scratch_shapes=[pltpu.VMEM((tm, d), F32)],
        compiler_params=_cparams("parallel", "parallel"),
        name="ffn",
    )(x, mods, g, wi, wo)


INPROJ_COLS = (640, 512, 1024, 512, 1536)


def _inproj_kernel(x_ref, m_ref, g_ref, w_ref, o_mla, o_nq, o_nkv, o_u, o_hy):
    x = x_ref[0]
    h = _modnorm(x, g_ref[2:3, :], m_ref[0, 3:4, :], m_ref[0, 4:5, :]).astype(BF16)
    off = 0
    for o_ref, n in zip((o_mla, o_nq, o_nkv, o_u, o_hy), INPROJ_COLS):
        o_ref[0] = _dot(h, w_ref[:, off:off + n]).astype(o_ref.dtype)
        off += n


def _inproj_call(x, mods, mod_row, g, w):
    b, t, d = x.shape
    tm = min(512, t)
    mrow = (lambda bi, ti: (bi, 0, 0)) if mod_row is None else (lambda bi, ti: (mod_row, 0, 0))
    dts = (F32, BF16, BF16, F32, F32)
    return pl.pallas_call(
        _inproj_kernel,
        out_shape=[jax.ShapeDtypeStruct((b, t, n), dt) for n, dt in zip(INPROJ_COLS, dts)],
        grid=(b, t // tm),
        in_specs=[pl.BlockSpec((1, tm, d), lambda bi, ti: (bi, ti, 0)),
                  pl.BlockSpec((1, N_MOD, d), mrow),
                  _const_spec(g.shape), _const_spec(w.shape)],
        out_specs=[pl.BlockSpec((1, tm, n), lambda bi, ti: (bi, ti, 0)) for n in INPROJ_COLS],
        compiler_params=_cparams("parallel", "parallel"),
        name="inproj",
    )(x, mods, g, w)


MLA_KW = 256


def _mla_kernel(*refs, n_ctx, n_lat, tq):
    if n_lat:
        zq_ref, zc_ref, zl_ref, tabq_ref, tabk_ref, gq_ref, gkv_ref, wq_ref, wkv_ref, o_ref, k_ref, v_ref = refs
    else:
        zq_ref, zc_ref, tabq_ref, gq_ref, gkv_ref, wq_ref, wkv_ref, o_ref, k_ref, v_ref = refs
    half = lax.broadcasted_iota(jnp.int32, (1, 128), 1) < MLA_ROPE

    def fill_kv(z, row0, tab):
        n = z.shape[0]
        kv = _dot(_rms(z[:, 0:MLA_RANK], gkv_ref[...]).astype(BF16), wkv_ref[...])
        r = z[:, 2 * MLA_RANK:2 * MLA_RANK + 128]
        if tab is None:
            rot = jnp.where(half, r, pltpu.roll(r, MLA_ROPE, 1))
        else:
            r = r * tab
            rot = r + pltpu.roll(r, MLA_ROPE, 1)
        rot = rot.astype(BF16)
        for h in range(MLA_HEADS):
            k_ref[h, pl.ds(row0, n), 0:MLA_NOPE] = kv[:, h * MLA_NOPE:(h + 1) * MLA_NOPE].astype(BF16)
            k_ref[h, pl.ds(row0, n), MLA_NOPE:MLA_KW] = rot
        v_ref[pl.ds(row0, n), :] = kv[:, MLA_HEADS * MLA_NOPE:].astype(BF16)

    @pl.when(pl.program_id(1) == 0)
    def _():
        fill_kv(zc_ref[0], 0, None)
        if n_lat:
            ck = 256

            def body(i, carry):
                r0 = pl.multiple_of(i * ck, ck)
                fill_kv(zl_ref[0, pl.ds(r0, ck), :], pl.multiple_of(n_ctx + r0, 16), tabk_ref[pl.ds(r0, ck), :])
                return carry

            lax.fori_loop(0, n_lat // ck, body, 0)

    zq = zq_ref[0]
    q = _dot(_rms(zq[:, MLA_RANK:2 * MLA_RANK], gq_ref[...]).astype(BF16), wq_ref[...])
    tabq = tabq_ref[...]
    for h in range(MLA_HEADS):
        qh = (q[:, h * MLA_KW:(h + 1) * MLA_KW] * tabq).astype(BF16)
        s = _dot_nt(qh, k_ref[h])
        m = jnp.max(s, axis=-1, keepdims=True)
        p = jnp.exp(s - m)
        l = jnp.sum(p, axis=-1, keepdims=True)
        o = _dot(p.astype(BF16), v_ref[:, h * MLA_V:(h + 1) * MLA_V])
        o_ref[0, :, h * MLA_V:(h + 1) * MLA_V] = (o / l).astype(o_ref.dtype)


def _mla_call(zq, zc, zl, tabq, tabk, gq, gkv, wq, wkv):
    b, t, w = zq.shape
    n_ctx = zc.shape[1]
    n_lat = 0 if zl is None else zl.shape[1]
    tq = min(256, t)
    n_keys = n_ctx + n_lat
    args = [zq, zc] + ([zl] if n_lat else []) + [tabq] + ([tabk] if n_lat else []) + [gq, gkv, wq, wkv]
    in_specs = [pl.BlockSpec((1, tq, w), lambda bi, qi: (bi, qi, 0)),
                pl.BlockSpec((1, n_ctx, w), lambda bi, qi: (bi, 0, 0))]
    if n_lat:
        in_specs.append(pl.BlockSpec((1, n_lat, w), lambda bi, qi: (bi, 0, 0)))
    in_specs.append(pl.BlockSpec((tq, MLA_KW), lambda bi, qi: (qi, 0)))
    if n_lat:
        in_specs.append(_const_spec(tabk.shape))
    in_specs += [_const_spec(gq.shape), _const_spec(gkv.shape), _const_spec(wq.shape), _const_spec(wkv.shape)]
    return pl.pallas_call(
        functools.partial(_mla_kernel, n_ctx=n_ctx, n_lat=n_lat, tq=tq),
        out_shape=jax.ShapeDtypeStruct((b, t, MLA_HEADS * MLA_V), BF16),
        grid=(b, t // tq),
        in_specs=in_specs,
        out_specs=pl.BlockSpec((1, tq, MLA_HEADS * MLA_V), lambda bi, qi: (bi, qi, 0)),
        scratch_shapes=[pltpu.VMEM((MLA_HEADS, n_keys, MLA_KW), BF16),
                        pltpu.VMEM((n_keys, MLA_HEADS * MLA_V), BF16)],
        compiler_params=_cparams("parallel", "arbitrary"),
        name="mla",
    )(*args)


NA_W = NA_HEADS * NA_DIM
NA_WIN = NA_WIN_R * GRID_W


def _na_kernel(*refs, local, rows):
    if local:
        q_ref, kvc_ref, kvl_ref, bias_ref, o_ref = refs
    else:
        q_ref, kvc_ref, o_ref = refs
    q = q_ref[0]
    lane_lo = lax.broadcasted_iota(jnp.int32, (1, 128), 1) < NA_DIM
    if local:
        r = pl.program_id(1)
        start = jnp.clip(r - NA_WIN_R // 2, 0, rows - NA_WIN_R)
        off = r - start
        win = pl.ds(pl.multiple_of(start * GRID_W, GRID_W), NA_WIN)
    scale = NA_DIM ** -0.5
    for j in range(NA_HEADS // 2):
        cols = slice(j * 128, (j + 1) * 128)
        vcols = slice(NA_W + j * 128, NA_W + (j + 1) * 128)
        qs = q[:, cols] * jnp.asarray(scale, BF16)
        k_c, v_c = kvc_ref[0, :, cols], kvc_ref[0, :, vcols]
        if local:
            k_l, v_l = kvl_ref[0, win, cols], kvl_ref[0, win, vcols]
        outs = []
        for e in range(2):
            qm = jnp.where(lane_lo if e == 0 else jnp.logical_not(lane_lo), qs, jnp.zeros_like(qs))
            s_c = _dot_nt(qm, k_c)
            m = jnp.max(s_c, axis=-1, keepdims=True)
            if local:
                s_l = _dot_nt(qm, k_l) + bias_ref[off, 2 * j + e]
                m = jnp.maximum(m, jnp.max(s_l, axis=-1, keepdims=True))
            p_c = jnp.exp(s_c - m)
            l = jnp.sum(p_c, axis=-1, keepdims=True)
            o = _dot(p_c.astype(BF16), v_c)
            if local:
                p_l = jnp.exp(s_l - m)
                l = l + jnp.sum(p_l, axis=-1, keepdims=True)
                o = o + _dot(p_l.astype(BF16), v_l)
            outs.append(o / l)
        o_ref[0, :, cols] = jnp.where(lane_lo, outs[0], outs[1]).astype(o_ref.dtype)


def _na_call(q, kvc, kvl, bias):
    b, t, _ = q.shape
    n_ctx = kvc.shape[1]
    local = kvl is not None
    tq = GRID_W if local else t
    rows = t // GRID_W
    if local:
        assert rows >= NA_WIN_R
    args = [q, kvc] + ([kvl, bias] if local else [])
    in_specs = [pl.BlockSpec((1, tq, NA_W), lambda bi, ri: (bi, ri, 0)),
                pl.BlockSpec((1, n_ctx, 2 * NA_W), lambda bi, ri: (bi, 0, 0))]
    if local:
        in_specs += [pl.BlockSpec((1, t, 2 * NA_W), lambda bi, ri: (bi, 0, 0)), _const_spec(bias.shape)]
    return pl.pallas_call(
        functools.partial(_na_kernel, local=local, rows=rows),
        out_shape=jax.ShapeDtypeStruct((b, t, NA_W), BF16),
        grid=(b, t // tq),
        in_specs=in_specs,
        out_specs=pl.BlockSpec((1, tq, NA_W), lambda bi, ri: (bi, ri, 0)),
        compiler_params=_cparams("parallel", "parallel"),
        name="na",
    )(*args)


def _hyfilt_kernel(z_ref, w1_ref, b1_ref, f1_ref, w2_ref, b2_ref, f2_ref, w3_ref, dec_ref, cm_ref, sm_ref,
                   hre_ref, him_ref, hny_ref):
    hi = lax.Precision.HIGHEST
    t = z_ref.shape[0]
    n_fft = 2 * t
    h = jnp.sin(f1_ref[...] * (jnp.dot(z_ref[...], w1_ref[...], precision=hi, preferred_element_type=F32) + b1_ref[...]))
    h = jnp.sin(f2_ref[...] * (jnp.dot(h, w2_ref[...], precision=hi, preferred_element_type=F32) + b2_ref[...]))
    taps = jnp.dot(h, w3_ref[...], precision=hi, preferred_element_type=F32)
    dec = dec_ref[...]
    hf = taps[:, :HY_W] * dec
    hb = taps[:, HY_W:] * dec
    row = lax.broadcasted_iota(jnp.int32, (t, 1), 0)
    wgt = jnp.where(row == 0, 1.0 / n_fft, 2.0 / n_fft)
    hsum = hf + hb
    hre_ref[...] = _dot(cm_ref[...], hsum.astype(BF16)) * wgt
    him_ref[...] = _dot(sm_ref[...], (hb - hf).astype(BF16)) * wgt
    alt = (1 - 2 * (row & 1)).astype(F32)
    hny = jnp.sum(hsum * alt, axis=0, keepdims=True) * (1.0 / n_fft)
    hny_ref[...] = jnp.broadcast_to(hny, hny_ref.shape)


def _hyfilt_call(z, w1, b1, f1, w2, b2, f2, w3, dec, cm, sm):
    t = z.shape[0]
    args = (z, w1, b1, f1, w2, b2, f2, w3, dec, cm, sm)
    return pl.pallas_call(
        _hyfilt_kernel,
        out_shape=[jax.ShapeDtypeStruct((t, HY_W), F32), jax.ShapeDtypeStruct((t, HY_W), F32),
                   jax.ShapeDtypeStruct((8, HY_W), F32)],
        grid=(1,),
        in_specs=[_const_spec(a.shape) for a in args],
        out_specs=[pl.BlockSpec((t, HY_W), lambda i: (0, 0)), pl.BlockSpec((t, HY_W), lambda i: (0, 0)),
                   pl.BlockSpec((8, HY_W), lambda i: (0, 0))],
        compiler_params=_cparams("arbitrary"),
        name="hyfilt",
    )(*args)


HY_CB = 256
HY_FB = 512


def _hyena_kernel(zv_ref, z1_ref, z0_ref, wv_ref, w1_ref, w0_ref, bv_ref, b1_ref, b0_ref, bd_ref,
                  cm_ref, sm_ref, hre_ref, him_ref, hny_ref, o_ref, y_ref):
    t = zv_ref.shape[1]
    row = lax.broadcasted_iota(jnp.int32, (t, 1), 0)

    def conv3(z_ref, w_ref, b_ref):
        z = z_ref[0]
        prev = jnp.where(row == 0, 0.0, pltpu.roll(z, 1, 0))
        nxt = jnp.where(row == t - 1, 0.0, pltpu.roll(z, t - 1, 0))
        return w_ref[0:1, :] * prev + w_ref[1:2, :] * z + w_ref[2:3, :] * nxt + b_ref[...]

    s = conv3(zv_ref, wv_ref, bv_ref) * conv3(z1_ref, w1_ref, b1_ref)
    alt = (1 - 2 * (row & 1)).astype(F32)
    sb = s.astype(BF16)
    xny = jnp.sum(s * alt, axis=0, keepdims=True)
    y_ref[...] = alt * (xny * hny_ref[0:1, :]) + s * bd_ref[...]
    fb = min(HY_FB, t)
    for f0 in range(0, t, fb):
        fr = slice(f0, f0 + fb)
        a = _dot(cm_ref[fr, :], sb)
        bq = _dot(sm_ref[fr, :], sb)
        hre = hre_ref[fr, :]
        him = him_ref[fr, :]
        yc = (a * hre + bq * him).astype(BF16)
        ys = (bq * hre - a * him).astype(BF16)
        y_ref[...] += _dot(cm_ref[:, fr], yc) + _dot(sm_ref[:, fr], ys)
    o_ref[0] = (conv3(z0_ref, w0_ref, b0_ref) * y_ref[...]).astype(o_ref.dtype)


def _hyena_call(z, conv_w, conv_b, bias_d, cm, sm, hre, him, hny):
    b, t, _ = z.shape
    nb = HY_W // HY_CB
    zspec = lambda k: pl.BlockSpec((1, t, HY_CB), lambda bi, ci: (bi, 0, k * nb + ci))
    wspec = lambda k: pl.BlockSpec((3, HY_CB), lambda bi, ci: (0, k * nb + ci))
    bspec = lambda k: pl.BlockSpec((1, HY_CB), lambda bi, ci: (0, k * nb + ci))
    cspec = lambda rows: pl.BlockSpec((rows, HY_CB), lambda bi, ci: (0, ci))
    conv_b = conv_b.reshape(1, -1)
    return pl.pallas_call(
        _hyena_kernel,
        out_shape=jax.ShapeDtypeStruct((b, t, HY_W), BF16),
        grid=(b, nb),
        in_specs=[zspec(0), zspec(1), zspec(2), wspec(0), wspec(1), wspec(2), bspec(0), bspec(1), bspec(2),
                  cspec(1), _const_spec(cm.shape), _const_spec(sm.shape), cspec(t), cspec(t), cspec(8)],
        out_specs=pl.BlockSpec((1, t, HY_CB), lambda bi, ci: (bi, 0, ci)),
        scratch_shapes=[pltpu.VMEM((t, HY_CB), F32)],
        compiler_params=_cparams("parallel", "parallel"),
        name="hyena",
    )(z, z, z, conv_w, conv_w, conv_w, conv_b, conv_b, conv_b, bias_d.reshape(1, -1), cm, sm, hre, him, hny)


S5_CW = S5_CHUNK * S5_H


def _s5_kernel(u_ref, m_ref, w_ref, v_ref, lam_ref, d_ref, o_ref, s_ref, xp_ref, *, nb, n_cc, n_lc):
    u = u_ref[0]
    ub = u.astype(BF16)
    s_ref[...] = _dot(ub, w_ref[0])
    ar = lam_ref[0, 0:1, :]
    ai = lam_ref[0, 1:2, :]
    fwd = lax.broadcasted_iota(jnp.int32, (1, 128), 1) < S5_P
    n_ch = n_cc + n_lc

    def body(k, carry):
        xa, xb = carry
        kb = jnp.where(k < n_cc, n_cc - 1 - k, n_ch + n_cc - 1 - k)
        rf = pl.multiple_of(k * nb, nb)
        rb = pl.multiple_of(kb * nb, nb)
        xp_ref[pl.ds(rf, nb), 0:S5_P] = xa[:, 0:S5_P]
        xp_ref[pl.ds(rf, nb), 128:128 + S5_P] = xb[:, 0:S5_P]
        xp_ref[pl.ds(rb, nb), S5_P:128] = xa[:, S5_P:128]
        xp_ref[pl.ds(rb, nb), 128 + S5_P:256] = xb[:, S5_P:128]
        sf = s_ref[pl.ds(rf, nb), :]
        sb = s_ref[pl.ds(rb, nb), :]
        in_a = jnp.where(fwd, sf[:, 0:128], sb[:, 0:128])
        in_b = jnp.where(fwd, sf[:, 128:256], sb[:, 128:256])
        return ar * xa - ai * xb + in_a, ar * xb + ai * xa + in_b

    zero = jnp.zeros((nb, 128), F32)
    lax.fori_loop(0, n_ch, body, (zero, zero))
    y = _dot(ub, m_ref[0]) + _dot(xp_ref[...].astype(BF16), v_ref[0]) + d_ref[0] * u
    cdf = 0.5 * (1.0 + jnp.tanh(math.sqrt(2.0 / math.pi) * (y + 0.044715 * (y * y * y))))
    o_ref[0] = (y * cdf).astype(o_ref.dtype)


def _s5_call(u, m, w, v, lam, d, nb, n_cc, n_lc):
    g, rows, cw = u.shape
    gspec = lambda a: pl.BlockSpec((1,) + a.shape[1:], lambda gi: (gi, 0, 0))
    return pl.pallas_call(
        functools.partial(_s5_kernel, nb=nb, n_cc=n_cc, n_lc=n_lc),
        out_shape=jax.ShapeDtypeStruct((g, rows, cw), BF16),
        grid=(g,),
        in_specs=[gspec(a) for a in (u, m, w, v, lam, d)],
        out_specs=pl.BlockSpec((1, rows, cw), lambda gi: (gi, 0, 0)),
        scratch_shapes=[pltpu.VMEM((rows, cw), F32), pltpu.VMEM((rows, cw), F32)],
        compiler_params=_cparams("parallel"),
        name="s5",
    )(u, m, w, v, lam, d)


def _merge_kernel(x_ref, m_ref, g_ref, a_ref, b_ref, c_ref, d_ref, wg_ref, wb_ref, wglu_ref, bglu_ref, wo_ref,
                  o_ref, acc_ref):
    x = x_ref[0]
    d = x.shape[-1]
    h = _modnorm(x, g_ref[2:3, :], m_ref[0, 3:4, :], m_ref[0, 4:5, :]).astype(BF16)
    glu = _dot(c_ref[0], wglu_ref[...]) + bglu_ref[...]
    nw = glu.shape[-1] // 2
    c = (glu[:, :nw] * _sigmoid(glu[:, nw:])).astype(BF16)
    for n, br in enumerate((a_ref[0], b_ref[0], c, d_ref[0])):
        gate = _sigmoid(_dot(h, wg_ref[:, n * d:(n + 1) * d]))
        part = gate * _dot(br, wb_ref[n])
        if n == 0:
            acc_ref[...] = part
        else:
            acc_ref[...] += part
    y = _dot(acc_ref[...].astype(BF16), wo_ref[...])
    o_ref[0] = x + m_ref[0, 5:6, :] * _rms(y, g_ref[3:4, :])


def _merge_call(x, mods, mod_row, g, a, b_, c, d_, wg, wb, wglu, bglu, wo):
    b, t, d = x.shape
    tm = min(256, t)
    mrow = (lambda bi, ti: (bi, 0, 0)) if mod_row is None else (lambda bi, ti: (mod_row, 0, 0))
    row = lambda n: pl.BlockSpec((1, tm, n), lambda bi, ti: (bi, ti, 0))
    bw = a.shape[-1]
    return pl.pallas_call(
        _merge_kernel,
        out_shape=jax.ShapeDtypeStruct(x.shape, F32),
        grid=(b, t // tm),
        in_specs=[row(d), pl.BlockSpec((1, N_MOD, d), mrow), _const_spec(g.shape),
                  row(bw), row(bw), row(bw), row(bw),
                  _const_spec(wg.shape), _const_spec(wb.shape), _const_spec(wglu.shape), _const_spec(bglu.shape),
                  _const_spec(wo.shape)],
        out_specs=row(d),
        scratch_shapes=[pltpu.VMEM((tm, d), F32)],
        compiler_params=_cparams("parallel", "parallel"),
        name="merge",
    )(x, mods, g, a, b_, c, d_, wg, wb, wglu, bglu, wo)


def _rope_table(n_tok):
    q = MLA_ROPE // 4
    t = jnp.arange(n_tok)
    pos = jnp.stack([t // GRID_W, t % GRID_W], axis=-1).astype(F32)
    inv = ROPE_BASE ** (-jnp.arange(q, dtype=F32) / q)
    ang = pos[:, :, None] * inv
    cos, sin = jnp.cos(ang), jnp.sin(ang)
    cos_t = jnp.stack([cos, cos], axis=2).reshape(n_tok, MLA_ROPE)
    sin_t = jnp.stack([-sin, sin], axis=2).reshape(n_tok, MLA_ROPE)
    return jnp.concatenate([cos_t, sin_t], axis=-1)


def _rope_swap_perm():
    q = MLA_ROPE // 4
    d = jnp.arange(MLA_ROPE)
    return jnp.where((d // q) % 2 == 0, d + q, d - q)


def _dft_mats(t):
    n = 2 * t
    f = jnp.arange(t, dtype=jnp.int32)
    ang = ((f[:, None] * f[None, :]) % n).astype(F32) * (2.0 * math.pi / n)
    return jnp.cos(ang).astype(BF16), jnp.sin(ang).astype(BF16)


def _hyena_consts(n_tok):
    bands = (HY_EMB - 1) // 2
    t = jnp.arange(n_tok, dtype=F32)
    t01 = jnp.linspace(0.0, 1.0, n_tok, dtype=F32)[:, None]
    ang = (2.0 * math.pi * t / n_tok)[:, None] * jnp.linspace(1e-4, bands - 1, bands, dtype=F32)
    z = jnp.concatenate([t01, jnp.cos(ang), -jnp.sin(ang)], axis=-1)
    z = jnp.pad(z, ((0, 0), (0, 128 - HY_EMB)))
    max_decay = math.log(HY_TARGET) / HY_FAST_PCT
    min_decay = math.log(HY_TARGET) / HY_SLOW_PCT
    deltas = jnp.abs(jnp.linspace(min_decay, max_decay, HY_W, dtype=F32))
    return z, jnp.exp(-t01 * deltas)


def _na_bias_table(rpb):
    o = jnp.arange(NA_WIN_R)
    k = jnp.arange(NA_WIN_R)
    dr = k[None, :] - o[:, None] + NA_WIN_R - 1
    col = jnp.arange(GRID_W)
    c0 = jnp.clip(col - NA_WIN_C // 2, 0, GRID_W - NA_WIN_C)
    in_win = (col[None, :] >= c0[:, None]) & (col[None, :] < c0[:, None] + NA_WIN_C)
    dc = jnp.clip(col[None, :] - col[:, None] + NA_WIN_C - 1, 0, 2 * NA_WIN_C - 2)
    bias = rpb[:, dr[:, None, :, None], dc[None, :, None, :]].astype(F32)
    bias = jnp.where(in_win[None, None, :, None, :], bias, -jnp.inf)
    return bias.transpose(1, 0, 2, 3, 4).reshape(NA_WIN_R, NA_HEADS, GRID_W, NA_WIN)


def _s5_operators(lam_re, lam_im, log_dt, b_re, b_im, c_re, c_im):
    tc = S5_CHUNK
    lam = lax.complex(jnp.minimum(lam_re, -1e-4), lam_im)
    lam_dt = lam * jnp.exp(log_dt)[..., None]
    lam_bar = jnp.exp(lam_dt)
    b_bar = ((lam_bar - 1.0) / lam)[..., None] * lax.complex(b_re, b_im)
    cm = lax.complex(c_re, c_im)
    pw = jnp.exp(lam_dt[..., None] * jnp.arange(tc + 1, dtype=F32))
    g = lam_re.shape[1]
    kern = jnp.real(jnp.einsum('dgap,dgpk,dgph->dkgah', cm, pw[..., :tc], b_bar,
                               precision=lax.Precision.HIGHEST))
    s = jnp.arange(tc)[:, None]
    j = jnp.arange(tc)[None, :]
    mf = jnp.where((j >= s)[:, :, None, None, None], kern[0][jnp.clip(j - s, 0, tc - 1)], 0.0)
    mb = jnp.where((s >= j)[:, :, None, None, None], kern[1][jnp.clip(s - j, 0, tc - 1)], 0.0)
    m = (mf + mb).transpose(2, 0, 4, 1, 3).reshape(g, S5_CW, S5_CW)
    pf = pw[0][..., tc - 1 - jnp.arange(tc)]
    pb = pw[1][..., :tc]
    wf = jnp.einsum('gps,gph->gshp', pf, b_bar[0]).reshape(g, S5_CW, S5_P)
    wb = jnp.einsum('gps,gph->gshp', pb, b_bar[1]).reshape(g, S5_CW, S5_P)
    w = jnp.concatenate([wf.real, wb.real, wf.imag, wb.imag], axis=-1)
    gf = jnp.einsum('gap,gpj->gpja', cm[0], pw[0][..., 1:tc + 1]).reshape(g, S5_P, S5_CW)
    gb = jnp.einsum('gap,gpj->gpja', cm[1], pw[1][..., tc - jnp.arange(tc)]).reshape(g, S5_P, S5_CW)
    v = jnp.concatenate([gf.real, gb.real, -gf.imag, -gb.imag], axis=1)
    a16 = pw[..., tc]
    lam16 = jnp.stack([jnp.concatenate([a16[0].real, a16[1].real], -1),
                       jnp.concatenate([a16[0].imag, a16[1].imag], -1)], axis=1)
    lam16 = jnp.pad(lam16, ((0, 0), (0, 6), (0, 0)))
    return m.astype(BF16), w.astype(BF16), v.astype(BF16), lam16


def _prep_layer(i, p):
    d = p['w_in'].shape[1]
    f = p['ffn_w_out'].shape[2]
    nc = f // FFN_CHUNK
    wi = p['ffn_w_in'][i].reshape(2, d, 2, nc, FFN_CHUNK).transpose(0, 2, 3, 1, 4).astype(BF16)
    wo = p['ffn_w_out'][i].reshape(2, nc, FFN_CHUNK, d).astype(BF16)
    w_in = p['w_in'][i]
    seg = lambda k: w_in[:, sum(IN_SPLITS[:k]):sum(IN_SPLITS[:k + 1])]
    kr = seg(1)
    w_proj = jnp.concatenate([seg(0), seg(5), kr, kr[:, _rope_swap_perm()], seg(6), seg(2), seg(3), seg(4), seg(7)],
                             axis=1).astype(BF16)
    w_gate = seg(8).astype(BF16)
    wuq = p['mla_w_uq'][i]
    rope = wuq[..., MLA_NOPE:]
    wq = jnp.concatenate([wuq[..., :MLA_NOPE], rope, rope[..., _rope_swap_perm()]], axis=-1)
    wq = wq.reshape(MLA_RANK, MLA_HEADS * MLA_KW).astype(BF16)
    wukv = p['mla_w_ukv'][i]
    wkv = jnp.concatenate([wukv[..., :MLA_NOPE].reshape(MLA_RANK, -1), wukv[..., MLA_NOPE:].reshape(MLA_RANK, -1)],
                          axis=1).astype(BF16)
    s5m, s5w, s5v, s5lam = _s5_operators(p['s5_lam_re'][i], p['s5_lam_im'][i], p['s5_log_dt'][i], p['s5_b_re'][i],
                                         p['s5_b_im'][i], p['s5_c_re'][i], p['s5_c_im'][i])
    g = p['s5_lam_re'].shape[2]
    s5d = jnp.tile(p['s5_d'][i].reshape(g, 1, S5_H), (1, 1, S5_CHUNK))
    return dict(
        g=p['norm_g'][i], wi=wi, wo=wo, w_proj=w_proj, w_gate=w_gate,
        gq=p['mla_g_q'][i].reshape(1, -1), gkv=p['mla_g_kv'][i].reshape(1, -1), wq=wq, wkv=wkv,
        bias=_na_bias_table(p['na_rpb'][i]),
        hy_mlp=(jnp.pad(p['hy_w1'][i], ((0, 128 - HY_EMB), (0, 0))), p['hy_b1'][i].reshape(1, -1),
                p['hy_freq1'][i].reshape(1, -1), p['hy_w2'][i], p['hy_b2'][i].reshape(1, -1),
                p['hy_freq2'][i].reshape(1, -1), p['hy_w3'][i]),
        conv_w=p['hy_conv_w'][i], conv_b=p['hy_conv_b'][i], hy_bias=p['hy_bias'][i],
        s5m=s5m, s5w=s5w, s5v=s5v, s5lam=s5lam, s5d=s5d,
        wglu=p['s5_w_glu'][i].astype(BF16), bglu=p['s5_b_glu'][i].reshape(1, -1),
        wb=p['w_branch'][i].astype(BF16), w_out=p['w_out'][i].astype(BF16),
    )


def _s5_branch(u_c, u_l, w):
    b, n_ctx, width = u_c.shape
    n_lat = u_l.shape[1]
    g = width // S5_H
    n_cc, n_lc = n_ctx // S5_CHUNK, n_lat // S5_CHUNK
    n_ch = n_cc + n_lc
    u = jnp.concatenate([u_c, u_l], axis=1).reshape(b, n_ch, S5_CHUNK, g, S5_H)
    u = u.transpose(3, 1, 0, 2, 4).reshape(g, n_ch * b, S5_CW)
    y = _s5_call(u, w['s5m'], w['s5w'], w['s5v'], w['s5lam'], w['s5d'], b, n_cc, n_lc)
    y = y.reshape(g, n_ch, b, S5_CHUNK, S5_H).transpose(2, 1, 3, 0, 4).reshape(b, n_ch * S5_CHUNK, width)
    return y[:, :n_ctx], y[:, n_ctx:]


def _mixer(xc, xl, mods, w, consts, ctx_out):
    b, n_lat, _ = xl.shape
    n_ctx = xc.shape[1]
    ctx_row = b
    mla_l, nq_l, nkv_l, u_l, hy_l = _inproj_call(xl, mods, None, w['g'], w['w_proj'])
    mla_c, nq_c, nkv_c, u_c, hy_c = _inproj_call(xc, mods, ctx_row, w['g'], w['w_proj'])

    scale = (MLA_NOPE + MLA_ROPE) ** -0.5
    rope_l = consts['rope_l']
    tabq_l = jnp.concatenate([jnp.ones((n_lat, MLA_NOPE), F32), rope_l], axis=-1) * scale
    a_l = _mla_call(mla_l, mla_c, mla_l, tabq_l, rope_l, w['gq'], w['gkv'], w['wq'], w['wkv'])
    d_l = _na_call(nq_l, nkv_c, nkv_l, w['bias'])
    cy_c, cy_l = _s5_branch(u_c, u_l, w)
    hre, him, hny = _hyfilt_call(consts['hy_z_l'], *w['hy_mlp'], consts['hy_dec_l'], *consts['dft_l'])
    b_l = _hyena_call(hy_l, w['conv_w'], w['conv_b'], w['hy_bias'], *consts['dft_l'], hre, him, hny)
    merge_w = (w['w_gate'], w['wb'], w['wglu'], w['bglu'], w['w_out'])
    xl = _merge_call(xl, mods, None, w['g'], a_l, b_l, cy_l, d_l, *merge_w)
    if not ctx_out:
        return None, xl

    tabq_c = jnp.concatenate([jnp.ones((n_ctx, MLA_NOPE + MLA_ROPE), F32), jnp.zeros((n_ctx, MLA_ROPE), F32)],
                             axis=-1) * scale
    a_c = _mla_call(mla_c, mla_c, None, tabq_c, None, w['gq'], w['gkv'], w['wq'], w['wkv'])
    d_c = _na_call(nq_c, nkv_c, None, None)
    hre, him, hny = _hyfilt_call(consts['hy_z_c'], *w['hy_mlp'], consts['hy_dec_c'], *consts['dft_c'])
    b_c = _hyena_call(hy_c, w['conv_w'], w['conv_b'], w['hy_bias'], *consts['dft_c'], hre, him, hny)
    xc = _merge_call(xc, mods, ctx_row, w['g'], a_c, b_c, cy_c, d_c, *merge_w)
    return xc, xl


def kernel(x, c, ctx, c_ctx, w_mod, b_mod, norm_g, ffn_w_in, ffn_w_out, w_in, mla_g_q, mla_g_kv, mla_w_uq, mla_w_ukv, na_rpb, hy_conv_w, hy_conv_b, hy_bias, hy_w1, hy_b1, hy_freq1, hy_w2, hy_b2, hy_freq2, hy_w3, s5_lam_re, s5_lam_im, s5_log_dt, s5_b_re, s5_b_im, s5_c_re, s5_c_im, s5_d, s5_w_glu, s5_b_glu, w_branch, w_out):
    p = dict(norm_g=norm_g, ffn_w_in=ffn_w_in, ffn_w_out=ffn_w_out, w_in=w_in, mla_g_q=mla_g_q, mla_g_kv=mla_g_kv,
             mla_w_uq=mla_w_uq, mla_w_ukv=mla_w_ukv, na_rpb=na_rpb, hy_conv_w=hy_conv_w, hy_conv_b=hy_conv_b,
             hy_bias=hy_bias, hy_w1=hy_w1, hy_b1=hy_b1, hy_freq1=hy_freq1, hy_w2=hy_w2, hy_b2=hy_b2,
             hy_freq2=hy_freq2, hy_w3=hy_w3, s5_lam_re=s5_lam_re, s5_lam_im=s5_lam_im, s5_log_dt=s5_log_dt,
             s5_b_re=s5_b_re, s5_b_im=s5_b_im, s5_c_re=s5_c_re, s5_c_im=s5_c_im, s5_d=s5_d, s5_w_glu=s5_w_glu,
             s5_b_glu=s5_b_glu, w_branch=w_branch, w_out=w_out)
    b, n_lat, d = x.shape
    n_ctx = ctx.shape[1]
    depth = w_mod.shape[0]
    assert b % 8 == 0 and n_lat % 256 == 0 and n_ctx % 16 == 0

    rows = -(-(b + 1) // 8) * 8
    acts = jnp.concatenate([c, c_ctx[None, :], jnp.zeros((rows - b - 1, d), F32)], axis=0)
    mods = _mod_call(acts, w_mod, b_mod).reshape(depth, rows, N_MOD, d)

    z_l, dec_l = _hyena_consts(n_lat)
    z_c, dec_c = _hyena_consts(n_ctx)
    consts = dict(rope_l=_rope_table(n_lat), dft_l=_dft_mats(n_lat), dft_c=_dft_mats(n_ctx),
                  hy_z_l=z_l, hy_dec_l=dec_l, hy_z_c=z_c, hy_dec_c=dec_c)

    xc, xl = ctx, x
    for i in range(depth):
        ctx_out = i < depth - 1
        w = _prep_layer(i, p)
        m = mods[i]
        xl = _ffn_call(xl, m, None, w['g'], w['wi'][0], w['wo'][0], 0, 0)
        xc = _ffn_call(xc, m, b, w['g'], w['wi'][0], w['wo'][0], 0, 0)
        xc, xl = _mixer(xc, xl, m, w, consts, ctx_out)
        xl = _ffn_call(xl, m, None, w['g'], w['wi'][1], w['wo'][1], 6, 4)
        if ctx_out:
            xc = _ffn_call(xc, m, b, w['g'], w['wi'][1], w['wo'][1], 6, 4)
    return xl
```

```python
import functools
import math

import jax
import jax.numpy as jnp
from jax import lax
from jax.experimental import pallas as pl
from jax.experimental.pallas import tpu as pltpu

F32 = jnp.float32
BF16 = jnp.bfloat16

EPS = 1e-6
GRID_W = 64
ROPE_BASE = 10000.0
N_MOD = 9
MLA_HEADS, MLA_NOPE, MLA_ROPE, MLA_V = 4, 128, 64, 128
MLA_RANK = 256
NA_HEADS, NA_DIM, NA_WIN_R, NA_WIN_C = 8, 64, 8, 16
HY_W, HY_EMB, HY_FILT = 512, 33, 64
HY_FAST_PCT, HY_SLOW_PCT, HY_TARGET = 0.3, 1.5, 1e-2
S5_H, S5_P = 16, 64
S5_CHUNK = 16
IN_SPLITS = (256, 64, 512, 512, 512, 256, 512, 1536, 4096)
FFN_CHUNK = 256
VMEM_LIMIT = 56 * 1024 * 1024


def _cparams(*sem):
    return pltpu.CompilerParams(dimension_semantics=sem, vmem_limit_bytes=VMEM_LIMIT)


def _const_spec(shape):
    nd = len(shape)
    return pl.BlockSpec(shape, lambda *_: (0,) * nd, pipeline_mode=pl.Buffered(1))


def _dot(a, b):
    return jnp.dot(a, b, preferred_element_type=F32)


def _dot_nt(a, b):
    return lax.dot_general(a, b, (((1,), (1,)), ((), ())), preferred_element_type=F32)


def _sigmoid(x):
    return 1.0 / (1.0 + jnp.exp(-x))


def _rms(x, g):
    return x * lax.rsqrt(jnp.mean(x * x, axis=-1, keepdims=True) + EPS) * g


def _modnorm(x, g, shift, scale):
    return _rms(x, g) * (1.0 + scale) + shift


def _mod_kernel(act_ref, w_ref, b_ref, o_ref):
    a = act_ref[...]
    a = a * _sigmoid(a)
    a_hi = a.astype(BF16)
    a_lo = (a - a_hi.astype(F32)).astype(BF16)
    w = w_ref[0]
    w_hi = w.astype(BF16)
    w_lo = (w - w_hi.astype(F32)).astype(BF16)
    o_ref[0] = _dot(a_hi, w_hi) + _dot(a_lo, w_hi) + _dot(a_hi, w_lo) + b_ref[0]


def _mod_call(acts, w_mod, b_mod):
    nl, d, nd = w_mod.shape
    rows = acts.shape[0]
    tn = 1152
    return pl.pallas_call(
        _mod_kernel,
        out_shape=jax.ShapeDtypeStruct((nl, rows, nd), F32),
        grid=(nl, nd // tn),
        in_specs=[pl.BlockSpec((rows, d), lambda l, j: (0, 0)),
                  pl.BlockSpec((1, d, tn), lambda l, j: (l, 0, j)),
                  pl.BlockSpec((1, 1, tn), lambda l, j: (l, 0, j))],
        out_specs=pl.BlockSpec((1, rows, tn), lambda l, j: (l, 0, j)),
        compiler_params=_cparams("parallel", "parallel"),
        name="mod",
    )(acts, w_mod, b_mod.reshape(nl, 1, nd))


def _ffn_kernel(x_ref, m_ref, g_ref, wi_ref, wo_ref, o_ref, acc_ref, *, base, gidx, n_chunks):
    x = x_ref[0]
    shift, scale, gate = (m_ref[0, base + k:base + k + 1, :] for k in range(3))
    h = _modnorm(x, g_ref[gidx:gidx + 1, :], shift, scale).astype(BF16)
    for j in range(n_chunks):
        a = _dot(h, wi_ref[0, j])
        b = _dot(h, wi_ref[1, j])
        act = (a * _sigmoid(a) * b).astype(BF16)
        part = _dot(act, wo_ref[j])
        if j == 0:
            acc_ref[...] = part
        else:
            acc_ref[...] += part
    y = acc_ref[...]
    o_ref[0] = x + 0.5 * gate * _rms(y, g_ref[gidx + 1:gidx + 2, :])


def _ffn_call(x, mods, mod_row, g, wi, wo, base, gidx):
    b, t, d = x.shape
    tm = min(512, t)
    n_chunks = wi.shape[1]
    mrow = (lambda bi, ti: (bi, 0, 0)) if mod_row is None else (lambda bi, ti: (mod_row, 0, 0))
    return pl.pallas_call(
        functools.partial(_ffn_kernel, base=base, gidx=gidx, n_chunks=n_chunks),
        out_shape=jax.ShapeDtypeStruct(x.shape, F32),
        grid=(b, t // tm),
        in_specs=[pl.BlockSpec((1, tm, d), lambda bi, ti: (bi, ti, 0)),
                  pl.BlockSpec((1, N_MOD, d), mrow),
                  _const_spec(g.shape), _const_spec(wi.shape), _const_spec(wo.shape)],
        out_specs=pl.BlockSpec((1, tm, d), lambda bi, ti: (bi, ti, 0)),
        scratch_shapes=[pltpu.VMEM((tm, d), F32)],
        compiler_params=_cparams("parallel", "parallel"),
        name="ffn",
    )(x, mods, g, wi, wo)


INPROJ_COLS = (640, 512, 1024, 512, 1536)


def _inproj_kernel(x_ref, m_ref, g_ref, w_ref, o_mla, o_nq, o_nkv, o_u, o_hy):
    x = x_ref[0]
    h = _modnorm(x, g_ref[2:3, :], m_ref[0, 3:4, :], m_ref[0, 4:5, :]).astype(BF16)
    off = 0
    for o_ref, n in zip((o_mla, o_nq, o_nkv, o_u, o_hy), INPROJ_COLS):
        o_ref[0] = _dot(h, w_ref[:, off:off + n]).astype(o_ref.dtype)
        off += n


def _inproj_call(x, mods, mod_row, g, w):
    b, t, d = x.shape
    tm = min(512, t)
    mrow = (lambda bi, ti: (bi, 0, 0)) if mod_row is None else (lambda bi, ti: (mod_row, 0, 0))
    dts = (F32, BF16, BF16, F32, F32)
    return pl.pallas_call(
        _inproj_kernel,
        out_shape=[jax.ShapeDtypeStruct((b, t, n), dt) for n, dt in zip(INPROJ_COLS, dts)],
        grid=(b, t // tm),
        in_specs=[pl.BlockSpec((1, tm, d), lambda bi, ti: (bi, ti, 0)),
                  pl.BlockSpec((1, N_MOD, d), mrow),
                  _const_spec(g.shape), _const_spec(w.shape)],
        out_specs=[pl.BlockSpec((1, tm, n), lambda bi, ti: (bi, ti, 0)) for n in INPROJ_COLS],
        compiler_params=_cparams("parallel", "parallel"),
        name="inproj",
    )(x, mods, g, w)


MLA_KW = 256


def _mla_kernel(*refs, n_ctx, n_lat, tq):
    if n_lat:
        zq_ref, zc_ref, zl_ref, tabq_ref, tabk_ref, gq_ref, gkv_ref, wq_ref, wkv_ref, o_ref, k_ref, v_ref = refs
    else:
        zq_ref, zc_ref, tabq_ref, gq_ref, gkv_ref, wq_ref, wkv_ref, o_ref, k_ref, v_ref = refs
    half = lax.broadcasted_iota(jnp.int32, (1, 128), 1) < MLA_ROPE

    def fill_kv(z, row0, tab):
        n = z.shape[0]
        kv = _dot(_rms(z[:, 0:MLA_RANK], gkv_ref[...]).astype(BF16), wkv_ref[...])
        r = z[:, 2 * MLA_RANK:2 * MLA_RANK + 128]
        if tab is None:
            rot = jnp.where(half, r, pltpu.roll(r, MLA_ROPE, 1))
        else:
            r = r * tab
            rot = r + pltpu.roll(r, MLA_ROPE, 1)
        rot = rot.astype(BF16)
        for h in range(MLA_HEADS):
            k_ref[h, pl.ds(row0, n), 0:MLA_NOPE] = kv[:, h * MLA_NOPE:(h + 1) * MLA_NOPE].astype(BF16)
            k_ref[h, pl.ds(row0, n), MLA_NOPE:MLA_KW] = rot
        v_ref[pl.ds(row0, n), :] = kv[:, MLA_HEADS * MLA_NOPE:].astype(BF16)

    @pl.when(pl.program_id(1) == 0)
    def _():
        fill_kv(zc_ref[0], 0, None)
        if n_lat:
            ck = 256

            def body(i, carry):
                r0 = pl.multiple_of(i * ck, ck)
                fill_kv(zl_ref[0, pl.ds(r0, ck), :], pl.multiple_of(n_ctx + r0, 16), tabk_ref[pl.ds(r0, ck), :])
                return carry

            lax.fori_loop(0, n_lat // ck, body, 0)

    zq = zq_ref[0]
    q = _dot(_rms(zq[:, MLA_RANK:2 * MLA_RANK], gq_ref[...]).astype(BF16), wq_ref[...])
    tabq = tabq_ref[...]
    for h in range(MLA_HEADS):
        qh = (q[:, h * MLA_KW:(h + 1) * MLA_KW] * tabq).astype(BF16)
        s = _dot_nt(qh, k_ref[h])
        m = jnp.max(s, axis=-1, keepdims=True)
        p = jnp.exp(s - m)
        l = jnp.sum(p, axis=-1, keepdims=True)
        o = _dot(p.astype(BF16), v_ref[:, h * MLA_V:(h + 1) * MLA_V])
        o_ref[0, :, h * MLA_V:(h + 1) * MLA_V] = (o / l).astype(o_ref.dtype)


def _mla_call(zq, zc, zl, tabq, tabk, gq, gkv, wq, wkv):
    b, t, w = zq.shape
    n_ctx = zc.shape[1]
    n_lat = 0 if zl is None else zl.shape[1]
    tq = min(256, t)
    n_keys = n_ctx + n_lat
    args = [zq, zc] + ([zl] if n_lat else []) + [tabq] + ([tabk] if n_lat else []) + [gq, gkv, wq, wkv]
    in_specs = [pl.BlockSpec((1, tq, w), lambda bi, qi: (bi, qi, 0)),
                pl.BlockSpec((1, n_ctx, w), lambda bi, qi: (bi, 0, 0))]
    if n_lat:
        in_specs.append(pl.BlockSpec((1, n_lat, w), lambda bi, qi: (bi, 0, 0)))
    in_specs.append(pl.BlockSpec((tq, MLA_KW), lambda bi, qi: (qi, 0)))
    if n_lat:
        in_specs.append(_const_spec(tabk.shape))
    in_specs += [_const_spec(gq.shape), _const_spec(gkv.shape), _const_spec(wq.shape), _const_spec(wkv.shape)]
    return pl.pallas_call(
        functools.partial(_mla_kernel, n_ctx=n_ctx, n_lat=n_lat, tq=tq),
        out_shape=jax.ShapeDtypeStruct((b, t, MLA_HEADS * MLA_V), BF16),
        grid=(b, t // tq),
        in_specs=in_specs,
        out_specs=pl.BlockSpec((1, tq, MLA_HEADS * MLA_V), lambda bi, qi: (bi, qi, 0)),
        scratch_shapes=[pltpu.VMEM((MLA_HEADS, n_keys, MLA_KW), BF16),
                        pltpu.VMEM((n_keys, MLA_HEADS * MLA_V), BF16)],
        compiler_params=_cparams("parallel", "arbitrary"),
        name="mla",
    )(*args)


NA_W = NA_HEADS * NA_DIM
NA_WIN = NA_WIN_R * GRID_W


def _na_kernel(*refs, local, rows):
    if local:
        q_ref, kvc_ref, kvl_ref, bias_ref, o_ref = refs
    else:
        q_ref, kvc_ref, o_ref = refs
    tq = q_ref.shape[1]
    lane_lo = lax.broadcasted_iota(jnp.int32, (1, 128), 1) < NA_DIM
    sub = GRID_W if local else tq
    wins, offs = [], []
    for rr in range(tq // sub if local else 0):
        r = pl.program_id(1) * (tq // sub) + rr
        start = jnp.clip(r - NA_WIN_R // 2, 0, rows - NA_WIN_R)
        offs.append(r - start)
        wins.append(pl.ds(pl.multiple_of(start * GRID_W, GRID_W), NA_WIN))
    scale = NA_DIM ** -0.5
    for j in range(NA_HEADS // 2):
        cols = slice(j * 128, (j + 1) * 128)
        vcols = slice(NA_W + j * 128, NA_W + (j + 1) * 128)
        k_c, v_c = kvc_ref[0, :, cols], kvc_ref[0, :, vcols]
        for rr in range(tq // sub):
            rs = slice(rr * sub, (rr + 1) * sub)
            qs = q_ref[0, rs, cols] * jnp.asarray(scale, BF16)
            if local:
                k_l, v_l = kvl_ref[0, wins[rr], cols], kvl_ref[0, wins[rr], vcols]
            outs = []
            for e in range(2):
                qm = jnp.where(lane_lo if e == 0 else jnp.logical_not(lane_lo), qs, jnp.zeros_like(qs))
                s_c = _dot_nt(qm, k_c)
                m = jnp.max(s_c, axis=-1, keepdims=True)
                if local:
                    s_l = _dot_nt(qm, k_l) + bias_ref[offs[rr], 2 * j + e]
                    m = jnp.maximum(m, jnp.max(s_l, axis=-1, keepdims=True))
                p_c = jnp.exp(s_c - m)
                l = jnp.sum(p_c, axis=-1, keepdims=True)
                o = _dot(p_c.astype(BF16), v_c)
                if local:
                    p_l = jnp.exp(s_l - m)
                    l = l + jnp.sum(p_l, axis=-1, keepdims=True)
                    o = o + _dot(p_l.astype(BF16), v_l)
                outs.append(o / l)
            o_ref[0, rs, cols] = jnp.where(lane_lo, outs[0], outs[1]).astype(o_ref.dtype)


NA_ROWS_PER_STEP = 4


def _nabias_kernel(r_ref, e_ref, m_ref, o_ref):
    o_ref[...] = jnp.dot(r_ref[...], e_ref[...], precision=lax.Precision.HIGHEST,
                         preferred_element_type=F32) + m_ref[...]


def _na_bias_table(rpb):
    h, na, nb = rpb.shape
    col = jnp.arange(GRID_W)
    c0 = jnp.clip(col - NA_WIN_C // 2, 0, GRID_W - NA_WIN_C)
    in_win = (col[None, :] >= c0[:, None]) & (col[None, :] < c0[:, None] + NA_WIN_C)
    dc = col[None, :] - col[:, None] + NA_WIN_C - 1
    onehot = (jnp.arange(128)[:, None, None] == dc[None]) & in_win[None]
    onehot = onehot.astype(F32).reshape(128, GRID_W * GRID_W)
    mask = jnp.where(in_win, 0.0, -jnp.inf).astype(F32).reshape(1, GRID_W * GRID_W)
    r = jnp.pad(rpb.reshape(h * na, nb), ((0, 128 - h * na), (0, 128 - nb)))
    t = pl.pallas_call(
        _nabias_kernel,
        out_shape=jax.ShapeDtypeStruct((128, GRID_W * GRID_W), F32),
        name="nabias",
    )(r, onehot, mask)
    t = t[:h * na].reshape(h, na, GRID_W, GRID_W)
    tabs = [t[:, NA_WIN_R - 1 - o:2 * NA_WIN_R - 1 - o].transpose(0, 2, 1, 3).reshape(h, GRID_W, NA_WIN)
            for o in range(NA_WIN_R)]
    return jnp.stack(tabs, axis=0)


def _na_call(q, kvc, kvl, bias):
    b, t, _ = q.shape
    n_ctx = kvc.shape[1]
    local = kvl is not None
    rows = t // GRID_W
    if local:
        assert rows >= NA_WIN_R and rows % NA_ROWS_PER_STEP == 0
    tq = GRID_W * NA_ROWS_PER_STEP if local else t
    args = [q, kvc] + ([kvl, bias] if local else [])
    in_specs = [pl.BlockSpec((1, tq, NA_W), lambda bi, ri: (bi, ri, 0)),
                pl.BlockSpec((1, n_ctx, 2 * NA_W), lambda bi, ri: (bi, 0, 0))]
    if local:
        in_specs += [pl.BlockSpec((1, t, 2 * NA_W), lambda bi, ri: (bi, 0, 0)), _const_spec(bias.shape)]
    return pl.pallas_call(
        functools.partial(_na_kernel, local=local, rows=rows),
        out_shape=jax.ShapeDtypeStruct((b, t, NA_W), BF16),
        grid=(b, t // tq),
        in_specs=in_specs,
        out_specs=pl.BlockSpec((1, tq, NA_W), lambda bi, ri: (bi, ri, 0)),
        compiler_params=_cparams("parallel", "parallel"),
        name="na",
    )(*args)


def _hyfilt_kernel(z_ref, w1_ref, b1_ref, f1_ref, w2_ref, b2_ref, f2_ref, w3_ref, dec_ref, cm_ref, sm_ref,
                   hre_ref, him_ref, hny_ref):
    hi = lax.Precision.HIGHEST
    t = z_ref.shape[0]
    n_fft = 2 * t
    h = jnp.sin(f1_ref[...] * (jnp.dot(z_ref[...], w1_ref[...], precision=hi, preferred_element_type=F32) + b1_ref[...]))
    h = jnp.sin(f2_ref[...] * (jnp.dot(h, w2_ref[...], precision=hi, preferred_element_type=F32) + b2_ref[...]))
    taps = jnp.dot(h, w3_ref[...], precision=hi, preferred_element_type=F32)
    dec = dec_ref[...]
    hf = taps[:, :HY_W] * dec
    hb = taps[:, HY_W:] * dec
    row = lax.broadcasted_iota(jnp.int32, (t, 1), 0)
    wgt = jnp.where(row == 0, 1.0 / n_fft, 2.0 / n_fft)
    hsum = hf + hb
    hre_ref[...] = _dot(cm_ref[...], hsum.astype(BF16)) * wgt
    him_ref[...] = _dot(sm_ref[...], (hb - hf).astype(BF16)) * wgt
    alt = (1 - 2 * (row & 1)).astype(F32)
    hny = jnp.sum(hsum * alt, axis=0, keepdims=True) * (1.0 / n_fft)
    hny_ref[...] = jnp.broadcast_to(hny, hny_ref.shape)


def _hyfilt_call(z, w1, b1, f1, w2, b2, f2, w3, dec, cm, sm):
    t = z.shape[0]
    args = (z, w1, b1, f1, w2, b2, f2, w3, dec, cm, sm)
    return pl.pallas_call(
        _hyfilt_kernel,
        out_shape=[jax.ShapeDtypeStruct((t, HY_W), F32), jax.ShapeDtypeStruct((t, HY_W), F32),
                   jax.ShapeDtypeStruct((8, HY_W), F32)],
        grid=(1,),
        in_specs=[_const_spec(a.shape) for a in args],
        out_specs=[pl.BlockSpec((t, HY_W), lambda i: (0, 0)), pl.BlockSpec((t, HY_W), lambda i: (0, 0)),
                   pl.BlockSpec((8, HY_W), lambda i: (0, 0))],
        compiler_params=_cparams("arbitrary"),
        name="hyfilt",
    )(*args)


HY_CB = 256
HY_FB = 512


def _hyena_kernel(zv_ref, z1_ref, z0_ref, wv_ref, w1_ref, w0_ref, bv_ref, b1_ref, b0_ref, bd_ref,
                  cm_ref, sm_ref, hre_ref, him_ref, hny_ref, o_ref, y_ref):
    t = zv_ref.shape[1]
    row = lax.broadcasted_iota(jnp.int32, (t, 1), 0)

    def conv3(z_ref, w_ref, b_ref):
        z = z_ref[0]
        prev = jnp.where(row == 0, 0.0, pltpu.roll(z, 1, 0))
        nxt = jnp.where(row == t - 1, 0.0, pltpu.roll(z, t - 1, 0))
        return w_ref[0:1, :] * prev + w_ref[1:2, :] * z + w_ref[2:3, :] * nxt + b_ref[...]

    s = conv3(zv_ref, wv_ref, bv_ref) * conv3(z1_ref, w1_ref, b1_ref)
    alt = (1 - 2 * (row & 1)).astype(F32)
    sb = s.astype(BF16)
    xny = jnp.sum(s * alt, axis=0, keepdims=True)
    y_ref[...] = alt * (xny * hny_ref[0:1, :]) + s * bd_ref[...]
    fb = min(HY_FB, t)
    for f0 in range(0, t, fb):
        fr = slice(f0, f0 + fb)
        a = _dot(cm_ref[fr, :], sb)
        bq = _dot(sm_ref[fr, :], sb)
        hre = hre_ref[fr, :]
        him = him_ref[fr, :]
        yc = (a * hre + bq * him).astype(BF16)
        ys = (bq * hre - a * him).astype(BF16)
        y_ref[...] += _dot(cm_ref[:, fr], yc) + _dot(sm_ref[:, fr], ys)
    o_ref[0] = (conv3(z0_ref, w0_ref, b0_ref) * y_ref[...]).astype(o_ref.dtype)


def _hyena_call(z, conv_w, conv_b, bias_d, cm, sm, hre, him, hny):
    b, t, _ = z.shape
    nb = HY_W // HY_CB
    zspec = lambda k: pl.BlockSpec((1, t, HY_CB), lambda bi, ci: (bi, 0, k * nb + ci))
    wspec = lambda k: pl.BlockSpec((3, HY_CB), lambda bi, ci: (0, k * nb + ci))
    bspec = lambda k: pl.BlockSpec((1, HY_CB), lambda bi, ci: (0, k * nb + ci))
    cspec = lambda rows: pl.BlockSpec((rows, HY_CB), lambda bi, ci: (0, ci))
    conv_b = conv_b.reshape(1, -1)
    return pl.pallas_call(
        _hyena_kernel,
        out_shape=jax.ShapeDtypeStruct((b, t, HY_W), BF16),
        grid=(b, nb),
        in_specs=[zspec(0), zspec(1), zspec(2), wspec(0), wspec(1), wspec(2), bspec(0), bspec(1), bspec(2),
                  cspec(1), _const_spec(cm.shape), _const_spec(sm.shape), cspec(t), cspec(t), cspec(8)],
        out_specs=pl.BlockSpec((1, t, HY_CB), lambda bi, ci: (bi, 0, ci)),
        scratch_shapes=[pltpu.VMEM((t, HY_CB), F32)],
        compiler_params=_cparams("parallel", "parallel"),
        name="hyena",
    )(z, z, z, conv_w, conv_w, conv_w, conv_b, conv_b, conv_b, bias_d.reshape(1, -1), cm, sm, hre, him, hny)


S5_CW = S5_CHUNK * S5_H


def _s5_kernel(u_ref, m_ref, w_ref, v_ref, lam_ref, d_ref, o_ref, s_ref, xp_ref, *, nb, n_cc, n_lc):
    u = u_ref[0]
    ub = u.astype(BF16)
    s_ref[...] = _dot(ub, w_ref[0])
    ar = lam_ref[0, 0:1, :]
    ai = lam_ref[0, 1:2, :]
    fwd = lax.broadcasted_iota(jnp.int32, (1, 128), 1) < S5_P
    n_ch = n_cc + n_lc

    def body(k, carry):
        xa, xb = carry
        kb = jnp.where(k < n_cc, n_cc - 1 - k, n_ch + n_cc - 1 - k)
        rf = pl.multiple_of(k * nb, nb)
        rb = pl.multiple_of(kb * nb, nb)
        xp_ref[pl.ds(rf, nb), 0:S5_P] = xa[:, 0:S5_P]
        xp_ref[pl.ds(rf, nb), 128:128 + S5_P] = xb[:, 0:S5_P]
        xp_ref[pl.ds(rb, nb), S5_P:128] = xa[:, S5_P:128]
        xp_ref[pl.ds(rb, nb), 128 + S5_P:256] = xb[:, S5_P:128]
        sf = s_ref[pl.ds(rf, nb), :]
        sb = s_ref[pl.ds(rb, nb), :]
        in_a = jnp.where(fwd, sf[:, 0:128], sb[:, 0:128])
        in_b = jnp.where(fwd, sf[:, 128:256], sb[:, 128:256])
        return ar * xa - ai * xb + in_a, ar * xb + ai * xa + in_b

    zero = jnp.zeros((nb, 128), F32)
    lax.fori_loop(0, n_ch, body, (zero, zero))
    y = _dot(ub, m_ref[0]) + _dot(xp_ref[...].astype(BF16), v_ref[0]) + d_ref[0] * u
    cdf = 0.5 * (1.0 + jnp.tanh(math.sqrt(2.0 / math.pi) * (y + 0.044715 * (y * y * y))))
    o_ref[0] = (y * cdf).astype(o_ref.dtype)


def _s5_call(u, m, w, v, lam, d, nb, n_cc, n_lc):
    g, rows, cw = u.shape
    gspec = lambda a: pl.BlockSpec((1,) + a.shape[1:], lambda gi: (gi, 0, 0))
    return pl.pallas_call(
        functools.partial(_s5_kernel, nb=nb, n_cc=n_cc, n_lc=n_lc),
        out_shape=jax.ShapeDtypeStruct((g, rows, cw), BF16),
        grid=(g,),
        in_specs=[gspec(a) for a in (u, m, w, v, lam, d)],
        out_specs=pl.BlockSpec((1, rows, cw), lambda gi: (gi, 0, 0)),
        scratch_shapes=[pltpu.VMEM((rows, cw), F32), pltpu.VMEM((rows, cw), F32)],
        compiler_params=_cparams("parallel"),
        name="s5",
    )(u, m, w, v, lam, d)


def _merge_kernel(x_ref, m_ref, g_ref, a_ref, b_ref, c_ref, d_ref, wg_ref, wb_ref, wglu_ref, bglu_ref, wo_ref,
                  o_ref, acc_ref):
    x = x_ref[0]
    d = x.shape[-1]
    h = _modnorm(x, g_ref[2:3, :], m_ref[0, 3:4, :], m_ref[0, 4:5, :]).astype(BF16)
    glu = _dot(c_ref[0], wglu_ref[...]) + bglu_ref[...]
    nw = glu.shape[-1] // 2
    c = (glu[:, :nw] * _sigmoid(glu[:, nw:])).astype(BF16)
    for n, br in enumerate((a_ref[0], b_ref[0], c, d_ref[0])):
        gate = _sigmoid(_dot(h, wg_ref[:, n * d:(n + 1) * d]))
        part = gate * _dot(br, wb_ref[n])
        if n == 0:
            acc_ref[...] = part
        else:
            acc_ref[...] += part
    y = _dot(acc_ref[...].astype(BF16), wo_ref[...])
    o_ref[0] = x + m_ref[0, 5:6, :] * _rms(y, g_ref[3:4, :])


def _merge_call(x, mods, mod_row, g, a, b_, c, d_, wg, wb, wglu, bglu, wo):
    b, t, d = x.shape
    tm = min(256, t)
    mrow = (lambda bi, ti: (bi, 0, 0)) if mod_row is None else (lambda bi, ti: (mod_row, 0, 0))
    row = lambda n: pl.BlockSpec((1, tm, n), lambda bi, ti: (bi, ti, 0))
    bw = a.shape[-1]
    return pl.pallas_call(
        _merge_kernel,
        out_shape=jax.ShapeDtypeStruct(x.shape, F32),
        grid=(b, t // tm),
        in_specs=[row(d), pl.BlockSpec((1, N_MOD, d), mrow), _const_spec(g.shape),
                  row(bw), row(bw), row(bw), row(bw),
                  _const_spec(wg.shape), _const_spec(wb.shape), _const_spec(wglu.shape), _const_spec(bglu.shape),
                  _const_spec(wo.shape)],
        out_specs=row(d),
        scratch_shapes=[pltpu.VMEM((tm, d), F32)],
        compiler_params=_cparams("parallel", "parallel"),
        name="merge",
    )(x, mods, g, a, b_, c, d_, wg, wb, wglu, bglu, wo)


def _rope_table(n_tok):
    q = MLA_ROPE // 4
    t = jnp.arange(n_tok)
    pos = jnp.stack([t // GRID_W, t % GRID_W], axis=-1).astype(F32)
    inv = ROPE_BASE ** (-jnp.arange(q, dtype=F32) / q)
    ang = pos[:, :, None] * inv
    cos, sin = jnp.cos(ang), jnp.sin(ang)
    cos_t = jnp.stack([cos, cos], axis=2).reshape(n_tok, MLA_ROPE)
    sin_t = jnp.stack([-sin, sin], axis=2).reshape(n_tok, MLA_ROPE)
    return jnp.concatenate([cos_t, sin_t], axis=-1)


def _rope_swap_perm():
    q = MLA_ROPE // 4
    d = jnp.arange(MLA_ROPE)
    return jnp.where((d // q) % 2 == 0, d + q, d - q)


def _dft_mats(t):
    n = 2 * t
    f = jnp.arange(t, dtype=jnp.int32)
    ang = ((f[:, None] * f[None, :]) % n).astype(F32) * (2.0 * math.pi / n)
    return jnp.cos(ang).astype(BF16), jnp.sin(ang).astype(BF16)


def _hyena_consts(n_tok):
    bands = (HY_EMB - 1) // 2
    t = jnp.arange(n_tok, dtype=F32)
    t01 = jnp.linspace(0.0, 1.0, n_tok, dtype=F32)[:, None]
    ang = (2.0 * math.pi * t / n_tok)[:, None] * jnp.linspace(1e-4, bands - 1, bands, dtype=F32)
    z = jnp.concatenate([t01, jnp.cos(ang), -jnp.sin(ang)], axis=-1)
    z = jnp.pad(z, ((0, 0), (0, 128 - HY_EMB)))
    max_decay = math.log(HY_TARGET) / HY_FAST_PCT
    min_decay = math.log(HY_TARGET) / HY_SLOW_PCT
    deltas = jnp.abs(jnp.linspace(min_decay, max_decay, HY_W, dtype=F32))
    return z, jnp.exp(-t01 * deltas)


def _s5_operators(lam_re, lam_im, log_dt, b_re, b_im, c_re, c_im):
    tc = S5_CHUNK
    lam = lax.complex(jnp.minimum(lam_re, -1e-4), lam_im)
    lam_dt = lam * jnp.exp(log_dt)[..., None]
    lam_bar = jnp.exp(lam_dt)
    b_bar = ((lam_bar - 1.0) / lam)[..., None] * lax.complex(b_re, b_im)
    cm = lax.complex(c_re, c_im)
    pw = jnp.exp(lam_dt[..., None] * jnp.arange(tc + 1, dtype=F32))
    g = lam_re.shape[1]
    kern = jnp.real(jnp.einsum('dgap,dgpk,dgph->dkgah', cm, pw[..., :tc], b_bar,
                               precision=lax.Precision.HIGHEST))
    s = jnp.arange(tc)[:, None]
    j = jnp.arange(tc)[None, :]
    mf = jnp.where((j >= s)[:, :, None, None, None], kern[0][jnp.clip(j - s, 0, tc - 1)], 0.0)
    mb = jnp.where((s >= j)[:, :, None, None, None], kern[1][jnp.clip(s - j, 0, tc - 1)], 0.0)
    m = (mf + mb).transpose(2, 0, 4, 1, 3).reshape(g, S5_CW, S5_CW)
    pf = pw[0][..., tc - 1 - jnp.arange(tc)]
    pb = pw[1][..., :tc]
    wf = jnp.einsum('gps,gph->gshp', pf, b_bar[0]).reshape(g, S5_CW, S5_P)
    wb = jnp.einsum('gps,gph->gshp', pb, b_bar[1]).reshape(g, S5_CW, S5_P)
    w = jnp.concatenate([wf.real, wb.real, wf.imag, wb.imag], axis=-1)
    gf = jnp.einsum('gap,gpj->gpja', cm[0], pw[0][..., 1:tc + 1]).reshape(g, S5_P, S5_CW)
    gb = jnp.einsum('gap,gpj->gpja', cm[1], pw[1][..., tc - jnp.arange(tc)]).reshape(g, S5_P, S5_CW)
    v = jnp.concatenate([gf.real, gb.real, -gf.imag, -gb.imag], axis=1)
    a16 = pw[..., tc]
    lam16 = jnp.stack([jnp.concatenate([a16[0].real, a16[1].real], -1),
                       jnp.concatenate([a16[0].imag, a16[1].imag], -1)], axis=1)
    lam16 = jnp.pad(lam16, ((0, 0), (0, 6), (0, 0)))
    return m.astype(BF16), w.astype(BF16), v.astype(BF16), lam16


def _prep_layer(i, p):
    d = p['w_in'].shape[1]
    f = p['ffn_w_out'].shape[2]
    nc = f // FFN_CHUNK
    wi = p['ffn_w_in'][i].reshape(2, d, 2, nc, FFN_CHUNK).transpose(0, 2, 3, 1, 4).astype(BF16)
    wo = p['ffn_w_out'][i].reshape(2, nc, FFN_CHUNK, d).astype(BF16)
    w_in = p['w_in'][i]
    seg = lambda k: w_in[:, sum(IN_SPLITS[:k]):sum(IN_SPLITS[:k + 1])]
    kr = seg(1)
    w_proj = jnp.concatenate([seg(0), seg(5), kr, kr[:, _rope_swap_perm()], seg(6), seg(2), seg(3), seg(4), seg(7)],
                             axis=1).astype(BF16)
    w_gate = seg(8).astype(BF16)
    wuq = p['mla_w_uq'][i]
    rope = wuq[..., MLA_NOPE:]
    wq = jnp.concatenate([wuq[..., :MLA_NOPE], rope, rope[..., _rope_swap_perm()]], axis=-1)
    wq = wq.reshape(MLA_RANK, MLA_HEADS * MLA_KW).astype(BF16)
    wukv = p['mla_w_ukv'][i]
    wkv = jnp.concatenate([wukv[..., :MLA_NOPE].reshape(MLA_RANK, -1), wukv[..., MLA_NOPE:].reshape(MLA_RANK, -1)],
                          axis=1).astype(BF16)
    s5m, s5w, s5v, s5lam = _s5_operators(p['s5_lam_re'][i], p['s5_lam_im'][i], p['s5_log_dt'][i], p['s5_b_re'][i],
                                         p['s5_b_im'][i], p['s5_c_re'][i], p['s5_c_im'][i])
    g = p['s5_lam_re'].shape[2]
    s5d = jnp.tile(p['s5_d'][i].reshape(g, 1, S5_H), (1, 1, S5_CHUNK))
    return dict(
        g=p['norm_g'][i], wi=wi, wo=wo, w_proj=w_proj, w_gate=w_gate,
        gq=p['mla_g_q'][i].reshape(1, -1), gkv=p['mla_g_kv'][i].reshape(1, -1), wq=wq, wkv=wkv,
        bias=_na_bias_table(p['na_rpb'][i]),
        hy_mlp=(jnp.pad(p['hy_w1'][i], ((0, 128 - HY_EMB), (0, 0))), p['hy_b1'][i].reshape(1, -1),
                p['hy_freq1'][i].reshape(1, -1), p['hy_w2'][i], p['hy_b2'][i].reshape(1, -1),
                p['hy_freq2'][i].reshape(1, -1), p['hy_w3'][i]),
        conv_w=p['hy_conv_w'][i], conv_b=p['hy_conv_b'][i], hy_bias=p['hy_bias'][i],
        s5m=s5m, s5w=s5w, s5v=s5v, s5lam=s5lam, s5d=s5d,
        wglu=p['s5_w_glu'][i].astype(BF16), bglu=p['s5_b_glu'][i].reshape(1, -1),
        wb=p['w_branch'][i].astype(BF16), w_out=p['w_out'][i].astype(BF16),
    )


def _s5_branch(u_c, u_l, w):
    b, n_ctx, width = u_c.shape
    n_lat = u_l.shape[1]
    g = width // S5_H
    n_cc, n_lc = n_ctx // S5_CHUNK, n_lat // S5_CHUNK
    n_ch = n_cc + n_lc
    u = jnp.concatenate([u_c, u_l], axis=1).reshape(b, n_ch, S5_CHUNK, g, S5_H)
    u = u.transpose(3, 1, 0, 2, 4).reshape(g, n_ch * b, S5_CW)
    y = _s5_call(u, w['s5m'], w['s5w'], w['s5v'], w['s5lam'], w['s5d'], b, n_cc, n_lc)
    y = y.reshape(g, n_ch, b, S5_CHUNK, S5_H).transpose(2, 1, 3, 0, 4).reshape(b, n_ch * S5_CHUNK, width)
    return y[:, :n_ctx], y[:, n_ctx:]


def _mixer(xc, xl, mods, w, consts, ctx_out):
    b, n_lat, _ = xl.shape
    n_ctx = xc.shape[1]
    ctx_row = b
    mla_l, nq_l, nkv_l, u_l, hy_l = _inproj_call(xl, mods, None, w['g'], w['w_proj'])
    mla_c, nq_c, nkv_c, u_c, hy_c = _inproj_call(xc, mods, ctx_row, w['g'], w['w_proj'])

    scale = (MLA_NOPE + MLA_ROPE) ** -0.5
    rope_l = consts['rope_l']
    tabq_l = jnp.concatenate([jnp.ones((n_lat, MLA_NOPE), F32), rope_l], axis=-1) * scale
    a_l = _mla_call(mla_l, mla_c, mla_l, tabq_l, rope_l, w['gq'], w['gkv'], w['wq'], w['wkv'])
    d_l = _na_call(nq_l, nkv_c, nkv_l, w['bias'])
    cy_c, cy_l = _s5_branch(u_c, u_l, w)
    hre, him, hny = _hyfilt_call(consts['hy_z_l'], *w['hy_mlp'], consts['hy_dec_l'], *consts['dft_l'])
    b_l = _hyena_call(hy_l, w['conv_w'], w['conv_b'], w['hy_bias'], *consts['dft_l'], hre, him, hny)
    merge_w = (w['w_gate'], w['wb'], w['wglu'], w['bglu'], w['w_out'])
    xl = _merge_call(xl, mods, None, w['g'], a_l, b_l, cy_l, d_l, *merge_w)
    if not ctx_out:
        return None, xl

    tabq_c = jnp.concatenate([jnp.ones((n_ctx, MLA_NOPE + MLA_ROPE), F32), jnp.zeros((n_ctx, MLA_ROPE), F32)],
                             axis=-1) * scale
    a_c = _mla_call(mla_c, mla_c, None, tabq_c, None, w['gq'], w['gkv'], w['wq'], w['wkv'])
    d_c = _na_call(nq_c, nkv_c, None, None)
    hre, him, hny = _hyfilt_call(consts['hy_z_c'], *w['hy_mlp'], consts['hy_dec_c'], *consts['dft_c'])
    b_c = _hyena_call(hy_c, w['conv_w'], w['conv_b'], w['hy_bias'], *consts['dft_c'], hre, him, hny)
    xc = _merge_call(xc, mods, ctx_row, w['g'], a_c, b_c, cy_c, d_c, *merge_w)
    return xc, xl


def kernel(x, c, ctx, c_ctx, w_mod, b_mod, norm_g, ffn_w_in, ffn_w_out, w_in, mla_g_q, mla_g_kv, mla_w_uq, mla_w_ukv, na_rpb, hy_conv_w, hy_conv_b, hy_bias, hy_w1, hy_b1, hy_freq1, hy_w2, hy_b2, hy_freq2, hy_w3, s5_lam_re, s5_lam_im, s5_log_dt, s5_b_re, s5_b_im, s5_c_re, s5_c_im, s5_d, s5_w_glu, s5_b_glu, w_branch, w_out):
    p = dict(norm_g=norm_g, ffn_w_in=ffn_w_in, ffn_w_out=ffn_w_out, w_in=w_in, mla_g_q=mla_g_q, mla_g_kv=mla_g_kv,
             mla_w_uq=mla_w_uq, mla_w_ukv=mla_w_ukv, na_rpb=na_rpb, hy_conv_w=hy_conv_w, hy_conv_b=hy_conv_b,
             hy_bias=hy_bias, hy_w1=hy_w1, hy_b1=hy_b1, hy_freq1=hy_freq1, hy_w2=hy_w2, hy_b2=hy_b2,
             hy_freq2=hy_freq2, hy_w3=hy_w3, s5_lam_re=s5_lam_re, s5_lam_im=s5_lam_im, s5_log_dt=s5_log_dt,
             s5_b_re=s5_b_re, s5_b_im=s5_b_im, s5_c_re=s5_c_re, s5_c_im=s5_c_im, s5_d=s5_d, s5_w_glu=s5_w_glu,
             s5_b_glu=s5_b_glu, w_branch=w_branch, w_out=w_out)
    b, n_lat, d = x.shape
    n_ctx = ctx.shape[1]
    depth = w_mod.shape[0]
    assert b % 8 == 0 and n_lat % 256 == 0 and n_ctx % 16 == 0

    rows = -(-(b + 1) // 8) * 8
    acts = jnp.concatenate([c, c_ctx[None, :], jnp.zeros((rows - b - 1, d), F32)], axis=0)
    mods = _mod_call(acts, w_mod, b_mod).reshape(depth, rows, N_MOD, d)

    z_l, dec_l = _hyena_consts(n_lat)
    z_c, dec_c = _hyena_consts(n_ctx)
    consts = dict(rope_l=_rope_table(n_lat), dft_l=_dft_mats(n_lat), dft_c=_dft_mats(n_ctx),
                  hy_z_l=z_l, hy_dec_l=dec_l, hy_z_c=z_c, hy_dec_c=dec_c)

    xc, xl = ctx, x
    for i in range(depth):
        ctx_out = i < depth - 1
        w = _prep_layer(i, p)
        m = mods[i]
        xl = _ffn_call(xl, m, None, w['g'], w['wi'][0], w['wo'][0], 0, 0)
        xc = _ffn_call(xc, m, b, w['g'], w['wi'][0], w['wo'][0], 0, 0)
        xc, xl = _mixer(xc, xl, m, w, consts, ctx_out)
        xl = _ffn_call(xl, m, None, w['g'], w['wi'][1], w['wo'][1], 6, 4)
        if ctx_out:
            xc = _ffn_call(xc, m, b, w['g'], w['wi'][1], w['wo'][1], 6, 4)
    return xl
```

```python
import functools
import math

import jax
import jax.numpy as jnp
from jax import lax
from jax.experimental import pallas as pl
from jax.experimental.pallas import tpu as pltpu

F32 = jnp.float32
BF16 = jnp.bfloat16

EPS = 1e-6
GRID_W = 64
ROPE_BASE = 10000.0
N_MOD = 9
MLA_HEADS, MLA_NOPE, MLA_ROPE, MLA_V = 4, 128, 64, 128
MLA_RANK = 256
NA_HEADS, NA_DIM, NA_WIN_R, NA_WIN_C = 8, 64, 8, 16
HY_W, HY_EMB, HY_FILT = 512, 33, 64
HY_FAST_PCT, HY_SLOW_PCT, HY_TARGET = 0.3, 1.5, 1e-2
S5_H, S5_P = 16, 64
S5_CHUNK = 16
IN_SPLITS = (256, 64, 512, 512, 512, 256, 512, 1536, 4096)
FFN_CHUNK = 256
VMEM_LIMIT = 56 * 1024 * 1024


def _cparams(*sem):
    return pltpu.CompilerParams(dimension_semantics=sem, vmem_limit_bytes=VMEM_LIMIT)


def _const_spec(shape):
    nd = len(shape)
    return pl.BlockSpec(shape, lambda *_: (0,) * nd, pipeline_mode=pl.Buffered(1))


def _dot(a, b):
    return jnp.dot(a, b, preferred_element_type=F32)


def _dot_nt(a, b):
    return lax.dot_general(a, b, (((1,), (1,)), ((), ())), preferred_element_type=F32)


def _sigmoid(x):
    return 1.0 / (1.0 + jnp.exp(-x))


def _rms(x, g):
    return x * lax.rsqrt(jnp.mean(x * x, axis=-1, keepdims=True) + EPS) * g


def _modnorm(x, g, shift, scale):
    return _rms(x, g) * (1.0 + scale) + shift


def _mod_kernel(act_ref, w_ref, b_ref, o_ref):
    a = act_ref[...]
    a = a * _sigmoid(a)
    a_hi = a.astype(BF16)
    a_lo = (a - a_hi.astype(F32)).astype(BF16)
    w = w_ref[0]
    w_hi = w.astype(BF16)
    w_lo = (w - w_hi.astype(F32)).astype(BF16)
    o_ref[0] = _dot(a_hi, w_hi) + _dot(a_lo, w_hi) + _dot(a_hi, w_lo) + b_ref[0]


def _mod_call(acts, w_mod, b_mod):
    nl, d, nd = w_mod.shape
    rows = acts.shape[0]
    tn = 1152
    return pl.pallas_call(
        _mod_kernel,
        out_shape=jax.ShapeDtypeStruct((nl, rows, nd), F32),
        grid=(nl, nd // tn),
        in_specs=[pl.BlockSpec((rows, d), lambda l, j: (0, 0)),
                  pl.BlockSpec((1, d, tn), lambda l, j: (l, 0, j)),
                  pl.BlockSpec((1, 1, tn), lambda l, j: (l, 0, j))],
        out_specs=pl.BlockSpec((1, rows, tn), lambda l, j: (l, 0, j)),
        compiler_params=_cparams("parallel", "parallel"),
        name="mod",
    )(acts, w_mod, b_mod.reshape(nl, 1, nd))


def _ffn_kernel(x_ref, m_ref, g_ref, wi_ref, wo_ref, o_ref, acc_ref, *, base, gidx, n_chunks):
    x = x_ref[0]
    shift, scale, gate = (m_ref[0, base + k:base + k + 1, :] for k in range(3))
    h = _modnorm(x, g_ref[gidx:gidx + 1, :], shift, scale).astype(BF16)
    for j in range(n_chunks):
        a = _dot(h, wi_ref[0, j])
        b = _dot(h, wi_ref[1, j])
        act = (a * _sigmoid(a) * b).astype(BF16)
        part = _dot(act, wo_ref[j])
        if j == 0:
            acc_ref[...] = part
        else:
            acc_ref[...] += part
    y = acc_ref[...]
    o_ref[0] = x + 0.5 * gate * _rms(y, g_ref[gidx + 1:gidx + 2, :])


def _ffn_call(x, mods, mod_row, g, wi, wo, base, gidx):
    b, t, d = x.shape
    tm = min(512, t)
    n_chunks = wi.shape[1]
    mrow = (lambda bi, ti: (bi, 0, 0)) if mod_row is None else (lambda bi, ti: (mod_row, 0, 0))
    return pl.pallas_call(
        functools.partial(_ffn_kernel, base=base, gidx=gidx, n_chunks=n_chunks),
        out_shape=jax.ShapeDtypeStruct(x.shape, F32),
        grid=(b, t // tm),
        in_specs=[pl.BlockSpec((1, tm, d), lambda bi, ti: (bi, ti, 0)),
                  pl.BlockSpec((1, N_MOD, d), mrow),
                  _const_spec(g.shape), _const_spec(wi.shape), _const_spec(wo.shape)],
        out_specs=pl.BlockSpec((1, tm, d), lambda bi, ti: (bi, ti, 0)),
        scratch_shapes=[pltpu.VMEM((tm, d), F32)],
        compiler_params=_cparams("parallel", "parallel"),
        name="ffn",
    )(x, mods, g, wi, wo)


INPROJ_COLS = (640, 512, 1024, 512, 1536)


def _inproj_kernel(x_ref, m_ref, g_ref, w_ref, o_mla, o_nq, o_nkv, o_u, o_hy, u_ref):
    x = x_ref[0]
    tm = x.shape[0]
    h = _modnorm(x, g_ref[2:3, :], m_ref[0, 3:4, :], m_ref[0, 4:5, :]).astype(BF16)
    off = 0
    for o_ref, n in zip((o_mla, o_nq, o_nkv, o_u, o_hy), INPROJ_COLS):
        z = _dot(h, w_ref[:, off:off + n])
        if o_ref is o_u:
            for k in range(n // 128):
                u_ref[k] = z[:, k * 128:(k + 1) * 128]
            for j in range(S5_CHUNK):
                for k in range(n // 128):
                    o_ref[j, 0, :, k * 128:(k + 1) * 128] = u_ref[k, pl.ds(j, tm // S5_CHUNK, stride=S5_CHUNK), :]
        else:
            o_ref[0] = z.astype(o_ref.dtype)
        off += n


def _inproj_call(x, mods, mod_row, g, w):
    b, t, d = x.shape
    tm = min(512, t)
    mrow = (lambda bi, ti: (bi, 0, 0)) if mod_row is None else (lambda bi, ti: (mod_row, 0, 0))
    dts = (F32, BF16, BF16, F32, F32)
    out_shape = [jax.ShapeDtypeStruct((b, t, n), dt) for n, dt in zip(INPROJ_COLS, dts)]
    out_specs = [pl.BlockSpec((1, tm, n), lambda bi, ti: (bi, ti, 0)) for n in INPROJ_COLS]
    nu = INPROJ_COLS[3]
    out_shape[3] = jax.ShapeDtypeStruct((S5_CHUNK, b, t // S5_CHUNK, nu), F32)
    out_specs[3] = pl.BlockSpec((S5_CHUNK, 1, tm // S5_CHUNK, nu), lambda bi, ti: (0, bi, ti, 0))
    return pl.pallas_call(
        _inproj_kernel,
        out_shape=out_shape,
        grid=(b, t // tm),
        in_specs=[pl.BlockSpec((1, tm, d), lambda bi, ti: (bi, ti, 0)),
                  pl.BlockSpec((1, N_MOD, d), mrow),
                  _const_spec(g.shape), _const_spec(w.shape)],
        out_specs=out_specs,
        scratch_shapes=[pltpu.VMEM((nu // 128, tm, 128), F32)],
        compiler_params=_cparams("parallel", "parallel"),
        name="inproj",
    )(x, mods, g, w)


MLA_KW = 256


def _mla_kernel(*refs, n_ctx, n_lat, tq):
    if n_lat:
        zq_ref, zc_ref, zl_ref, tabq_ref, tabk_ref, gq_ref, gkv_ref, wq_ref, wkv_ref, o_ref, k_ref, v_ref = refs
    else:
        zq_ref, zc_ref, tabq_ref, gq_ref, gkv_ref, wq_ref, wkv_ref, o_ref, k_ref, v_ref = refs
    half = lax.broadcasted_iota(jnp.int32, (1, 128), 1) < MLA_ROPE

    def fill_kv(z, row0, tab):
        n = z.shape[0]
        kv = _dot(_rms(z[:, 0:MLA_RANK], gkv_ref[...]).astype(BF16), wkv_ref[...])
        r = z[:, 2 * MLA_RANK:2 * MLA_RANK + 128]
        if tab is None:
            rot = jnp.where(half, r, pltpu.roll(r, MLA_ROPE, 1))
        else:
            r = r * tab
            rot = r + pltpu.roll(r, MLA_ROPE, 1)
        rot = rot.astype(BF16)
        for h in range(MLA_HEADS):
            k_ref[h, pl.ds(row0, n), 0:MLA_NOPE] = kv[:, h * MLA_NOPE:(h + 1) * MLA_NOPE].astype(BF16)
            k_ref[h, pl.ds(row0, n), MLA_NOPE:MLA_KW] = rot
        v_ref[pl.ds(row0, n), :] = kv[:, MLA_HEADS * MLA_NOPE:].astype(BF16)

    @pl.when(pl.program_id(1) == 0)
    def _():
        fill_kv(zc_ref[0], 0, None)
        if n_lat:
            ck = 256

            def body(i, carry):
                r0 = pl.multiple_of(i * ck, ck)
                fill_kv(zl_ref[0, pl.ds(r0, ck), :], pl.multiple_of(n_ctx + r0, 16), tabk_ref[pl.ds(r0, ck), :])
                return carry

            lax.fori_loop(0, n_lat // ck, body, 0)

    zq = zq_ref[0]
    q = _dot(_rms(zq[:, MLA_RANK:2 * MLA_RANK], gq_ref[...]).astype(BF16), wq_ref[...])
    tabq = tabq_ref[...]
    for h in range(MLA_HEADS):
        qh = (q[:, h * MLA_KW:(h + 1) * MLA_KW] * tabq).astype(BF16)
        s = _dot_nt(qh, k_ref[h])
        m = jnp.max(s, axis=-1, keepdims=True)
        p = jnp.exp(s - m)
        l = jnp.sum(p, axis=-1, keepdims=True)
        o = _dot(p.astype(BF16), v_ref[:, h * MLA_V:(h + 1) * MLA_V])
        o_ref[0, :, h * MLA_V:(h + 1) * MLA_V] = (o / l).astype(o_ref.dtype)


def _mla_call(zq, zc, zl, tabq, tabk, gq, gkv, wq, wkv):
    b, t, w = zq.shape
    n_ctx = zc.shape[1]
    n_lat = 0 if zl is None else zl.shape[1]
    tq = min(256, t)
    n_keys = n_ctx + n_lat
    args = [zq, zc] + ([zl] if n_lat else []) + [tabq] + ([tabk] if n_lat else []) + [gq, gkv, wq, wkv]
    in_specs = [pl.BlockSpec((1, tq, w), lambda bi, qi: (bi, qi, 0)),
                pl.BlockSpec((1, n_ctx, w), lambda bi, qi: (bi, 0, 0))]
    if n_lat:
        in_specs.append(pl.BlockSpec((1, n_lat, w), lambda bi, qi: (bi, 0, 0)))
    in_specs.append(pl.BlockSpec((tq, MLA_KW), lambda bi, qi: (qi, 0)))
    if n_lat:
        in_specs.append(_const_spec(tabk.shape))
    in_specs += [_const_spec(gq.shape), _const_spec(gkv.shape), _const_spec(wq.shape), _const_spec(wkv.shape)]
    return pl.pallas_call(
        functools.partial(_mla_kernel, n_ctx=n_ctx, n_lat=n_lat, tq=tq),
        out_shape=jax.ShapeDtypeStruct((b, t, MLA_HEADS * MLA_V), BF16),
        grid=(b, t // tq),
        in_specs=in_specs,
        out_specs=pl.BlockSpec((1, tq, MLA_HEADS * MLA_V), lambda bi, qi: (bi, qi, 0)),
        scratch_shapes=[pltpu.VMEM((MLA_HEADS, n_keys, MLA_KW), BF16),
                        pltpu.VMEM((n_keys, MLA_HEADS * MLA_V), BF16)],
        compiler_params=_cparams("parallel", "arbitrary"),
        name="mla",
    )(*args)


NA_W = NA_HEADS * NA_DIM
NA_WIN = NA_WIN_R * GRID_W


def _na_kernel(*refs, local, rows):
    if local:
        q_ref, kvc_ref, kvl_ref, bias_ref, o_ref = refs
    else:
        q_ref, kvc_ref, o_ref = refs
    tq = q_ref.shape[1]
    lane_lo = lax.broadcasted_iota(jnp.int32, (1, 128), 1) < NA_DIM
    sub = GRID_W if local else tq
    wins, offs = [], []
    for rr in range(tq // sub if local else 0):
        r = pl.program_id(1) * (tq // sub) + rr
        start = jnp.clip(r - NA_WIN_R // 2, 0, rows - NA_WIN_R)
        offs.append(r - start)
        wins.append(pl.ds(pl.multiple_of(start * GRID_W, GRID_W), NA_WIN))
    scale = NA_DIM ** -0.5
    for j in range(NA_HEADS // 2):
        cols = slice(j * 128, (j + 1) * 128)
        vcols = slice(NA_W + j * 128, NA_W + (j + 1) * 128)
        k_c, v_c = kvc_ref[0, :, cols], kvc_ref[0, :, vcols]
        pieces = []
        for rr in range(tq // sub):
            qs = q_ref[0, rr * sub:(rr + 1) * sub, cols] * jnp.asarray(scale, BF16)
            pieces += [jnp.where(lane_lo, qs, jnp.zeros_like(qs)), jnp.where(lane_lo, jnp.zeros_like(qs), qs)]
        q_all = jnp.concatenate(pieces, axis=0)
        s_c = _dot_nt(q_all, k_c)
        m_c = jnp.max(s_c, axis=-1, keepdims=True)
        p_cs, p_ls, ls = [], [], []
        for rr in range(tq // sub):
            rs = slice(rr * 2 * sub, (rr + 1) * 2 * sub)
            m = m_c[rs]
            if local:
                bias = bias_ref[offs[rr], 2 * j:2 * j + 2].reshape(2 * sub, NA_WIN)
                s_l = _dot_nt(q_all[rs], kvl_ref[0, wins[rr], cols]) + bias
                m = jnp.maximum(m, jnp.max(s_l, axis=-1, keepdims=True))
            p_c = jnp.exp(s_c[rs] - m)
            l = jnp.sum(p_c, axis=-1, keepdims=True)
            if local:
                p_l = jnp.exp(s_l - m)
                l = l + jnp.sum(p_l, axis=-1, keepdims=True)
                p_ls.append(p_l.astype(BF16))
            p_cs.append(p_c.astype(BF16))
            ls.append(l)
        o_c = _dot(jnp.concatenate(p_cs, axis=0), v_c)
        for rr in range(tq // sub):
            rs = slice(rr * 2 * sub, (rr + 1) * 2 * sub)
            o = o_c[rs]
            if local:
                o = o + _dot(p_ls[rr], kvl_ref[0, wins[rr], vcols])
            o = o / ls[rr]
            o_ref[0, rr * sub:(rr + 1) * sub, cols] = jnp.where(lane_lo, o[:sub], o[sub:]).astype(o_ref.dtype)


NA_ROWS_PER_STEP = 4


def _nabias_kernel(r_ref, e_ref, m_ref, o_ref):
    o_ref[...] = jnp.dot(r_ref[...], e_ref[...], precision=lax.Precision.HIGHEST,
                         preferred_element_type=F32) + m_ref[...]


def _na_bias_table(rpb):
    h, na, nb = rpb.shape
    col = jnp.arange(GRID_W)
    c0 = jnp.clip(col - NA_WIN_C // 2, 0, GRID_W - NA_WIN_C)
    in_win = (col[None, :] >= c0[:, None]) & (col[None, :] < c0[:, None] + NA_WIN_C)
    dc = col[None, :] - col[:, None] + NA_WIN_C - 1
    onehot = (jnp.arange(128)[:, None, None] == dc[None]) & in_win[None]
    onehot = onehot.astype(F32).reshape(128, GRID_W * GRID_W)
    mask = jnp.where(in_win, 0.0, -jnp.inf).astype(F32).reshape(1, GRID_W * GRID_W)
    r = jnp.pad(rpb.reshape(h * na, nb), ((0, 128 - h * na), (0, 128 - nb)))
    t = pl.pallas_call(
        _nabias_kernel,
        out_shape=jax.ShapeDtypeStruct((128, GRID_W * GRID_W), F32),
        name="nabias",
    )(r, onehot, mask)
    t = t[:h * na].reshape(h, na, GRID_W, GRID_W)
    tabs = [t[:, NA_WIN_R - 1 - o:2 * NA_WIN_R - 1 - o].transpose(0, 2, 1, 3).reshape(h, GRID_W, NA_WIN)
            for o in range(NA_WIN_R)]
    return jnp.stack(tabs, axis=0)


def _na_call(q, kvc, kvl, bias):
    b, t, _ = q.shape
    n_ctx = kvc.shape[1]
    local = kvl is not None
    rows = t // GRID_W
    if local:
        assert rows >= NA_WIN_R and rows % NA_ROWS_PER_STEP == 0
    tq = GRID_W * NA_ROWS_PER_STEP if local else t
    args = [q, kvc] + ([kvl, bias] if local else [])
    in_specs = [pl.BlockSpec((1, tq, NA_W), lambda bi, ri: (bi, ri, 0)),
                pl.BlockSpec((1, n_ctx, 2 * NA_W), lambda bi, ri: (bi, 0, 0))]
    if local:
        in_specs += [pl.BlockSpec((1, t, 2 * NA_W), lambda bi, ri: (bi, 0, 0)), _const_spec(bias.shape)]
    return pl.pallas_call(
        functools.partial(_na_kernel, local=local, rows=rows),
        out_shape=jax.ShapeDtypeStruct((b, t, NA_W), BF16),
        grid=(b, t // tq),
        in_specs=in_specs,
        out_specs=pl.BlockSpec((1, tq, NA_W), lambda bi, ri: (bi, ri, 0)),
        compiler_params=_cparams("parallel", "parallel"),
        name="na",
    )(*args)


def _hyfilt_kernel(z_ref, w1_ref, b1_ref, f1_ref, w2_ref, b2_ref, f2_ref, w3_ref, dec_ref, cm_ref, sm_ref,
                   hre_ref, him_ref, hny_ref):
    hi = lax.Precision.HIGHEST
    t = z_ref.shape[0]
    n_fft = 2 * t
    h = jnp.sin(f1_ref[...] * (jnp.dot(z_ref[...], w1_ref[...], precision=hi, preferred_element_type=F32) + b1_ref[...]))
    h = jnp.sin(f2_ref[...] * (jnp.dot(h, w2_ref[...], precision=hi, preferred_element_type=F32) + b2_ref[...]))
    taps = jnp.dot(h, w3_ref[...], precision=hi, preferred_element_type=F32)
    dec = dec_ref[...]
    hf = taps[:, :HY_W] * dec
    hb = taps[:, HY_W:] * dec
    row = lax.broadcasted_iota(jnp.int32, (t, 1), 0)
    wgt = jnp.where(row == 0, 1.0 / n_fft, 2.0 / n_fft)
    hsum = hf + hb
    hre_ref[...] = _dot(cm_ref[...], hsum.astype(BF16)) * wgt
    him_ref[...] = _dot(sm_ref[...], (hb - hf).astype(BF16)) * wgt
    alt = (1 - 2 * (row & 1)).astype(F32)
    hny = jnp.sum(hsum * alt, axis=0, keepdims=True) * (1.0 / n_fft)
    hny_ref[...] = jnp.broadcast_to(hny, hny_ref.shape)


def _hyfilt_call(z, w1, b1, f1, w2, b2, f2, w3, dec, cm, sm):
    t = z.shape[0]
    args = (z, w1, b1, f1, w2, b2, f2, w3, dec, cm, sm)
    return pl.pallas_call(
        _hyfilt_kernel,
        out_shape=[jax.ShapeDtypeStruct((t, HY_W), F32), jax.ShapeDtypeStruct((t, HY_W), F32),
                   jax.ShapeDtypeStruct((8, HY_W), F32)],
        grid=(1,),
        in_specs=[_const_spec(a.shape) for a in args],
        out_specs=[pl.BlockSpec((t, HY_W), lambda i: (0, 0)), pl.BlockSpec((t, HY_W), lambda i: (0, 0)),
                   pl.BlockSpec((8, HY_W), lambda i: (0, 0))],
        compiler_params=_cparams("arbitrary"),
        name="hyfilt",
    )(*args)


HY_CB = 256
HY_FB = 512


def _hyena_kernel(zv_ref, z1_ref, z0_ref, wv_ref, w1_ref, w0_ref, bv_ref, b1_ref, b0_ref, bd_ref,
                  cm_ref, sm_ref, hre_ref, him_ref, hny_ref, o_ref, y_ref):
    t = zv_ref.shape[1]
    row = lax.broadcasted_iota(jnp.int32, (t, 1), 0)

    def conv3(z_ref, w_ref, b_ref):
        z = z_ref[0]
        prev = jnp.where(row == 0, 0.0, pltpu.roll(z, 1, 0))
        nxt = jnp.where(row == t - 1, 0.0, pltpu.roll(z, t - 1, 0))
        return w_ref[0:1, :] * prev + w_ref[1:2, :] * z + w_ref[2:3, :] * nxt + b_ref[...]

    s = conv3(zv_ref, wv_ref, bv_ref) * conv3(z1_ref, w1_ref, b1_ref)
    alt = (1 - 2 * (row & 1)).astype(F32)
    sb = s.astype(BF16)
    xny = jnp.sum(s * alt, axis=0, keepdims=True)
    y_ref[...] = alt * (xny * hny_ref[0:1, :]) + s * bd_ref[...]
    fb = min(HY_FB, t)
    for f0 in range(0, t, fb):
        fr = slice(f0, f0 + fb)
        a = _dot(cm_ref[fr, :], sb)
        bq = _dot(sm_ref[fr, :], sb)
        hre = hre_ref[fr, :]
        him = him_ref[fr, :]
        yc = (a * hre + bq * him).astype(BF16)
        ys = (bq * hre - a * him).astype(BF16)
        y_ref[...] += _dot(cm_ref[:, fr], yc) + _dot(sm_ref[:, fr], ys)
    o_ref[0] = (conv3(z0_ref, w0_ref, b0_ref) * y_ref[...]).astype(o_ref.dtype)


def _hyena_call(z, conv_w, conv_b, bias_d, cm, sm, hre, him, hny):
    b, t, _ = z.shape
    nb = HY_W // HY_CB
    zspec = lambda k: pl.BlockSpec((1, t, HY_CB), lambda bi, ci: (bi, 0, k * nb + ci))
    wspec = lambda k: pl.BlockSpec((3, HY_CB), lambda bi, ci: (0, k * nb + ci))
    bspec = lambda k: pl.BlockSpec((1, HY_CB), lambda bi, ci: (0, k * nb + ci))
    cspec = lambda rows: pl.BlockSpec((rows, HY_CB), lambda bi, ci: (0, ci))
    conv_b = conv_b.reshape(1, -1)
    return pl.pallas_call(
        _hyena_kernel,
        out_shape=jax.ShapeDtypeStruct((b, t, HY_W), BF16),
        grid=(b, nb),
        in_specs=[zspec(0), zspec(1), zspec(2), wspec(0), wspec(1), wspec(2), bspec(0), bspec(1), bspec(2),
                  cspec(1), _const_spec(cm.shape), _const_spec(sm.shape), cspec(t), cspec(t), cspec(8)],
        out_specs=pl.BlockSpec((1, t, HY_CB), lambda bi, ci: (bi, 0, ci)),
        scratch_shapes=[pltpu.VMEM((t, HY_CB), F32)],
        compiler_params=_cparams("parallel", "parallel"),
        name="hyena",
    )(z, z, z, conv_w, conv_w, conv_w, conv_b, conv_b, conv_b, bias_d.reshape(1, -1), cm, sm, hre, him, hny)


S5_CW = S5_CHUNK * S5_H


S5_GPS = 128 // S5_H
S5_NB = 4
S5_RB = 64


def _s5_kernel(xc_ref, xl_ref, m_ref, w_ref, v_ref, lam_ref, d_ref, yc_ref, yl_ref,
               u_ref, s_ref, xpf_ref, xpb_ref, y_ref, *, nb, n_cc, n_lc):
    parts = ((xc_ref, yc_ref, 0, n_cc), (xl_ref, yl_ref, nb * n_cc, n_lc))
    blk = lax.broadcasted_iota(jnp.int32, (1, 128), 1) // S5_H
    fwd = lax.broadcasted_iota(jnp.int32, (1, 128), 1) < S5_P

    def row_blocks(n_chunks):
        rb = min(S5_RB, n_chunks)
        return [(c0, rb) for c0 in range(0, n_chunks, rb)]

    def gather_rows(b, carry):
        for x_ref, _, base, n_chunks in parts:
            for c0, cn in row_blocks(n_chunks):
                xs = [x_ref[j, b, c0:c0 + cn, :] for j in range(S5_CHUNK)]
                r0 = pl.multiple_of(base + b * n_chunks + c0, 16)
                for g in range(S5_GPS):
                    halves = []
                    for half in range(2):
                        acc = None
                        for jj in range(S5_GPS):
                            piece = xs[half * S5_GPS + jj]
                            sh = ((jj - g) % S5_GPS) * S5_H
                            if sh:
                                piece = pltpu.roll(piece, sh, 1)
                            acc = piece if acc is None else jnp.where(blk == jj, piece, acc)
                        halves.append(acc)
                    u_ref[g, pl.ds(r0, cn), :] = jnp.concatenate(halves, axis=1).astype(BF16)
        return carry

    lax.fori_loop(0, nb, gather_rows, 0)

    for g in range(S5_GPS):
        s = _dot(u_ref[g], w_ref[g])
        s_ref[g, 0] = s[:, 0:128]
        s_ref[g, 1] = s[:, 128:256]

    def scan_part(carry, base, n_chunks):
        def body(k, carry):
            kb = n_chunks - 1 - k
            rf = pl.ds(base + k, nb, stride=n_chunks)
            rb = pl.ds(base + kb, nb, stride=n_chunks)
            new = []
            for g in range(S5_GPS):
                xa, xb = carry[2 * g], carry[2 * g + 1]
                ar, ai = lam_ref[g, 0:1, :], lam_ref[g, 1:2, :]
                xpf_ref[g, 0, rf, :] = xa
                xpf_ref[g, 1, rf, :] = xb
                xpb_ref[g, 0, rb, :] = xa
                xpb_ref[g, 1, rb, :] = xb
                in_a = jnp.where(fwd, s_ref[g, 0, rf, :], s_ref[g, 0, rb, :])
                in_b = jnp.where(fwd, s_ref[g, 1, rf, :], s_ref[g, 1, rb, :])
                new += [ar * xa - ai * xb + in_a, ar * xb + ai * xa + in_b]
            return tuple(new)

        return lax.fori_loop(0, n_chunks, body, carry)

    carry = tuple(jnp.zeros((nb, 128), F32) for _ in range(2 * S5_GPS))
    for _, _, base, n_chunks in parts:
        carry = scan_part(carry, base, n_chunks)

    for g in range(S5_GPS):
        xp = jnp.concatenate([jnp.where(fwd, xpf_ref[g, h], xpb_ref[g, h]) for h in range(2)], axis=1).astype(BF16)
        y_ref[g] = _dot(u_ref[g], m_ref[g]) + _dot(xp, v_ref[g])

    def scatter_rows(b, carry):
        for x_ref, o_ref, base, n_chunks in parts:
            for c0, cn in row_blocks(n_chunks):
                r0 = pl.multiple_of(base + b * n_chunks + c0, 16)
                ys = [y_ref[g, pl.ds(r0, cn), :] for g in range(S5_GPS)]
                for j in range(S5_CHUNK):
                    half, jj = divmod(j, S5_GPS)
                    acc = None
                    for g in range(S5_GPS):
                        piece = ys[g][:, half * 128:(half + 1) * 128]
                        sh = ((g - jj) % S5_GPS) * S5_H
                        if sh:
                            piece = pltpu.roll(piece, sh, 1)
                        acc = piece if acc is None else jnp.where(blk == g, piece, acc)
                    y = acc + d_ref[...] * x_ref[j, b, c0:c0 + cn, :]
                    cdf = 0.5 * (1.0 + jnp.tanh(math.sqrt(2.0 / math.pi) * (y + 0.044715 * (y * y * y))))
                    o_ref[j, b, c0:c0 + cn, :] = y * cdf
        return carry

    lax.fori_loop(0, nb, scatter_rows, 0)


def _s5_call(xc, xl, m, w, v, lam, d):
    _, b, n_cc, width = xc.shape
    n_lc = xl.shape[2]
    nb = min(S5_NB, b)
    rows = nb * (n_cc + n_lc)
    xspec = lambda n: pl.BlockSpec((S5_CHUNK, nb, n, 128), lambda mi, bi: (0, bi, 0, mi))
    gspec = lambda a: pl.BlockSpec((S5_GPS,) + a.shape[1:], lambda mi, bi: (mi, 0, 0))
    return pl.pallas_call(
        functools.partial(_s5_kernel, nb=nb, n_cc=n_cc, n_lc=n_lc),
        out_shape=[jax.ShapeDtypeStruct(xc.shape, F32), jax.ShapeDtypeStruct(xl.shape, F32)],
        grid=(width // 128, b // nb),
        in_specs=[xspec(n_cc), xspec(n_lc), gspec(m), gspec(w), gspec(v), gspec(lam),
                  pl.BlockSpec((1, 128), lambda mi, bi: (0, mi))],
        out_specs=[xspec(n_cc), xspec(n_lc)],
        scratch_shapes=[pltpu.VMEM((S5_GPS, rows, S5_CW), BF16), pltpu.VMEM((S5_GPS, 2, rows, 128), F32),
                        pltpu.VMEM((S5_GPS, 2, rows, 128), F32), pltpu.VMEM((S5_GPS, 2, rows, 128), F32),
                        pltpu.VMEM((S5_GPS, rows, S5_CW), F32)],
        compiler_params=_cparams("parallel", "parallel"),
        name="s5",
    )(xc, xl, m, w, v, lam, d)


def _merge_kernel(x_ref, m_ref, g_ref, a_ref, b_ref, c_ref, d_ref, wg_ref, wb_ref, wglu_ref, bglu_ref, wo_ref,
                  o_ref, acc_ref, cy_ref):
    x = x_ref[0]
    tm, d = x.shape
    h = _modnorm(x, g_ref[2:3, :], m_ref[0, 3:4, :], m_ref[0, 4:5, :]).astype(BF16)
    for j in range(S5_CHUNK):
        for k in range(cy_ref.shape[0]):
            cy_ref[k, pl.ds(j, tm // S5_CHUNK, stride=S5_CHUNK), :] = c_ref[j, 0, :, k * 128:(k + 1) * 128]
    cy = jnp.concatenate([cy_ref[k] for k in range(cy_ref.shape[0])], axis=1)
    glu = _dot(cy.astype(BF16), wglu_ref[...]) + bglu_ref[...]
    nw = glu.shape[-1] // 2
    c = (glu[:, :nw] * _sigmoid(glu[:, nw:])).astype(BF16)
    for n, br in enumerate((a_ref[0], b_ref[0], c, d_ref[0])):
        gate = _sigmoid(_dot(h, wg_ref[:, n * d:(n + 1) * d]))
        part = gate * _dot(br, wb_ref[n])
        if n == 0:
            acc_ref[...] = part
        else:
            acc_ref[...] += part
    y = _dot(acc_ref[...].astype(BF16), wo_ref[...])
    o_ref[0] = x + m_ref[0, 5:6, :] * _rms(y, g_ref[3:4, :])


def _merge_call(x, mods, mod_row, g, a, b_, c, d_, wg, wb, wglu, bglu, wo):
    b, t, d = x.shape
    tm = min(256, t)
    mrow = (lambda bi, ti: (bi, 0, 0)) if mod_row is None else (lambda bi, ti: (mod_row, 0, 0))
    row = lambda n: pl.BlockSpec((1, tm, n), lambda bi, ti: (bi, ti, 0))
    bw = a.shape[-1]
    return pl.pallas_call(
        _merge_kernel,
        out_shape=jax.ShapeDtypeStruct(x.shape, F32),
        grid=(b, t // tm),
        in_specs=[row(d), pl.BlockSpec((1, N_MOD, d), mrow), _const_spec(g.shape),
                  row(bw), row(bw),
                  pl.BlockSpec((S5_CHUNK, 1, tm // S5_CHUNK, bw), lambda bi, ti: (0, bi, ti, 0)), row(bw),
                  _const_spec(wg.shape), _const_spec(wb.shape), _const_spec(wglu.shape), _const_spec(bglu.shape),
                  _const_spec(wo.shape)],
        out_specs=row(d),
        scratch_shapes=[pltpu.VMEM((tm, d), F32), pltpu.VMEM((bw // 128, tm, 128), F32)],
        compiler_params=_cparams("parallel", "parallel"),
        name="merge",
    )(x, mods, g, a, b_, c, d_, wg, wb, wglu, bglu, wo)


def _rope_table(n_tok):
    q = MLA_ROPE // 4
    t = jnp.arange(n_tok)
    pos = jnp.stack([t // GRID_W, t % GRID_W], axis=-1).astype(F32)
    inv = ROPE_BASE ** (-jnp.arange(q, dtype=F32) / q)
    ang = pos[:, :, None] * inv
    cos, sin = jnp.cos(ang), jnp.sin(ang)
    cos_t = jnp.stack([cos, cos], axis=2).reshape(n_tok, MLA_ROPE)
    sin_t = jnp.stack([-sin, sin], axis=2).reshape(n_tok, MLA_ROPE)
    return jnp.concatenate([cos_t, sin_t], axis=-1)


def _rope_swap_perm():
    q = MLA_ROPE // 4
    d = jnp.arange(MLA_ROPE)
    return jnp.where((d // q) % 2 == 0, d + q, d - q)


def _dft_mats(t):
    n = 2 * t
    f = jnp.arange(t, dtype=jnp.int32)
    ang = ((f[:, None] * f[None, :]) % n).astype(F32) * (2.0 * math.pi / n)
    return jnp.cos(ang).astype(BF16), jnp.sin(ang).astype(BF16)


def _hyena_consts(n_tok):
    bands = (HY_EMB - 1) // 2
    t = jnp.arange(n_tok, dtype=F32)
    t01 = jnp.linspace(0.0, 1.0, n_tok, dtype=F32)[:, None]
    ang = (2.0 * math.pi * t / n_tok)[:, None] * jnp.linspace(1e-4, bands - 1, bands, dtype=F32)
    z = jnp.concatenate([t01, jnp.cos(ang), -jnp.sin(ang)], axis=-1)
    z = jnp.pad(z, ((0, 0), (0, 128 - HY_EMB)))
    max_decay = math.log(HY_TARGET) / HY_FAST_PCT
    min_decay = math.log(HY_TARGET) / HY_SLOW_PCT
    deltas = jnp.abs(jnp.linspace(min_decay, max_decay, HY_W, dtype=F32))
    return z, jnp.exp(-t01 * deltas)


def _s5_operators(lam_re, lam_im, log_dt, b_re, b_im, c_re, c_im):
    tc = S5_CHUNK
    lam = lax.complex(jnp.minimum(lam_re, -1e-4), lam_im)
    lam_dt = lam * jnp.exp(log_dt)[..., None]
    lam_bar = jnp.exp(lam_dt)
    b_bar = ((lam_bar - 1.0) / lam)[..., None] * lax.complex(b_re, b_im)
    cm = lax.complex(c_re, c_im)
    pw = jnp.exp(lam_dt[..., None] * jnp.arange(tc + 1, dtype=F32))
    g = lam_re.shape[1]
    kern = jnp.real(jnp.einsum('dgap,dgpk,dgph->dkgah', cm, pw[..., :tc], b_bar,
                               precision=lax.Precision.HIGHEST))
    s = jnp.arange(tc)[:, None]
    j = jnp.arange(tc)[None, :]
    mf = jnp.where((j >= s)[:, :, None, None, None], kern[0][jnp.clip(j - s, 0, tc - 1)], 0.0)
    mb = jnp.where((s >= j)[:, :, None, None, None], kern[1][jnp.clip(s - j, 0, tc - 1)], 0.0)
    m = (mf + mb).transpose(2, 0, 4, 1, 3).reshape(g, S5_CW, S5_CW)
    pf = pw[0][..., tc - 1 - jnp.arange(tc)]
    pb = pw[1][..., :tc]
    wf = jnp.einsum('gps,gph->gshp', pf, b_bar[0]).reshape(g, S5_CW, S5_P)
    wb = jnp.einsum('gps,gph->gshp', pb, b_bar[1]).reshape(g, S5_CW, S5_P)
    w = jnp.concatenate([wf.real, wb.real, wf.imag, wb.imag], axis=-1)
    gf = jnp.einsum('gap,gpj->gpja', cm[0], pw[0][..., 1:tc + 1]).reshape(g, S5_P, S5_CW)
    gb = jnp.einsum('gap,gpj->gpja', cm[1], pw[1][..., tc - jnp.arange(tc)]).reshape(g, S5_P, S5_CW)
    v = jnp.concatenate([gf.real, gb.real, -gf.imag, -gb.imag], axis=1)
    a16 = pw[..., tc]
    lam16 = jnp.stack([jnp.concatenate([a16[0].real, a16[1].real], -1),
                       jnp.concatenate([a16[0].imag, a16[1].imag], -1)], axis=1)
    lam16 = jnp.pad(lam16, ((0, 0), (0, 6), (0, 0)))
    return m.astype(BF16), w.astype(BF16), v.astype(BF16), lam16


def _prep_layer(i, p):
    d = p['w_in'].shape[1]
    f = p['ffn_w_out'].shape[2]
    nc = f // FFN_CHUNK
    wi = p['ffn_w_in'][i].reshape(2, d, 2, nc, FFN_CHUNK).transpose(0, 2, 3, 1, 4).astype(BF16)
    wo = p['ffn_w_out'][i].reshape(2, nc, FFN_CHUNK, d).astype(BF16)
    w_in = p['w_in'][i]
    seg = lambda k: w_in[:, sum(IN_SPLITS[:k]):sum(IN_SPLITS[:k + 1])]
    kr = seg(1)
    w_proj = jnp.concatenate([seg(0), seg(5), kr, kr[:, _rope_swap_perm()], seg(6), seg(2), seg(3), seg(4), seg(7)],
                             axis=1).astype(BF16)
    w_gate = seg(8).astype(BF16)
    wuq = p['mla_w_uq'][i]
    rope = wuq[..., MLA_NOPE:]
    wq = jnp.concatenate([wuq[..., :MLA_NOPE], rope, rope[..., _rope_swap_perm()]], axis=-1)
    wq = wq.reshape(MLA_RANK, MLA_HEADS * MLA_KW).astype(BF16)
    wukv = p['mla_w_ukv'][i]
    wkv = jnp.concatenate([wukv[..., :MLA_NOPE].reshape(MLA_RANK, -1), wukv[..., MLA_NOPE:].reshape(MLA_RANK, -1)],
                          axis=1).astype(BF16)
    s5m, s5w, s5v, s5lam = _s5_operators(p['s5_lam_re'][i], p['s5_lam_im'][i], p['s5_log_dt'][i], p['s5_b_re'][i],
                                         p['s5_b_im'][i], p['s5_c_re'][i], p['s5_c_im'][i])
    s5d = p['s5_d'][i].reshape(1, -1)
    return dict(
        g=p['norm_g'][i], wi=wi, wo=wo, w_proj=w_proj, w_gate=w_gate,
        gq=p['mla_g_q'][i].reshape(1, -1), gkv=p['mla_g_kv'][i].reshape(1, -1), wq=wq, wkv=wkv,
        bias=_na_bias_table(p['na_rpb'][i]),
        hy_mlp=(jnp.pad(p['hy_w1'][i], ((0, 128 - HY_EMB), (0, 0))), p['hy_b1'][i].reshape(1, -1),
                p['hy_freq1'][i].reshape(1, -1), p['hy_w2'][i], p['hy_b2'][i].reshape(1, -1),
                p['hy_freq2'][i].reshape(1, -1), p['hy_w3'][i]),
        conv_w=p['hy_conv_w'][i], conv_b=p['hy_conv_b'][i], hy_bias=p['hy_bias'][i],
        s5m=s5m, s5w=s5w, s5v=s5v, s5lam=s5lam, s5d=s5d,
        wglu=p['s5_w_glu'][i].astype(BF16), bglu=p['s5_b_glu'][i].reshape(1, -1),
        wb=p['w_branch'][i].astype(BF16), w_out=p['w_out'][i].astype(BF16),
    )


def _mixer(xc, xl, mods, w, consts, ctx_out):
    b, n_lat, _ = xl.shape
    n_ctx = xc.shape[1]
    ctx_row = b
    mla_l, nq_l, nkv_l, u_l, hy_l = _inproj_call(xl, mods, None, w['g'], w['w_proj'])
    mla_c, nq_c, nkv_c, u_c, hy_c = _inproj_call(xc, mods, ctx_row, w['g'], w['w_proj'])

    scale = (MLA_NOPE + MLA_ROPE) ** -0.5
    rope_l = consts['rope_l']
    tabq_l = jnp.concatenate([jnp.ones((n_lat, MLA_NOPE), F32), rope_l], axis=-1) * scale
    a_l = _mla_call(mla_l, mla_c, mla_l, tabq_l, rope_l, w['gq'], w['gkv'], w['wq'], w['wkv'])
    d_l = _na_call(nq_l, nkv_c, nkv_l, w['bias'])
    cy_c, cy_l = _s5_call(u_c, u_l, w['s5m'], w['s5w'], w['s5v'], w['s5lam'], w['s5d'])
    hre, him, hny = _hyfilt_call(consts['hy_z_l'], *w['hy_mlp'], consts['hy_dec_l'], *consts['dft_l'])
    b_l = _hyena_call(hy_l, w['conv_w'], w['conv_b'], w['hy_bias'], *consts['dft_l'], hre, him, hny)
    merge_w = (w['w_gate'], w['wb'], w['wglu'], w['bglu'], w['w_out'])
    xl = _merge_call(xl, mods, None, w['g'], a_l, b_l, cy_l, d_l, *merge_w)
    if not ctx_out:
        return None, xl

    tabq_c = jnp.concatenate([jnp.ones((n_ctx, MLA_NOPE + MLA_ROPE), F32), jnp.zeros((n_ctx, MLA_ROPE), F32)],
                             axis=-1) * scale
    a_c = _mla_call(mla_c, mla_c, None, tabq_c, None, w['gq'], w['gkv'], w['wq'], w['wkv'])
    d_c = _na_call(nq_c, nkv_c, None, None)
    hre, him, hny = _hyfilt_call(consts['hy_z_c'], *w['hy_mlp'], consts['hy_dec_c'], *consts['dft_c'])
    b_c = _hyena_call(hy_c, w['conv_w'], w['conv_b'], w['hy_bias'], *consts['dft_c'], hre, him, hny)
    xc = _merge_call(xc, mods, ctx_row, w['g'], a_c, b_c, cy_c, d_c, *merge_w)
    return xc, xl


def kernel(x, c, ctx, c_ctx, w_mod, b_mod, norm_g, ffn_w_in, ffn_w_out, w_in, mla_g_q, mla_g_kv, mla_w_uq, mla_w_ukv, na_rpb, hy_conv_w, hy_conv_b, hy_bias, hy_w1, hy_b1, hy_freq1, hy_w2, hy_b2, hy_freq2, hy_w3, s5_lam_re, s5_lam_im, s5_log_dt, s5_b_re, s5_b_im, s5_c_re, s5_c_im, s5_d, s5_w_glu, s5_b_glu, w_branch, w_out):
    p = dict(norm_g=norm_g, ffn_w_in=ffn_w_in, ffn_w_out=ffn_w_out, w_in=w_in, mla_g_q=mla_g_q, mla_g_kv=mla_g_kv,
             mla_w_uq=mla_w_uq, mla_w_ukv=mla_w_ukv, na_rpb=na_rpb, hy_conv_w=hy_conv_w, hy_conv_b=hy_conv_b,
             hy_bias=hy_bias, hy_w1=hy_w1, hy_b1=hy_b1, hy_freq1=hy_freq1, hy_w2=hy_w2, hy_b2=hy_b2,
             hy_freq2=hy_freq2, hy_w3=hy_w3, s5_lam_re=s5_lam_re, s5_lam_im=s5_lam_im, s5_log_dt=s5_log_dt,
             s5_b_re=s5_b_re, s5_b_im=s5_b_im, s5_c_re=s5_c_re, s5_c_im=s5_c_im, s5_d=s5_d, s5_w_glu=s5_w_glu,
             s5_b_glu=s5_b_glu, w_branch=w_branch, w_out=w_out)
    b, n_lat, d = x.shape
    n_ctx = ctx.shape[1]
    depth = w_mod.shape[0]
    assert b % 8 == 0 and n_lat % 256 == 0 and n_ctx % 256 == 0

    rows = -(-(b + 1) // 8) * 8
    acts = jnp.concatenate([c, c_ctx[None, :], jnp.zeros((rows - b - 1, d), F32)], axis=0)
    mods = _mod_call(acts, w_mod, b_mod).reshape(depth, rows, N_MOD, d)

    z_l, dec_l = _hyena_consts(n_lat)
    z_c, dec_c = _hyena_consts(n_ctx)
    consts = dict(rope_l=_rope_table(n_lat), dft_l=_dft_mats(n_lat), dft_c=_dft_mats(n_ctx),
                  hy_z_l=z_l, hy_dec_l=dec_l, hy_z_c=z_c, hy_dec_c=dec_c)

    xc, xl = ctx, x
    for i in range(depth):
        ctx_out = i < depth - 1
        w = _prep_layer(i, p)
        m = mods[i]
        xl = _ffn_call(xl, m, None, w['g'], w['wi'][0], w['wo'][0], 0, 0)
        xc = _ffn_call(xc, m, b, w['g'], w['wi'][0], w['wo'][0], 0, 0)
        xc, xl = _mixer(xc, xl, m, w, consts, ctx_out)
        xl = _ffn_call(xl, m, None, w['g'], w['wi'][1], w['wo'][1], 6, 4)
        if ctx_out:
            xc = _ffn_call(xc, m, b, w['g'], w['wi'][1], w['wo'][1], 6, 4)
    return xl
```

```python
import functools
import math

import jax
import jax.numpy as jnp
from jax import lax
from jax.experimental import pallas as pl
from jax.experimental.pallas import tpu as pltpu

F32 = jnp.float32
BF16 = jnp.bfloat16

EPS = 1e-6
GRID_W = 64
ROPE_BASE = 10000.0
N_MOD = 9
MLA_HEADS, MLA_NOPE, MLA_ROPE, MLA_V = 4, 128, 64, 128
MLA_RANK = 256
NA_HEADS, NA_DIM, NA_WIN_R, NA_WIN_C = 8, 64, 8, 16
HY_W, HY_EMB, HY_FILT = 512, 33, 64
HY_FAST_PCT, HY_SLOW_PCT, HY_TARGET = 0.3, 1.5, 1e-2
S5_H, S5_P = 16, 64
S5_CHUNK = 16
IN_SPLITS = (256, 64, 512, 512, 512, 256, 512, 1536, 4096)
FFN_CHUNK = 256
VMEM_LIMIT = 56 * 1024 * 1024


def _cparams(*sem):
    return pltpu.CompilerParams(dimension_semantics=sem, vmem_limit_bytes=VMEM_LIMIT)


def _const_spec(shape):
    nd = len(shape)
    return pl.BlockSpec(shape, lambda *_: (0,) * nd, pipeline_mode=pl.Buffered(1))


def _dot(a, b):
    return jnp.dot(a, b, preferred_element_type=F32)


def _dot_nt(a, b):
    return lax.dot_general(a, b, (((1,), (1,)), ((), ())), preferred_element_type=F32)


def _sigmoid(x):
    return 1.0 / (1.0 + jnp.exp(-x))


def _rms(x, g):
    return x * lax.rsqrt(jnp.mean(x * x, axis=-1, keepdims=True) + EPS) * g


def _modnorm(x, g, shift, scale):
    return _rms(x, g) * (1.0 + scale) + shift


def _mod_kernel(act_ref, w_ref, b_ref, o_ref):
    a = act_ref[...]
    a = a * _sigmoid(a)
    a_hi = a.astype(BF16)
    a_lo = (a - a_hi.astype(F32)).astype(BF16)
    w = w_ref[0]
    w_hi = w.astype(BF16)
    w_lo = (w - w_hi.astype(F32)).astype(BF16)
    o_ref[0] = _dot(a_hi, w_hi) + _dot(a_lo, w_hi) + _dot(a_hi, w_lo) + b_ref[0]


def _mod_call(acts, w_mod, b_mod):
    nl, d, nd = w_mod.shape
    rows = acts.shape[0]
    tn = 1152
    return pl.pallas_call(
        _mod_kernel,
        out_shape=jax.ShapeDtypeStruct((nl, rows, nd), F32),
        grid=(nl, nd // tn),
        in_specs=[pl.BlockSpec((rows, d), lambda l, j: (0, 0)),
                  pl.BlockSpec((1, d, tn), lambda l, j: (l, 0, j)),
                  pl.BlockSpec((1, 1, tn), lambda l, j: (l, 0, j))],
        out_specs=pl.BlockSpec((1, rows, tn), lambda l, j: (l, 0, j)),
        compiler_params=_cparams("parallel", "parallel"),
        name="mod",
    )(acts, w_mod, b_mod.reshape(nl, 1, nd))


def _ffn_kernel(x_ref, m_ref, g_ref, wi_ref, wo_ref, o_ref, acc_ref, *, base, gidx, n_chunks):
    x = x_ref[0]
    shift, scale, gate = (m_ref[0, base + k:base + k + 1, :] for k in range(3))
    h = _modnorm(x, g_ref[gidx:gidx + 1, :], shift, scale).astype(BF16)
    f = n_chunks * FFN_CHUNK
    for j in range(n_chunks):
        cs = slice(j * FFN_CHUNK, (j + 1) * FFN_CHUNK)
        a = _dot(h, wi_ref[:, cs])
        b = _dot(h, wi_ref[:, f + j * FFN_CHUNK:f + (j + 1) * FFN_CHUNK])
        act = (a * _sigmoid(a) * b).astype(BF16)
        part = _dot(act, wo_ref[cs, :])
        if j == 0:
            acc_ref[...] = part
        else:
            acc_ref[...] += part
    y = acc_ref[...]
    o_ref[0] = x + 0.5 * gate * _rms(y, g_ref[gidx + 1:gidx + 2, :])


def _ffn_call(x, mods, mod_row, g, wi, wo, base, gidx):
    b, t, d = x.shape
    tm = min(512, t)
    n_chunks = wo.shape[0] // FFN_CHUNK
    mrow = (lambda bi, ti: (bi, 0, 0)) if mod_row is None else (lambda bi, ti: (mod_row, 0, 0))
    return pl.pallas_call(
        functools.partial(_ffn_kernel, base=base, gidx=gidx, n_chunks=n_chunks),
        out_shape=jax.ShapeDtypeStruct(x.shape, F32),
        grid=(b, t // tm),
        in_specs=[pl.BlockSpec((1, tm, d), lambda bi, ti: (bi, ti, 0)),
                  pl.BlockSpec((1, N_MOD, d), mrow),
                  _const_spec(g.shape), _const_spec(wi.shape), _const_spec(wo.shape)],
        out_specs=pl.BlockSpec((1, tm, d), lambda bi, ti: (bi, ti, 0)),
        scratch_shapes=[pltpu.VMEM((tm, d), F32)],
        compiler_params=_cparams("parallel", "parallel"),
        name="ffn",
    )(x, mods, g, wi, wo)


INPROJ_COLS = (640, 512, 1024, 512, 1536)


def _inproj_kernel(x_ref, m_ref, g_ref, w_ref, o_mla, o_nq, o_nkv, o_u, o_hy, u_ref):
    x = x_ref[0]
    tm = x.shape[0]
    h = _modnorm(x, g_ref[2:3, :], m_ref[0, 3:4, :], m_ref[0, 4:5, :]).astype(BF16)
    off = 0
    for o_ref, n in zip((o_mla, o_nq, o_nkv, o_u, o_hy), INPROJ_COLS):
        z = _dot(h, w_ref[:, off:off + n])
        if o_ref is o_u:
            for k in range(n // 128):
                u_ref[k] = z[:, k * 128:(k + 1) * 128]
            for j in range(S5_CHUNK):
                for k in range(n // 128):
                    o_ref[j, 0, :, k * 128:(k + 1) * 128] = u_ref[k, pl.ds(j, tm // S5_CHUNK, stride=S5_CHUNK), :]
        else:
            o_ref[0] = z.astype(o_ref.dtype)
        off += n


def _inproj_call(x, mods, mod_row, g, w):
    b, t, d = x.shape
    tm = min(512, t)
    mrow = (lambda bi, ti: (bi, 0, 0)) if mod_row is None else (lambda bi, ti: (mod_row, 0, 0))
    dts = (F32, BF16, BF16, F32, F32)
    out_shape = [jax.ShapeDtypeStruct((b, t, n), dt) for n, dt in zip(INPROJ_COLS, dts)]
    out_specs = [pl.BlockSpec((1, tm, n), lambda bi, ti: (bi, ti, 0)) for n in INPROJ_COLS]
    nu = INPROJ_COLS[3]
    out_shape[3] = jax.ShapeDtypeStruct((S5_CHUNK, b, t // S5_CHUNK, nu), F32)
    out_specs[3] = pl.BlockSpec((S5_CHUNK, 1, tm // S5_CHUNK, nu), lambda bi, ti: (0, bi, ti, 0))
    return pl.pallas_call(
        _inproj_kernel,
        out_shape=out_shape,
        grid=(b, t // tm),
        in_specs=[pl.BlockSpec((1, tm, d), lambda bi, ti: (bi, ti, 0)),
                  pl.BlockSpec((1, N_MOD, d), mrow),
                  _const_spec(g.shape), _const_spec(w.shape)],
        out_specs=out_specs,
        scratch_shapes=[pltpu.VMEM((nu // 128, tm, 128), F32)],
        compiler_params=_cparams("parallel", "parallel"),
        name="inproj",
    )(x, mods, g, w)


MLA_KW = 256


def _mla_kernel(*refs, n_ctx, n_lat, tq):
    if n_lat:
        zq_ref, zc_ref, zl_ref, tabq_ref, tabk_ref, gq_ref, gkv_ref, wq_ref, wkv_ref, o_ref, k_ref, v_ref = refs
    else:
        zq_ref, zc_ref, tabq_ref, gq_ref, gkv_ref, wq_ref, wkv_ref, o_ref, k_ref, v_ref = refs
    half = lax.broadcasted_iota(jnp.int32, (1, 128), 1) < MLA_ROPE

    def fill_kv(z, row0, tab):
        n = z.shape[0]
        kv = _dot(_rms(z[:, 0:MLA_RANK], gkv_ref[...]).astype(BF16), wkv_ref[...])
        r = z[:, 2 * MLA_RANK:2 * MLA_RANK + 128]
        if tab is None:
            rot = jnp.where(half, r, pltpu.roll(r, MLA_ROPE, 1))
        else:
            r = r * tab
            rot = r + pltpu.roll(r, MLA_ROPE, 1)
        rot = rot.astype(BF16)
        for h in range(MLA_HEADS):
            k_ref[h, pl.ds(row0, n), 0:MLA_NOPE] = kv[:, h * MLA_NOPE:(h + 1) * MLA_NOPE].astype(BF16)
            k_ref[h, pl.ds(row0, n), MLA_NOPE:MLA_KW] = rot
        v_ref[pl.ds(row0, n), :] = kv[:, MLA_HEADS * MLA_NOPE:].astype(BF16)

    @pl.when(pl.program_id(1) == 0)
    def _():
        fill_kv(zc_ref[0], 0, None)
        if n_lat:
            ck = 256

            def body(i, carry):
                r0 = pl.multiple_of(i * ck, ck)
                fill_kv(zl_ref[0, pl.ds(r0, ck), :], pl.multiple_of(n_ctx + r0, 16), tabk_ref[pl.ds(r0, ck), :])
                return carry

            lax.fori_loop(0, n_lat // ck, body, 0)

    zq = zq_ref[0]
    q = _dot(_rms(zq[:, MLA_RANK:2 * MLA_RANK], gq_ref[...]).astype(BF16), wq_ref[...])
    tabq = tabq_ref[...]
    for h in range(MLA_HEADS):
        qh = (q[:, h * MLA_KW:(h + 1) * MLA_KW] * tabq).astype(BF16)
        s = _dot_nt(qh, k_ref[h])
        m = jnp.max(s, axis=-1, keepdims=True)
        p = jnp.exp2(s - m)
        l = jnp.sum(p, axis=-1, keepdims=True)
        o = _dot(p.astype(BF16), v_ref[:, h * MLA_V:(h + 1) * MLA_V])
        o_ref[0, :, h * MLA_V:(h + 1) * MLA_V] = (o / l).astype(o_ref.dtype)


def _mla_call(zq, zc, zl, tabq, tabk, gq, gkv, wq, wkv):
    b, t, w = zq.shape
    n_ctx = zc.shape[1]
    n_lat = 0 if zl is None else zl.shape[1]
    tq = min(512, t)
    n_keys = n_ctx + n_lat
    args = [zq, zc] + ([zl] if n_lat else []) + [tabq] + ([tabk] if n_lat else []) + [gq, gkv, wq, wkv]
    in_specs = [pl.BlockSpec((1, tq, w), lambda bi, qi: (bi, qi, 0)),
                pl.BlockSpec((1, n_ctx, w), lambda bi, qi: (bi, 0, 0))]
    if n_lat:
        in_specs.append(pl.BlockSpec((1, n_lat, w), lambda bi, qi: (bi, 0, 0)))
    in_specs.append(pl.BlockSpec((tq, MLA_KW), lambda bi, qi: (qi, 0)))
    if n_lat:
        in_specs.append(_const_spec(tabk.shape))
    in_specs += [_const_spec(gq.shape), _const_spec(gkv.shape), _const_spec(wq.shape), _const_spec(wkv.shape)]
    return pl.pallas_call(
        functools.partial(_mla_kernel, n_ctx=n_ctx, n_lat=n_lat, tq=tq),
        out_shape=jax.ShapeDtypeStruct((b, t, MLA_HEADS * MLA_V), BF16),
        grid=(b, t // tq),
        in_specs=in_specs,
        out_specs=pl.BlockSpec((1, tq, MLA_HEADS * MLA_V), lambda bi, qi: (bi, qi, 0)),
        scratch_shapes=[pltpu.VMEM((MLA_HEADS, n_keys, MLA_KW), BF16),
                        pltpu.VMEM((n_keys, MLA_HEADS * MLA_V), BF16)],
        compiler_params=_cparams("parallel", "arbitrary"),
        name="mla",
    )(*args)


NA_W = NA_HEADS * NA_DIM
NA_WIN = NA_WIN_R * GRID_W


def _na_kernel(*refs, local, rows):
    if local:
        q_ref, kvc_ref, kvl_ref, bias_ref, o_ref = refs
    else:
        q_ref, kvc_ref, o_ref = refs
    tq = q_ref.shape[1]
    lane_lo = lax.broadcasted_iota(jnp.int32, (1, 128), 1) < NA_DIM
    sub = GRID_W if local else tq
    wins, offs = [], []
    for rr in range(tq // sub if local else 0):
        r = pl.program_id(1) * (tq // sub) + rr
        start = jnp.clip(r - NA_WIN_R // 2, 0, rows - NA_WIN_R)
        offs.append(r - start)
        wins.append(pl.ds(pl.multiple_of(start * GRID_W, GRID_W), NA_WIN))
    scale = NA_DIM ** -0.5
    for j in range(NA_HEADS // 2):
        cols = slice(j * 128, (j + 1) * 128)
        vcols = slice(NA_W + j * 128, NA_W + (j + 1) * 128)
        k_c, v_c = kvc_ref[0, :, cols], kvc_ref[0, :, vcols]
        pieces = []
        for rr in range(tq // sub):
            qs = q_ref[0, rr * sub:(rr + 1) * sub, cols] * jnp.asarray(scale, BF16)
            pieces += [jnp.where(lane_lo, qs, jnp.zeros_like(qs)), jnp.where(lane_lo, jnp.zeros_like(qs), qs)]
        q_all = jnp.concatenate(pieces, axis=0)
        s_c = _dot_nt(q_all, k_c)
        m_c = jnp.max(s_c, axis=-1, keepdims=True)
        p_cs, p_ls, ls = [], [], []
        for rr in range(tq // sub):
            rs = slice(rr * 2 * sub, (rr + 1) * 2 * sub)
            m = m_c[rs]
            if local:
                bias = bias_ref[offs[rr], 2 * j:2 * j + 2].reshape(2 * sub, NA_WIN)
                s_l = _dot_nt(q_all[rs], kvl_ref[0, wins[rr], cols]) + bias
                m = jnp.maximum(m, jnp.max(s_l, axis=-1, keepdims=True))
            p_c = jnp.exp(s_c[rs] - m)
            l = jnp.sum(p_c, axis=-1, keepdims=True)
            if local:
                p_l = jnp.exp(s_l - m)
                l = l + jnp.sum(p_l, axis=-1, keepdims=True)
                p_ls.append(p_l.astype(BF16))
            p_cs.append(p_c.astype(BF16))
            ls.append(l)
        o_c = _dot(jnp.concatenate(p_cs, axis=0), v_c)
        for rr in range(tq // sub):
            rs = slice(rr * 2 * sub, (rr + 1) * 2 * sub)
            o = o_c[rs]
            if local:
                o = o + _dot(p_ls[rr], kvl_ref[0, wins[rr], vcols])
            o = o / ls[rr]
            o_ref[0, rr * sub:(rr + 1) * sub, cols] = jnp.where(lane_lo, o[:sub], o[sub:]).astype(o_ref.dtype)


NA_ROWS_PER_STEP = 4


def _nabias_kernel(r_ref, e_ref, m_ref, o_ref):
    o_ref[...] = jnp.dot(r_ref[...], e_ref[...], precision=lax.Precision.HIGHEST,
                         preferred_element_type=F32) + m_ref[...]


def _na_bias_table(rpb):
    h, na, nb = rpb.shape
    col = jnp.arange(GRID_W)
    c0 = jnp.clip(col - NA_WIN_C // 2, 0, GRID_W - NA_WIN_C)
    in_win = (col[None, :] >= c0[:, None]) & (col[None, :] < c0[:, None] + NA_WIN_C)
    dc = col[None, :] - col[:, None] + NA_WIN_C - 1
    onehot = (jnp.arange(128)[:, None, None] == dc[None]) & in_win[None]
    onehot = onehot.astype(F32).reshape(128, GRID_W * GRID_W)
    mask = jnp.where(in_win, 0.0, -jnp.inf).astype(F32).reshape(1, GRID_W * GRID_W)
    r = jnp.pad(rpb.reshape(h * na, nb), ((0, 128 - h * na), (0, 128 - nb)))
    t = pl.pallas_call(
        _nabias_kernel,
        out_shape=jax.ShapeDtypeStruct((128, GRID_W * GRID_W), F32),
        name="nabias",
    )(r, onehot, mask)
    t = t[:h * na].reshape(h, na, GRID_W, GRID_W)
    tabs = [t[:, NA_WIN_R - 1 - o:2 * NA_WIN_R - 1 - o].transpose(0, 2, 1, 3).reshape(h, GRID_W, NA_WIN)
            for o in range(NA_WIN_R)]
    return jnp.stack(tabs, axis=0)


def _na_call(q, kvc, kvl, bias):
    b, t, _ = q.shape
    n_ctx = kvc.shape[1]
    local = kvl is not None
    rows = t // GRID_W
    if local:
        assert rows >= NA_WIN_R and rows % NA_ROWS_PER_STEP == 0
    tq = GRID_W * NA_ROWS_PER_STEP if local else t
    args = [q, kvc] + ([kvl, bias] if local else [])
    in_specs = [pl.BlockSpec((1, tq, NA_W), lambda bi, ri: (bi, ri, 0)),
                pl.BlockSpec((1, n_ctx, 2 * NA_W), lambda bi, ri: (bi, 0, 0))]
    if local:
        in_specs += [pl.BlockSpec((1, t, 2 * NA_W), lambda bi, ri: (bi, 0, 0)), _const_spec(bias.shape)]
    return pl.pallas_call(
        functools.partial(_na_kernel, local=local, rows=rows),
        out_shape=jax.ShapeDtypeStruct((b, t, NA_W), BF16),
        grid=(b, t // tq),
        in_specs=in_specs,
        out_specs=pl.BlockSpec((1, tq, NA_W), lambda bi, ri: (bi, ri, 0)),
        compiler_params=_cparams("parallel", "parallel"),
        name="na",
    )(*args)


def _hyfilt_kernel(z_ref, w1_ref, b1_ref, f1_ref, w2_ref, b2_ref, f2_ref, w3_ref, dec_ref, cm_ref, sm_ref,
                   hre_ref, him_ref, hny_ref):
    hi = lax.Precision.HIGHEST
    t = z_ref.shape[0]
    n_fft = 2 * t
    h = jnp.sin(f1_ref[...] * (jnp.dot(z_ref[...], w1_ref[...], precision=hi, preferred_element_type=F32) + b1_ref[...]))
    h = jnp.sin(f2_ref[...] * (jnp.dot(h, w2_ref[...], precision=hi, preferred_element_type=F32) + b2_ref[...]))
    taps = jnp.dot(h, w3_ref[...], precision=hi, preferred_element_type=F32)
    dec = dec_ref[...]
    hf = taps[:, :HY_W] * dec
    hb = taps[:, HY_W:] * dec
    row = lax.broadcasted_iota(jnp.int32, (t, 1), 0)
    wgt = jnp.where(row == 0, 1.0 / n_fft, 2.0 / n_fft)
    hsum = hf + hb
    hre_ref[...] = _dot(cm_ref[...], hsum.astype(BF16)) * wgt
    him_ref[...] = _dot(sm_ref[...], (hb - hf).astype(BF16)) * wgt
    alt = (1 - 2 * (row & 1)).astype(F32)
    hny = jnp.sum(hsum * alt, axis=0, keepdims=True) * (1.0 / n_fft)
    hny_ref[...] = jnp.broadcast_to(hny, hny_ref.shape)


def _hyfilt_call(z, w1, b1, f1, w2, b2, f2, w3, dec, cm, sm):
    t = z.shape[0]
    args = (z, w1, b1, f1, w2, b2, f2, w3, dec, cm, sm)
    return pl.pallas_call(
        _hyfilt_kernel,
        out_shape=[jax.ShapeDtypeStruct((t, HY_W), F32), jax.ShapeDtypeStruct((t, HY_W), F32),
                   jax.ShapeDtypeStruct((8, HY_W), F32)],
        grid=(1,),
        in_specs=[_const_spec(a.shape) for a in args],
        out_specs=[pl.BlockSpec((t, HY_W), lambda i: (0, 0)), pl.BlockSpec((t, HY_W), lambda i: (0, 0)),
                   pl.BlockSpec((8, HY_W), lambda i: (0, 0))],
        compiler_params=_cparams("arbitrary"),
        name="hyfilt",
    )(*args)


HY_CB = 256
HY_FB = 512


def _hyena_kernel(zv_ref, z1_ref, z0_ref, wv_ref, w1_ref, w0_ref, bv_ref, b1_ref, b0_ref, bd_ref,
                  cm_ref, sm_ref, hre_ref, him_ref, hny_ref, o_ref, y_ref):
    t = zv_ref.shape[1]
    row = lax.broadcasted_iota(jnp.int32, (t, 1), 0)

    def conv3(z_ref, w_ref, b_ref):
        z = z_ref[0]
        prev = jnp.where(row == 0, 0.0, pltpu.roll(z, 1, 0))
        nxt = jnp.where(row == t - 1, 0.0, pltpu.roll(z, t - 1, 0))
        return w_ref[0:1, :] * prev + w_ref[1:2, :] * z + w_ref[2:3, :] * nxt + b_ref[...]

    s = conv3(zv_ref, wv_ref, bv_ref) * conv3(z1_ref, w1_ref, b1_ref)
    alt = (1 - 2 * (row & 1)).astype(F32)
    sb = s.astype(BF16)
    xny = jnp.sum(s * alt, axis=0, keepdims=True)
    y_ref[...] = alt * (xny * hny_ref[0:1, :]) + s * bd_ref[...]
    fb = min(HY_FB, t)
    for f0 in range(0, t, fb):
        fr = slice(f0, f0 + fb)
        a = _dot(cm_ref[fr, :], sb)
        bq = _dot(sm_ref[fr, :], sb)
        hre = hre_ref[fr, :]
        him = him_ref[fr, :]
        yc = (a * hre + bq * him).astype(BF16)
        ys = (bq * hre - a * him).astype(BF16)
        y_ref[...] += _dot(cm_ref[:, fr], yc) + _dot(sm_ref[:, fr], ys)
    o_ref[0] = (conv3(z0_ref, w0_ref, b0_ref) * y_ref[...]).astype(o_ref.dtype)


def _hyena_call(z, conv_w, conv_b, bias_d, cm, sm, hre, him, hny):
    b, t, _ = z.shape
    nb = HY_W // HY_CB
    zspec = lambda k: pl.BlockSpec((1, t, HY_CB), lambda bi, ci: (bi, 0, k * nb + ci))
    wspec = lambda k: pl.BlockSpec((3, HY_CB), lambda bi, ci: (0, k * nb + ci))
    bspec = lambda k: pl.BlockSpec((1, HY_CB), lambda bi, ci: (0, k * nb + ci))
    cspec = lambda rows: pl.BlockSpec((rows, HY_CB), lambda bi, ci: (0, ci))
    conv_b = conv_b.reshape(1, -1)
    return pl.pallas_call(
        _hyena_kernel,
        out_shape=jax.ShapeDtypeStruct((b, t, HY_W), BF16),
        grid=(b, nb),
        in_specs=[zspec(0), zspec(1), zspec(2), wspec(0), wspec(1), wspec(2), bspec(0), bspec(1), bspec(2),
                  cspec(1), _const_spec(cm.shape), _const_spec(sm.shape), cspec(t), cspec(t), cspec(8)],
        out_specs=pl.BlockSpec((1, t, HY_CB), lambda bi, ci: (bi, 0, ci)),
        scratch_shapes=[pltpu.VMEM((t, HY_CB), F32)],
        compiler_params=_cparams("parallel", "parallel"),
        name="hyena",
    )(z, z, z, conv_w, conv_w, conv_w, conv_b, conv_b, conv_b, bias_d.reshape(1, -1), cm, sm, hre, him, hny)


S5_CW = S5_CHUNK * S5_H


S5_GPS = 128 // S5_H
S5_NB = 4
S5_RB = 64


def _block_transpose(xs, blk):
    xs = list(xs)
    n = len(xs)
    d = n // 2
    while d:
        keep = (blk & d) == 0
        new = list(xs)
        for i in range(n):
            if i & d:
                continue
            a, b = xs[i], xs[i + d]
            new[i] = jnp.where(keep, a, pltpu.roll(b, d * S5_H, 1))
            new[i + d] = jnp.where(keep, pltpu.roll(a, 128 - d * S5_H, 1), b)
        xs = new
        d //= 2
    return xs


def _s5_kernel(xc_ref, xl_ref, m_ref, w_ref, v_ref, lam_ref, d_ref, yc_ref, yl_ref,
               u_ref, s_ref, xpf_ref, xpb_ref, y_ref, *, nb, n_cc, n_lc):
    parts = ((xc_ref, yc_ref, 0, n_cc), (xl_ref, yl_ref, nb * n_cc, n_lc))
    blk = lax.broadcasted_iota(jnp.int32, (1, 128), 1) // S5_H
    fwd = lax.broadcasted_iota(jnp.int32, (1, 128), 1) < S5_P

    def row_blocks(n_chunks):
        rb = min(S5_RB, n_chunks)
        return [(c0, rb) for c0 in range(0, n_chunks, rb)]

    def gather_rows(b, carry):
        for x_ref, _, base, n_chunks in parts:
            for c0, cn in row_blocks(n_chunks):
                xs = [x_ref[j, b, c0:c0 + cn, :] for j in range(S5_CHUNK)]
                r0 = pl.multiple_of(base + b * n_chunks + c0, 16)
                halves = [_block_transpose(xs[h * S5_GPS:(h + 1) * S5_GPS], blk) for h in range(2)]
                for g in range(S5_GPS):
                    u_ref[g, pl.ds(r0, cn), :] = jnp.concatenate([halves[0][g], halves[1][g]], axis=1).astype(BF16)
        return carry

    lax.fori_loop(0, nb, gather_rows, 0)

    for g in range(S5_GPS):
        s = _dot(u_ref[g], w_ref[g])
        s_ref[g, 0] = s[:, 0:128]
        s_ref[g, 1] = s[:, 128:256]

    def scan_part(carry, base, n_chunks):
        def body(k, carry):
            kb = n_chunks - 1 - k
            rf = pl.ds(base + k, nb, stride=n_chunks)
            rb = pl.ds(base + kb, nb, stride=n_chunks)
            new = []
            for g in range(S5_GPS):
                xa, xb = carry[2 * g], carry[2 * g + 1]
                ar, ai = lam_ref[g, 0:1, :], lam_ref[g, 1:2, :]
                xpf_ref[g, 0, rf, :] = xa
                xpf_ref[g, 1, rf, :] = xb
                xpb_ref[g, 0, rb, :] = xa
                xpb_ref[g, 1, rb, :] = xb
                in_a = jnp.where(fwd, s_ref[g, 0, rf, :], s_ref[g, 0, rb, :])
                in_b = jnp.where(fwd, s_ref[g, 1, rf, :], s_ref[g, 1, rb, :])
                new += [ar * xa - ai * xb + in_a, ar * xb + ai * xa + in_b]
            return tuple(new)

        return lax.fori_loop(0, n_chunks, body, carry)

    carry = tuple(jnp.zeros((nb, 128), F32) for _ in range(2 * S5_GPS))
    for _, _, base, n_chunks in parts:
        carry = scan_part(carry, base, n_chunks)

    for g in range(S5_GPS):
        xp = jnp.concatenate([jnp.where(fwd, xpf_ref[g, h], xpb_ref[g, h]) for h in range(2)], axis=1).astype(BF16)
        y_ref[g] = _dot(u_ref[g], m_ref[g]) + _dot(xp, v_ref[g])

    def scatter_rows(b, carry):
        for x_ref, o_ref, base, n_chunks in parts:
            for c0, cn in row_blocks(n_chunks):
                r0 = pl.multiple_of(base + b * n_chunks + c0, 16)
                ys = [y_ref[g, pl.ds(r0, cn), :] for g in range(S5_GPS)]
                halves = [_block_transpose([y[:, h * 128:(h + 1) * 128] for y in ys], blk) for h in range(2)]
                for j in range(S5_CHUNK):
                    half, jj = divmod(j, S5_GPS)
                    y = halves[half][jj] + d_ref[...] * x_ref[j, b, c0:c0 + cn, :]
                    cdf = 0.5 * (1.0 + jnp.tanh(math.sqrt(2.0 / math.pi) * (y + 0.044715 * (y * y * y))))
                    o_ref[j, b, c0:c0 + cn, :] = y * cdf
        return carry

    lax.fori_loop(0, nb, scatter_rows, 0)


def _s5_call(xc, xl, m, w, v, lam, d):
    _, b, n_cc, width = xc.shape
    n_lc = xl.shape[2]
    nb = min(S5_NB, b)
    rows = nb * (n_cc + n_lc)
    xspec = lambda n: pl.BlockSpec((S5_CHUNK, nb, n, 128), lambda mi, bi: (0, bi, 0, mi))
    gspec = lambda a: pl.BlockSpec((S5_GPS,) + a.shape[1:], lambda mi, bi: (mi, 0, 0))
    return pl.pallas_call(
        functools.partial(_s5_kernel, nb=nb, n_cc=n_cc, n_lc=n_lc),
        out_shape=[jax.ShapeDtypeStruct(xc.shape, F32), jax.ShapeDtypeStruct(xl.shape, F32)],
        grid=(width // 128, b // nb),
        in_specs=[xspec(n_cc), xspec(n_lc), gspec(m), gspec(w), gspec(v), gspec(lam),
                  pl.BlockSpec((1, 128), lambda mi, bi: (0, mi))],
        out_specs=[xspec(n_cc), xspec(n_lc)],
        scratch_shapes=[pltpu.VMEM((S5_GPS, rows, S5_CW), BF16), pltpu.VMEM((S5_GPS, 2, rows, 128), F32),
                        pltpu.VMEM((S5_GPS, 2, rows, 128), F32), pltpu.VMEM((S5_GPS, 2, rows, 128), F32),
                        pltpu.VMEM((S5_GPS, rows, S5_CW), F32)],
        compiler_params=_cparams("parallel", "parallel"),
        name="s5",
    )(xc, xl, m, w, v, lam, d)


def _merge_kernel(x_ref, m_ref, g_ref, a_ref, b_ref, c_ref, d_ref, wg_ref, wb_ref, wglu_ref, bglu_ref, wo_ref,
                  o_ref, acc_ref, cy_ref):
    x = x_ref[0]
    tm, d = x.shape
    h = _modnorm(x, g_ref[2:3, :], m_ref[0, 3:4, :], m_ref[0, 4:5, :]).astype(BF16)
    for j in range(S5_CHUNK):
        for k in range(cy_ref.shape[0]):
            cy_ref[k, pl.ds(j, tm // S5_CHUNK, stride=S5_CHUNK), :] = c_ref[j, 0, :, k * 128:(k + 1) * 128]
    cy = jnp.concatenate([cy_ref[k] for k in range(cy_ref.shape[0])], axis=1)
    glu = _dot(cy.astype(BF16), wglu_ref[...]) + bglu_ref[...]
    nw = glu.shape[-1] // 2
    c = (glu[:, :nw] * _sigmoid(glu[:, nw:])).astype(BF16)
    for n, br in enumerate((a_ref[0], b_ref[0], c, d_ref[0])):
        gate = _sigmoid(_dot(h, wg_ref[:, n * d:(n + 1) * d]))
        part = gate * _dot(br, wb_ref[n])
        if n == 0:
            acc_ref[...] = part
        else:
            acc_ref[...] += part
    y = _dot(acc_ref[...].astype(BF16), wo_ref[...])
    o_ref[0] = x + m_ref[0, 5:6, :] * _rms(y, g_ref[3:4, :])


def _merge_call(x, mods, mod_row, g, a, b_, c, d_, wg, wb, wglu, bglu, wo):
    b, t, d = x.shape
    tm = min(256, t)
    mrow = (lambda bi, ti: (bi, 0, 0)) if mod_row is None else (lambda bi, ti: (mod_row, 0, 0))
    row = lambda n: pl.BlockSpec((1, tm, n), lambda bi, ti: (bi, ti, 0))
    bw = a.shape[-1]
    return pl.pallas_call(
        _merge_kernel,
        out_shape=jax.ShapeDtypeStruct(x.shape, F32),
        grid=(b, t // tm),
        in_specs=[row(d), pl.BlockSpec((1, N_MOD, d), mrow), _const_spec(g.shape),
                  row(bw), row(bw),
                  pl.BlockSpec((S5_CHUNK, 1, tm // S5_CHUNK, bw), lambda bi, ti: (0, bi, ti, 0)), row(bw),
                  _const_spec(wg.shape), _const_spec(wb.shape), _const_spec(wglu.shape), _const_spec(bglu.shape),
                  _const_spec(wo.shape)],
        out_specs=row(d),
        scratch_shapes=[pltpu.VMEM((tm, d), F32), pltpu.VMEM((bw // 128, tm, 128), F32)],
        compiler_params=_cparams("parallel", "parallel"),
        name="merge",
    )(x, mods, g, a, b_, c, d_, wg, wb, wglu, bglu, wo)


def _rope_table(n_tok):
    q = MLA_ROPE // 4
    t = jnp.arange(n_tok)
    pos = jnp.stack([t // GRID_W, t % GRID_W], axis=-1).astype(F32)
    inv = ROPE_BASE ** (-jnp.arange(q, dtype=F32) / q)
    ang = pos[:, :, None] * inv
    cos, sin = jnp.cos(ang), jnp.sin(ang)
    cos_t = jnp.stack([cos, cos], axis=2).reshape(n_tok, MLA_ROPE)
    sin_t = jnp.stack([-sin, sin], axis=2).reshape(n_tok, MLA_ROPE)
    return jnp.concatenate([cos_t, sin_t], axis=-1)


def _rope_swap(w):
    q = MLA_ROPE // 4
    return w.reshape(w.shape[:-1] + (2, 2, q))[..., ::-1, :].reshape(w.shape)


def _dft_mats(t):
    n = 2 * t
    nb = 64 if t % 64 == 0 else 1
    tt = jnp.arange(t, dtype=jnp.int32)[None, :]
    ang = lambda f: ((f[:, None] * tt) % n).astype(F32) * (2.0 * math.pi / n)
    aa = ang(jnp.arange(t // nb, dtype=jnp.int32) * nb)[:, None, :]
    ab = ang(jnp.arange(nb, dtype=jnp.int32))[None, :, :]
    ca, sa, cb, sb = jnp.cos(aa), jnp.sin(aa), jnp.cos(ab), jnp.sin(ab)
    cm = (ca * cb - sa * sb).reshape(t, t)
    sm = (sa * cb + ca * sb).reshape(t, t)
    return cm.astype(BF16), sm.astype(BF16)


def _hyena_consts(n_tok):
    bands = (HY_EMB - 1) // 2
    t = jnp.arange(n_tok, dtype=F32)
    t01 = jnp.linspace(0.0, 1.0, n_tok, dtype=F32)[:, None]
    ang = (2.0 * math.pi * t / n_tok)[:, None] * jnp.linspace(1e-4, bands - 1, bands, dtype=F32)
    z = jnp.concatenate([t01, jnp.cos(ang), -jnp.sin(ang)], axis=-1)
    z = jnp.pad(z, ((0, 0), (0, 128 - HY_EMB)))
    max_decay = math.log(HY_TARGET) / HY_FAST_PCT
    min_decay = math.log(HY_TARGET) / HY_SLOW_PCT
    deltas = jnp.abs(jnp.linspace(min_decay, max_decay, HY_W, dtype=F32))
    return z, jnp.exp(-t01 * deltas)


def _s5_operators(lam_re, lam_im, log_dt, b_re, b_im, c_re, c_im):
    tc = S5_CHUNK
    lam = lax.complex(jnp.minimum(lam_re, -1e-4), lam_im)
    lam_dt = lam * jnp.exp(log_dt)[..., None]
    lam_bar = jnp.exp(lam_dt)
    b_bar = ((lam_bar - 1.0) / lam)[..., None] * lax.complex(b_re, b_im)
    cm = lax.complex(c_re, c_im)
    pw = jnp.exp(lam_dt[..., None] * jnp.arange(tc + 1, dtype=F32))
    g = lam_re.shape[1]
    kern = jnp.real(jnp.einsum('dgap,dgpk,dgph->dkgah', cm, pw[..., :tc], b_bar,
                               precision=lax.Precision.HIGHEST))
    zpad = lambda x, lo, hi: jnp.pad(x, ((lo, hi),) + ((0, 0),) * (x.ndim - 1))
    mf = jnp.stack([zpad(kern[0][:tc - s], s, 0) for s in range(tc)])
    mb = jnp.stack([zpad(kern[1][:s + 1][::-1], 0, tc - 1 - s) for s in range(tc)])
    m = (mf + mb).transpose(2, 0, 4, 1, 3).reshape(g, S5_CW, S5_CW)
    pf = pw[0][..., :tc][..., ::-1]
    pb = pw[1][..., :tc]
    wf = jnp.einsum('gps,gph->gshp', pf, b_bar[0]).reshape(g, S5_CW, S5_P)
    wb = jnp.einsum('gps,gph->gshp', pb, b_bar[1]).reshape(g, S5_CW, S5_P)
    w = jnp.concatenate([wf.real, wb.real, wf.imag, wb.imag], axis=-1)
    gf = jnp.einsum('gap,gpj->gpja', cm[0], pw[0][..., 1:tc + 1]).reshape(g, S5_P, S5_CW)
    gb = jnp.einsum('gap,gpj->gpja', cm[1], pw[1][..., 1:tc + 1][..., ::-1]).reshape(g, S5_P, S5_CW)
    v = jnp.concatenate([gf.real, gb.real, -gf.imag, -gb.imag], axis=1)
    a16 = pw[..., tc]
    lam16 = jnp.stack([jnp.concatenate([a16[0].real, a16[1].real], -1),
                       jnp.concatenate([a16[0].imag, a16[1].imag], -1)], axis=1)
    lam16 = jnp.pad(lam16, ((0, 0), (0, 6), (0, 0)))
    return m.astype(BF16), w.astype(BF16), v.astype(BF16), lam16


def _prep_layer(i, p):
    wi = p['ffn_w_in'][i].astype(BF16)
    wo = p['ffn_w_out'][i].astype(BF16)
    w_in = p['w_in'][i]
    seg = lambda k: w_in[:, sum(IN_SPLITS[:k]):sum(IN_SPLITS[:k + 1])]
    kr = seg(1)
    w_proj = jnp.concatenate([seg(0), seg(5), kr, _rope_swap(kr), seg(6), seg(2), seg(3), seg(4), seg(7)],
                             axis=1).astype(BF16)
    w_gate = seg(8).astype(BF16)
    wuq = p['mla_w_uq'][i]
    rope = wuq[..., MLA_NOPE:]
    wq = jnp.concatenate([wuq[..., :MLA_NOPE], rope, _rope_swap(rope)], axis=-1)
    wq = wq.reshape(MLA_RANK, MLA_HEADS * MLA_KW).astype(BF16)
    wukv = p['mla_w_ukv'][i]
    wkv = jnp.concatenate([wukv[..., :MLA_NOPE].reshape(MLA_RANK, -1), wukv[..., MLA_NOPE:].reshape(MLA_RANK, -1)],
                          axis=1).astype(BF16)
    s5m, s5w, s5v, s5lam = _s5_operators(p['s5_lam_re'][i], p['s5_lam_im'][i], p['s5_log_dt'][i], p['s5_b_re'][i],
                                         p['s5_b_im'][i], p['s5_c_re'][i], p['s5_c_im'][i])
    s5d = p['s5_d'][i].reshape(1, -1)
    return dict(
        g=p['norm_g'][i], wi=wi, wo=wo, w_proj=w_proj, w_gate=w_gate,
        gq=p['mla_g_q'][i].reshape(1, -1), gkv=p['mla_g_kv'][i].reshape(1, -1), wq=wq, wkv=wkv,
        bias=_na_bias_table(p['na_rpb'][i]),
        hy_mlp=(jnp.pad(p['hy_w1'][i], ((0, 128 - HY_EMB), (0, 0))), p['hy_b1'][i].reshape(1, -1),
                p['hy_freq1'][i].reshape(1, -1), p['hy_w2'][i], p['hy_b2'][i].reshape(1, -1),
                p['hy_freq2'][i].reshape(1, -1), p['hy_w3'][i]),
        conv_w=p['hy_conv_w'][i], conv_b=p['hy_conv_b'][i], hy_bias=p['hy_bias'][i],
        s5m=s5m, s5w=s5w, s5v=s5v, s5lam=s5lam, s5d=s5d,
        wglu=p['s5_w_glu'][i].astype(BF16), bglu=p['s5_b_glu'][i].reshape(1, -1),
        wb=p['w_branch'][i].astype(BF16), w_out=p['w_out'][i].astype(BF16),
    )


def _mixer(xc, xl, mods, w, consts, ctx_out):
    b, n_lat, _ = xl.shape
    n_ctx = xc.shape[1]
    ctx_row = b
    mla_l, nq_l, nkv_l, u_l, hy_l = _inproj_call(xl, mods, None, w['g'], w['w_proj'])
    mla_c, nq_c, nkv_c, u_c, hy_c = _inproj_call(xc, mods, ctx_row, w['g'], w['w_proj'])

    scale = (MLA_NOPE + MLA_ROPE) ** -0.5 * math.log2(math.e)
    rope_l = consts['rope_l']
    tabq_l = jnp.concatenate([jnp.ones((n_lat, MLA_NOPE), F32), rope_l], axis=-1) * scale
    a_l = _mla_call(mla_l, mla_c, mla_l, tabq_l, rope_l, w['gq'], w['gkv'], w['wq'], w['wkv'])
    d_l = _na_call(nq_l, nkv_c, nkv_l, w['bias'])
    cy_c, cy_l = _s5_call(u_c, u_l, w['s5m'], w['s5w'], w['s5v'], w['s5lam'], w['s5d'])
    hre, him, hny = _hyfilt_call(consts['hy_z_l'], *w['hy_mlp'], consts['hy_dec_l'], *consts['dft_l'])
    b_l = _hyena_call(hy_l, w['conv_w'], w['conv_b'], w['hy_bias'], *consts['dft_l'], hre, him, hny)
    merge_w = (w['w_gate'], w['wb'], w['wglu'], w['bglu'], w['w_out'])
    xl = _merge_call(xl, mods, None, w['g'], a_l, b_l, cy_l, d_l, *merge_w)
    if not ctx_out:
        return None, xl

    tabq_c = jnp.concatenate([jnp.ones((n_ctx, MLA_NOPE + MLA_ROPE), F32), jnp.zeros((n_ctx, MLA_ROPE), F32)],
                             axis=-1) * scale
    a_c = _mla_call(mla_c, mla_c, None, tabq_c, None, w['gq'], w['gkv'], w['wq'], w['wkv'])
    d_c = _na_call(nq_c, nkv_c, None, None)
    hre, him, hny = _hyfilt_call(consts['hy_z_c'], *w['hy_mlp'], consts['hy_dec_c'], *consts['dft_c'])
    b_c = _hyena_call(hy_c, w['conv_w'], w['conv_b'], w['hy_bias'], *consts['dft_c'], hre, him, hny)
    xc = _merge_call(xc, mods, ctx_row, w['g'], a_c, b_c, cy_c, d_c, *merge_w)
    return xc, xl


def kernel(x, c, ctx, c_ctx, w_mod, b_mod, norm_g, ffn_w_in, ffn_w_out, w_in, mla_g_q, mla_g_kv, mla_w_uq, mla_w_ukv, na_rpb, hy_conv_w, hy_conv_b, hy_bias, hy_w1, hy_b1, hy_freq1, hy_w2, hy_b2, hy_freq2, hy_w3, s5_lam_re, s5_lam_im, s5_log_dt, s5_b_re, s5_b_im, s5_c_re, s5_c_im, s5_d, s5_w_glu, s5_b_glu, w_branch, w_out):
    p = dict(norm_g=norm_g, ffn_w_in=ffn_w_in, ffn_w_out=ffn_w_out, w_in=w_in, mla_g_q=mla_g_q, mla_g_kv=mla_g_kv,
             mla_w_uq=mla_w_uq, mla_w_ukv=mla_w_ukv, na_rpb=na_rpb, hy_conv_w=hy_conv_w, hy_conv_b=hy_conv_b,
             hy_bias=hy_bias, hy_w1=hy_w1, hy_b1=hy_b1, hy_freq1=hy_freq1, hy_w2=hy_w2, hy_b2=hy_b2,
             hy_freq2=hy_freq2, hy_w3=hy_w3, s5_lam_re=s5_lam_re, s5_lam_im=s5_lam_im, s5_log_dt=s5_log_dt,
             s5_b_re=s5_b_re, s5_b_im=s5_b_im, s5_c_re=s5_c_re, s5_c_im=s5_c_im, s5_d=s5_d, s5_w_glu=s5_w_glu,
             s5_b_glu=s5_b_glu, w_branch=w_branch, w_out=w_out)
    b, n_lat, d = x.shape
    n_ctx = ctx.shape[1]
    depth = w_mod.shape[0]
    assert b % 8 == 0 and n_lat % 256 == 0 and n_ctx % 256 == 0

    rows = -(-(b + 1) // 8) * 8
    acts = jnp.concatenate([c, c_ctx[None, :], jnp.zeros((rows - b - 1, d), F32)], axis=0)
    mods = _mod_call(acts, w_mod, b_mod).reshape(depth, rows, N_MOD, d)

    z_l, dec_l = _hyena_consts(n_lat)
    z_c, dec_c = _hyena_consts(n_ctx)
    consts = dict(rope_l=_rope_table(n_lat), dft_l=_dft_mats(n_lat), dft_c=_dft_mats(n_ctx),
                  hy_z_l=z_l, hy_dec_l=dec_l, hy_z_c=z_c, hy_dec_c=dec_c)

    xc, xl = ctx, x
    for i in range(depth):
        ctx_out = i < depth - 1
        w = _prep_layer(i, p)
        m = mods[i]
        xl = _ffn_call(xl, m, None, w['g'], w['wi'][0], w['wo'][0], 0, 0)
        xc = _ffn_call(xc, m, b, w['g'], w['wi'][0], w['wo'][0], 0, 0)
        xc, xl = _mixer(xc, xl, m, w, consts, ctx_out)
        xl = _ffn_call(xl, m, None, w['g'], w['wi'][1], w['wo'][1], 6, 4)
        if ctx_out:
            xc = _ffn_call(xc, m, b, w['g'], w['wi'][1], w['wo'][1], 6, 4)
    return xl
```

```python
import functools
import math

import jax
import jax.numpy as jnp
from jax import lax
from jax.experimental import pallas as pl
from jax.experimental.pallas import tpu as pltpu

F32 = jnp.float32
BF16 = jnp.bfloat16

EPS = 1e-6
GRID_W = 64
ROPE_BASE = 10000.0
N_MOD = 9
MLA_HEADS, MLA_NOPE, MLA_ROPE, MLA_V = 4, 128, 64, 128
MLA_RANK = 256
NA_HEADS, NA_DIM, NA_WIN_R, NA_WIN_C = 8, 64, 8, 16
HY_W, HY_EMB, HY_FILT = 512, 33, 64
HY_FAST_PCT, HY_SLOW_PCT, HY_TARGET = 0.3, 1.5, 1e-2
S5_H, S5_P = 16, 64
S5_CHUNK = 16
IN_SPLITS = (256, 64, 512, 512, 512, 256, 512, 1536, 4096)
FFN_CHUNK = 256
VMEM_LIMIT = 56 * 1024 * 1024
FFN_ROWS, INPROJ_ROWS, MERGE_ROWS = 1024, 1024, 512


def _cparams(*sem):
    return pltpu.CompilerParams(dimension_semantics=sem, vmem_limit_bytes=VMEM_LIMIT)


def _const_spec(shape):
    nd = len(shape)
    return pl.BlockSpec(shape, lambda *_: (0,) * nd, pipeline_mode=pl.Buffered(1))


def _dot(a, b):
    return jnp.dot(a, b, preferred_element_type=F32)


def _dot_nt(a, b):
    return lax.dot_general(a, b, (((1,), (1,)), ((), ())), preferred_element_type=F32)


def _sigmoid(x):
    return 1.0 / (1.0 + jnp.exp(-x))


def _rms(x, g):
    return x * lax.rsqrt(jnp.mean(x * x, axis=-1, keepdims=True) + EPS) * g


def _modnorm(x, g, shift, scale):
    return _rms(x, g) * (1.0 + scale) + shift


def _mod_kernel(act_ref, w_ref, b_ref, o_ref):
    a = act_ref[...]
    a = a * _sigmoid(a)
    a_hi = a.astype(BF16)
    a_lo = (a - a_hi.astype(F32)).astype(BF16)
    w = w_ref[0]
    w_hi = w.astype(BF16)
    w_lo = (w - w_hi.astype(F32)).astype(BF16)
    o_ref[0] = _dot(a_hi, w_hi) + _dot(a_lo, w_hi) + _dot(a_hi, w_lo) + b_ref[0]


def _mod_call(acts, w_mod, b_mod):
    nl, d, nd = w_mod.shape
    rows = acts.shape[0]
    tn = 1152
    return pl.pallas_call(
        _mod_kernel,
        out_shape=jax.ShapeDtypeStruct((nl, rows, nd), F32),
        grid=(nl, nd // tn),
        in_specs=[pl.BlockSpec((rows, d), lambda l, j: (0, 0)),
                  pl.BlockSpec((1, d, tn), lambda l, j: (l, 0, j)),
                  pl.BlockSpec((1, 1, tn), lambda l, j: (l, 0, j))],
        out_specs=pl.BlockSpec((1, rows, tn), lambda l, j: (l, 0, j)),
        compiler_params=_cparams("parallel", "parallel"),
        name="mod",
    )(acts, w_mod, b_mod.reshape(nl, 1, nd))


def _ffn_kernel(x_ref, m_ref, g_ref, wi_ref, wo_ref, o_ref, acc_ref, *, base, gidx, n_chunks):
    x = x_ref[0]
    shift, scale, gate = (m_ref[0, base + k:base + k + 1, :] for k in range(3))
    h = _modnorm(x, g_ref[gidx:gidx + 1, :], shift, scale).astype(BF16)
    f = n_chunks * FFN_CHUNK
    for j in range(n_chunks):
        cs = slice(j * FFN_CHUNK, (j + 1) * FFN_CHUNK)
        a = _dot(h, wi_ref[:, cs])
        b = _dot(h, wi_ref[:, f + j * FFN_CHUNK:f + (j + 1) * FFN_CHUNK])
        act = (a * _sigmoid(a) * b).astype(BF16)
        part = _dot(act, wo_ref[cs, :])
        if j == 0:
            acc_ref[...] = part
        else:
            acc_ref[...] += part
    y = acc_ref[...]
    o_ref[0] = x + 0.5 * gate * _rms(y, g_ref[gidx + 1:gidx + 2, :])


def _ffn_call(x, mods, mod_row, g, wi, wo, base, gidx):
    b, t, d = x.shape
    tm = min(FFN_ROWS, t)
    n_chunks = wo.shape[0] // FFN_CHUNK
    mrow = (lambda bi, ti: (bi, 0, 0)) if mod_row is None else (lambda bi, ti: (mod_row, 0, 0))
    return pl.pallas_call(
        functools.partial(_ffn_kernel, base=base, gidx=gidx, n_chunks=n_chunks),
        out_shape=jax.ShapeDtypeStruct(x.shape, F32),
        grid=(b, t // tm),
        in_specs=[pl.BlockSpec((1, tm, d), lambda bi, ti: (bi, ti, 0)),
                  pl.BlockSpec((1, N_MOD, d), mrow),
                  _const_spec(g.shape), _const_spec(wi.shape), _const_spec(wo.shape)],
        out_specs=pl.BlockSpec((1, tm, d), lambda bi, ti: (bi, ti, 0)),
        scratch_shapes=[pltpu.VMEM((tm, d), F32)],
        compiler_params=_cparams("parallel", "parallel"),
        name="ffn",
    )(x, mods, g, wi, wo)


INPROJ_COLS = (640, 512, 1024, 512, 1536)


def _inproj_kernel(x_ref, m_ref, g_ref, w_ref, o_mla, o_nq, o_nkv, o_u, o_hy, u_ref):
    x = x_ref[0]
    tm = x.shape[0]
    h = _modnorm(x, g_ref[2:3, :], m_ref[0, 3:4, :], m_ref[0, 4:5, :]).astype(BF16)
    off = 0
    for o_ref, n in zip((o_mla, o_nq, o_nkv, o_u, o_hy), INPROJ_COLS):
        z = _dot(h, w_ref[:, off:off + n])
        if o_ref is o_u:
            for k in range(n // 128):
                u_ref[k] = z[:, k * 128:(k + 1) * 128]
            for j in range(S5_CHUNK):
                for k in range(n // 128):
                    o_ref[j, 0, :, k * 128:(k + 1) * 128] = u_ref[k, pl.ds(j, tm // S5_CHUNK, stride=S5_CHUNK), :]
        else:
            o_ref[0] = z.astype(o_ref.dtype)
        off += n


def _inproj_call(x, mods, mod_row, g, w):
    b, t, d = x.shape
    tm = min(INPROJ_ROWS, t)
    mrow = (lambda bi, ti: (bi, 0, 0)) if mod_row is None else (lambda bi, ti: (mod_row, 0, 0))
    dts = (F32, BF16, BF16, F32, F32)
    out_shape = [jax.ShapeDtypeStruct((b, t, n), dt) for n, dt in zip(INPROJ_COLS, dts)]
    out_specs = [pl.BlockSpec((1, tm, n), lambda bi, ti: (bi, ti, 0)) for n in INPROJ_COLS]
    nu = INPROJ_COLS[3]
    out_shape[3] = jax.ShapeDtypeStruct((S5_CHUNK, b, t // S5_CHUNK, nu), F32)
    out_specs[3] = pl.BlockSpec((S5_CHUNK, 1, tm // S5_CHUNK, nu), lambda bi, ti: (0, bi, ti, 0))
    return pl.pallas_call(
        _inproj_kernel,
        out_shape=out_shape,
        grid=(b, t // tm),
        in_specs=[pl.BlockSpec((1, tm, d), lambda bi, ti: (bi, ti, 0)),
                  pl.BlockSpec((1, N_MOD, d), mrow),
                  _const_spec(g.shape), _const_spec(w.shape)],
        out_specs=out_specs,
        scratch_shapes=[pltpu.VMEM((nu // 128, tm, 128), F32)],
        compiler_params=_cparams("parallel", "parallel"),
        name="inproj",
    )(x, mods, g, w)


MLA_KW = 256


def _mla_kernel(*refs, n_ctx, n_lat, tq):
    if n_lat:
        zq_ref, zc_ref, zl_ref, tabq_ref, tabk_ref, gq_ref, gkv_ref, wq_ref, wkv_ref, o_ref, k_ref, v_ref = refs
    else:
        zq_ref, zc_ref, tabq_ref, gq_ref, gkv_ref, wq_ref, wkv_ref, o_ref, k_ref, v_ref = refs
    half = lax.broadcasted_iota(jnp.int32, (1, 128), 1) < MLA_ROPE

    def fill_kv(z, row0, tab):
        n = z.shape[0]
        kv = _dot(_rms(z[:, 0:MLA_RANK], gkv_ref[...]).astype(BF16), wkv_ref[...])
        r = z[:, 2 * MLA_RANK:2 * MLA_RANK + 128]
        if tab is None:
            rot = jnp.where(half, r, pltpu.roll(r, MLA_ROPE, 1))
        else:
            r = r * tab
            rot = r + pltpu.roll(r, MLA_ROPE, 1)
        rot = rot.astype(BF16)
        for h in range(MLA_HEADS):
            k_ref[h, pl.ds(row0, n), 0:MLA_NOPE] = kv[:, h * MLA_NOPE:(h + 1) * MLA_NOPE].astype(BF16)
            k_ref[h, pl.ds(row0, n), MLA_NOPE:MLA_KW] = rot
        v_ref[pl.ds(row0, n), :] = kv[:, MLA_HEADS * MLA_NOPE:].astype(BF16)

    @pl.when(pl.program_id(1) == 0)
    def _():
        fill_kv(zc_ref[0], 0, None)
        if n_lat:
            ck = 256

            def body(i, carry):
                r0 = pl.multiple_of(i * ck, ck)
                fill_kv(zl_ref[0, pl.ds(r0, ck), :], pl.multiple_of(n_ctx + r0, 16), tabk_ref[pl.ds(r0, ck), :])
                return carry

            lax.fori_loop(0, n_lat // ck, body, 0)

    zq = zq_ref[0]
    q = _dot(_rms(zq[:, MLA_RANK:2 * MLA_RANK], gq_ref[...]).astype(BF16), wq_ref[...])
    tabq = tabq_ref[...]
    for h in range(MLA_HEADS):
        qh = (q[:, h * MLA_KW:(h + 1) * MLA_KW] * tabq).astype(BF16)
        s = _dot_nt(qh, k_ref[h])
        m = jnp.max(s, axis=-1, keepdims=True)
        p = jnp.exp2(s - m)
        l = jnp.sum(p, axis=-1, keepdims=True)
        o = _dot(p.astype(BF16), v_ref[:, h * MLA_V:(h + 1) * MLA_V])
        o_ref[0, :, h * MLA_V:(h + 1) * MLA_V] = (o / l).astype(o_ref.dtype)


def _mla_call(zq, zc, zl, tabq, tabk, gq, gkv, wq, wkv):
    b, t, w = zq.shape
    n_ctx = zc.shape[1]
    n_lat = 0 if zl is None else zl.shape[1]
    tq = min(512, t)
    n_keys = n_ctx + n_lat
    args = [zq, zc] + ([zl] if n_lat else []) + [tabq] + ([tabk] if n_lat else []) + [gq, gkv, wq, wkv]
    in_specs = [pl.BlockSpec((1, tq, w), lambda bi, qi: (bi, qi, 0)),
                pl.BlockSpec((1, n_ctx, w), lambda bi, qi: (bi, 0, 0))]
    if n_lat:
        in_specs.append(pl.BlockSpec((1, n_lat, w), lambda bi, qi: (bi, 0, 0)))
    in_specs.append(pl.BlockSpec((tq, MLA_KW), lambda bi, qi: (qi, 0)))
    if n_lat:
        in_specs.append(_const_spec(tabk.shape))
    in_specs += [_const_spec(gq.shape), _const_spec(gkv.shape), _const_spec(wq.shape), _const_spec(wkv.shape)]
    return pl.pallas_call(
        functools.partial(_mla_kernel, n_ctx=n_ctx, n_lat=n_lat, tq=tq),
        out_shape=jax.ShapeDtypeStruct((b, t, MLA_HEADS * MLA_V), BF16),
        grid=(b, t // tq),
        in_specs=in_specs,
        out_specs=pl.BlockSpec((1, tq, MLA_HEADS * MLA_V), lambda bi, qi: (bi, qi, 0)),
        scratch_shapes=[pltpu.VMEM((MLA_HEADS, n_keys, MLA_KW), BF16),
                        pltpu.VMEM((n_keys, MLA_HEADS * MLA_V), BF16)],
        compiler_params=_cparams("parallel", "arbitrary"),
        name="mla",
    )(*args)


NA_W = NA_HEADS * NA_DIM
NA_WIN = NA_WIN_R * GRID_W


def _na_kernel(*refs, local, rows):
    if local:
        q_ref, kvc_ref, kvl_ref, bias_ref, o_ref = refs
    else:
        q_ref, kvc_ref, o_ref = refs
    tq = q_ref.shape[1]
    lane_lo = lax.broadcasted_iota(jnp.int32, (1, 128), 1) < NA_DIM
    sub = GRID_W if local else tq
    wins, offs = [], []
    for rr in range(tq // sub if local else 0):
        r = pl.program_id(1) * (tq // sub) + rr
        start = jnp.clip(r - NA_WIN_R // 2, 0, rows - NA_WIN_R)
        offs.append(r - start)
        wins.append(pl.ds(pl.multiple_of(start * GRID_W, GRID_W), NA_WIN))
    scale = NA_DIM ** -0.5
    for j in range(NA_HEADS // 2):
        cols = slice(j * 128, (j + 1) * 128)
        vcols = slice(NA_W + j * 128, NA_W + (j + 1) * 128)
        k_c, v_c = kvc_ref[0, :, cols], kvc_ref[0, :, vcols]
        pieces = []
        for rr in range(tq // sub):
            qs = q_ref[0, rr * sub:(rr + 1) * sub, cols] * jnp.asarray(scale, BF16)
            pieces += [jnp.where(lane_lo, qs, jnp.zeros_like(qs)), jnp.where(lane_lo, jnp.zeros_like(qs), qs)]
        q_all = jnp.concatenate(pieces, axis=0)
        s_c = _dot_nt(q_all, k_c)
        m_c = jnp.max(s_c, axis=-1, keepdims=True)
        p_cs, p_ls, ls = [], [], []
        for rr in range(tq // sub):
            rs = slice(rr * 2 * sub, (rr + 1) * 2 * sub)
            m = m_c[rs]
            if local:
                bias = bias_ref[offs[rr], 2 * j:2 * j + 2].reshape(2 * sub, NA_WIN)
                s_l = _dot_nt(q_all[rs], kvl_ref[0, wins[rr], cols]) + bias
                m = jnp.maximum(m, jnp.max(s_l, axis=-1, keepdims=True))
            p_c = jnp.exp(s_c[rs] - m)
            l = jnp.sum(p_c, axis=-1, keepdims=True)
            if local:
                p_l = jnp.exp(s_l - m)
                l = l + jnp.sum(p_l, axis=-1, keepdims=True)
                p_ls.append(p_l.astype(BF16))
            p_cs.append(p_c.astype(BF16))
            ls.append(l)
        o_c = _dot(jnp.concatenate(p_cs, axis=0), v_c)
        for rr in range(tq // sub):
            rs = slice(rr * 2 * sub, (rr + 1) * 2 * sub)
            o = o_c[rs]
            if local:
                o = o + _dot(p_ls[rr], kvl_ref[0, wins[rr], vcols])
            o = o / ls[rr]
            o_ref[0, rr * sub:(rr + 1) * sub, cols] = jnp.where(lane_lo, o[:sub], o[sub:]).astype(o_ref.dtype)


NA_ROWS_PER_STEP = 4


def _nabias_kernel(r_ref, e_ref, m_ref, o_ref):
    o_ref[...] = jnp.dot(r_ref[...], e_ref[...], precision=lax.Precision.HIGHEST,
                         preferred_element_type=F32) + m_ref[...]


def _na_bias_table(rpb):
    h, na, nb = rpb.shape
    col = jnp.arange(GRID_W)
    c0 = jnp.clip(col - NA_WIN_C // 2, 0, GRID_W - NA_WIN_C)
    in_win = (col[None, :] >= c0[:, None]) & (col[None, :] < c0[:, None] + NA_WIN_C)
    dc = col[None, :] - col[:, None] + NA_WIN_C - 1
    onehot = (jnp.arange(128)[:, None, None] == dc[None]) & in_win[None]
    onehot = onehot.astype(F32).reshape(128, GRID_W * GRID_W)
    mask = jnp.where(in_win, 0.0, -jnp.inf).astype(F32).reshape(1, GRID_W * GRID_W)
    r = jnp.pad(rpb.reshape(h * na, nb), ((0, 128 - h * na), (0, 128 - nb)))
    t = pl.pallas_call(
        _nabias_kernel,
        out_shape=jax.ShapeDtypeStruct((128, GRID_W * GRID_W), F32),
        name="nabias",
    )(r, onehot, mask)
    t = t[:h * na].reshape(h, na, GRID_W, GRID_W)
    tabs = [t[:, NA_WIN_R - 1 - o:2 * NA_WIN_R - 1 - o].transpose(0, 2, 1, 3).reshape(h, GRID_W, NA_WIN)
            for o in range(NA_WIN_R)]
    return jnp.stack(tabs, axis=0)


def _na_call(q, kvc, kvl, bias):
    b, t, _ = q.shape
    n_ctx = kvc.shape[1]
    local = kvl is not None
    rows = t // GRID_W
    if local:
        assert rows >= NA_WIN_R and rows % NA_ROWS_PER_STEP == 0
    tq = GRID_W * NA_ROWS_PER_STEP if local else t
    args = [q, kvc] + ([kvl, bias] if local else [])
    in_specs = [pl.BlockSpec((1, tq, NA_W), lambda bi, ri: (bi, ri, 0)),
                pl.BlockSpec((1, n_ctx, 2 * NA_W), lambda bi, ri: (bi, 0, 0))]
    if local:
        in_specs += [pl.BlockSpec((1, t, 2 * NA_W), lambda bi, ri: (bi, 0, 0)), _const_spec(bias.shape)]
    return pl.pallas_call(
        functools.partial(_na_kernel, local=local, rows=rows),
        out_shape=jax.ShapeDtypeStruct((b, t, NA_W), BF16),
        grid=(b, t // tq),
        in_specs=in_specs,
        out_specs=pl.BlockSpec((1, tq, NA_W), lambda bi, ri: (bi, ri, 0)),
        compiler_params=_cparams("parallel", "parallel"),
        name="na",
    )(*args)


def _hyfilt_kernel(z_ref, w1_ref, b1_ref, f1_ref, w2_ref, b2_ref, f2_ref, w3_ref, dec_ref, cm_ref, sm_ref,
                   hre_ref, him_ref, hny_ref):
    hi = lax.Precision.HIGHEST
    t = z_ref.shape[0]
    n_fft = 2 * t
    h = jnp.sin(f1_ref[...] * (jnp.dot(z_ref[...], w1_ref[...], precision=hi, preferred_element_type=F32) + b1_ref[...]))
    h = jnp.sin(f2_ref[...] * (jnp.dot(h, w2_ref[...], precision=hi, preferred_element_type=F32) + b2_ref[...]))
    taps = jnp.dot(h, w3_ref[...], precision=hi, preferred_element_type=F32)
    dec = dec_ref[...]
    hf = taps[:, :HY_W] * dec
    hb = taps[:, HY_W:] * dec
    row = lax.broadcasted_iota(jnp.int32, (t, 1), 0)
    wgt = jnp.where(row == 0, 1.0 / n_fft, 2.0 / n_fft)
    hsum = hf + hb
    hre_ref[...] = _dot(cm_ref[...], hsum.astype(BF16)) * wgt
    him_ref[...] = _dot(sm_ref[...], (hb - hf).astype(BF16)) * wgt
    alt = (1 - 2 * (row & 1)).astype(F32)
    hny = jnp.sum(hsum * alt, axis=0, keepdims=True) * (1.0 / n_fft)
    hny_ref[...] = jnp.broadcast_to(hny, hny_ref.shape)


def _hyfilt_call(z, w1, b1, f1, w2, b2, f2, w3, dec, cm, sm):
    t = z.shape[0]
    args = (z, w1, b1, f1, w2, b2, f2, w3, dec, cm, sm)
    return pl.pallas_call(
        _hyfilt_kernel,
        out_shape=[jax.ShapeDtypeStruct((t, HY_W), F32), jax.ShapeDtypeStruct((t, HY_W), F32),
                   jax.ShapeDtypeStruct((8, HY_W), F32)],
        grid=(1,),
        in_specs=[_const_spec(a.shape) for a in args],
        out_specs=[pl.BlockSpec((t, HY_W), lambda i: (0, 0)), pl.BlockSpec((t, HY_W), lambda i: (0, 0)),
                   pl.BlockSpec((8, HY_W), lambda i: (0, 0))],
        compiler_params=_cparams("arbitrary"),
        name="hyfilt",
    )(*args)


HY_CB = 256
HY_FB = 1024


def _hyena_kernel(zv_ref, z1_ref, z0_ref, wv_ref, w1_ref, w0_ref, bv_ref, b1_ref, b0_ref, bd_ref,
                  cm_ref, sm_ref, hre_ref, him_ref, hny_ref, o_ref, y_ref):
    t = zv_ref.shape[1]
    row = lax.broadcasted_iota(jnp.int32, (t, 1), 0)

    def conv3(z_ref, w_ref, b_ref):
        z = z_ref[0]
        prev = jnp.where(row == 0, 0.0, pltpu.roll(z, 1, 0))
        nxt = jnp.where(row == t - 1, 0.0, pltpu.roll(z, t - 1, 0))
        return w_ref[0:1, :] * prev + w_ref[1:2, :] * z + w_ref[2:3, :] * nxt + b_ref[...]

    s = conv3(zv_ref, wv_ref, bv_ref) * conv3(z1_ref, w1_ref, b1_ref)
    alt = (1 - 2 * (row & 1)).astype(F32)
    sb = s.astype(BF16)
    xny = jnp.sum(s * alt, axis=0, keepdims=True)
    y_ref[...] = alt * (xny * hny_ref[0:1, :]) + s * bd_ref[...]
    fb = min(HY_FB, t)
    for f0 in range(0, t, fb):
        fr = slice(f0, f0 + fb)
        a = _dot(cm_ref[fr, :], sb)
        bq = _dot(sm_ref[fr, :], sb)
        hre = hre_ref[fr, :]
        him = him_ref[fr, :]
        yc = (a * hre + bq * him).astype(BF16)
        ys = (bq * hre - a * him).astype(BF16)
        y_ref[...] += _dot(cm_ref[:, fr], yc) + _dot(sm_ref[:, fr], ys)
    o_ref[0] = (conv3(z0_ref, w0_ref, b0_ref) * y_ref[...]).astype(o_ref.dtype)


def _hyena_call(z, conv_w, conv_b, bias_d, cm, sm, hre, him, hny):
    b, t, _ = z.shape
    nb = HY_W // HY_CB
    zspec = lambda k: pl.BlockSpec((1, t, HY_CB), lambda bi, ci: (bi, 0, k * nb + ci))
    wspec = lambda k: pl.BlockSpec((3, HY_CB), lambda bi, ci: (0, k * nb + ci))
    bspec = lambda k: pl.BlockSpec((1, HY_CB), lambda bi, ci: (0, k * nb + ci))
    cspec = lambda rows: pl.BlockSpec((rows, HY_CB), lambda bi, ci: (0, ci))
    conv_b = conv_b.reshape(1, -1)
    return pl.pallas_call(
        _hyena_kernel,
        out_shape=jax.ShapeDtypeStruct((b, t, HY_W), BF16),
        grid=(b, nb),
        in_specs=[zspec(0), zspec(1), zspec(2), wspec(0), wspec(1), wspec(2), bspec(0), bspec(1), bspec(2),
                  cspec(1), _const_spec(cm.shape), _const_spec(sm.shape), cspec(t), cspec(t), cspec(8)],
        out_specs=pl.BlockSpec((1, t, HY_CB), lambda bi, ci: (bi, 0, ci)),
        scratch_shapes=[pltpu.VMEM((t, HY_CB), F32)],
        compiler_params=_cparams("parallel", "parallel"),
        name="hyena",
    )(z, z, z, conv_w, conv_w, conv_w, conv_b, conv_b, conv_b, bias_d.reshape(1, -1), cm, sm, hre, him, hny)


S5_CW = S5_CHUNK * S5_H


S5_GPS = 128 // S5_H
S5_NB = 4
S5_RB = 64


def _block_transpose(xs, blk):
    xs = list(xs)
    n = len(xs)
    d = n // 2
    while d:
        keep = (blk & d) == 0
        new = list(xs)
        for i in range(n):
            if i & d:
                continue
            a, b = xs[i], xs[i + d]
            new[i] = jnp.where(keep, a, pltpu.roll(b, d * S5_H, 1))
            new[i + d] = jnp.where(keep, pltpu.roll(a, 128 - d * S5_H, 1), b)
        xs = new
        d //= 2
    return xs


def _s5_kernel(xc_ref, xl_ref, m_ref, w_ref, v_ref, lam_ref, d_ref, yc_ref, yl_ref,
               u_ref, s_ref, xpf_ref, xpb_ref, y_ref, *, nb, n_cc, n_lc):
    parts = ((xc_ref, yc_ref, 0, n_cc), (xl_ref, yl_ref, nb * n_cc, n_lc))
    blk = lax.broadcasted_iota(jnp.int32, (1, 128), 1) // S5_H
    fwd = lax.broadcasted_iota(jnp.int32, (1, 128), 1) < S5_P

    def row_blocks(n_chunks):
        rb = min(S5_RB, n_chunks)
        return [(c0, rb) for c0 in range(0, n_chunks, rb)]

    def gather_rows(b, carry):
        for x_ref, _, base, n_chunks in parts:
            for c0, cn in row_blocks(n_chunks):
                xs = [x_ref[j, b, c0:c0 + cn, :] for j in range(S5_CHUNK)]
                r0 = pl.multiple_of(base + b * n_chunks + c0, 16)
                halves = [_block_transpose(xs[h * S5_GPS:(h + 1) * S5_GPS], blk) for h in range(2)]
                for g in range(S5_GPS):
                    u_ref[g, pl.ds(r0, cn), :] = jnp.concatenate([halves[0][g], halves[1][g]], axis=1).astype(BF16)
        return carry

    lax.fori_loop(0, nb, gather_rows, 0)

    for g in range(S5_GPS):
        s = _dot(u_ref[g], w_ref[g])
        s_ref[g, 0] = s[:, 0:128]
        s_ref[g, 1] = s[:, 128:256]

    def scan_part(carry, base, n_chunks):
        def body(k, carry):
            kb = n_chunks - 1 - k
            rf = pl.ds(base + k, nb, stride=n_chunks)
            rb = pl.ds(base + kb, nb, stride=n_chunks)
            new = []
            for g in range(S5_GPS):
                xa, xb = carry[2 * g], carry[2 * g + 1]
                ar, ai = lam_ref[g, 0:1, :], lam_ref[g, 1:2, :]
                xpf_ref[g, 0, rf, :] = xa
                xpf_ref[g, 1, rf, :] = xb
                xpb_ref[g, 0, rb, :] = xa
                xpb_ref[g, 1, rb, :] = xb
                in_a = jnp.where(fwd, s_ref[g, 0, rf, :], s_ref[g, 0, rb, :])
                in_b = jnp.where(fwd, s_ref[g, 1, rf, :], s_ref[g, 1, rb, :])
                new += [ar * xa - ai * xb + in_a, ar * xb + ai * xa + in_b]
            return tuple(new)

        return lax.fori_loop(0, n_chunks, body, carry)

    carry = tuple(jnp.zeros((nb, 128), F32) for _ in range(2 * S5_GPS))
    for _, _, base, n_chunks in parts:
        carry = scan_part(carry, base, n_chunks)

    for g in range(S5_GPS):
        xp = jnp.concatenate([jnp.where(fwd, xpf_ref[g, h], xpb_ref[g, h]) for h in range(2)], axis=1).astype(BF16)
        y_ref[g] = _dot(u_ref[g], m_ref[g]) + _dot(xp, v_ref[g])

    def scatter_rows(b, carry):
        for x_ref, o_ref, base, n_chunks in parts:
            for c0, cn in row_blocks(n_chunks):
                r0 = pl.multiple_of(base + b * n_chunks + c0, 16)
                ys = [y_ref[g, pl.ds(r0, cn), :] for g in range(S5_GPS)]
                halves = [_block_transpose([y[:, h * 128:(h + 1) * 128] for y in ys], blk) for h in range(2)]
                for j in range(S5_CHUNK):
                    half, jj = divmod(j, S5_GPS)
                    y = halves[half][jj] + d_ref[...] * x_ref[j, b, c0:c0 + cn, :]
                    cdf = 0.5 * (1.0 + jnp.tanh(math.sqrt(2.0 / math.pi) * (y + 0.044715 * (y * y * y))))
                    o_ref[j, b, c0:c0 + cn, :] = y * cdf
        return carry

    lax.fori_loop(0, nb, scatter_rows, 0)


def _s5_call(xc, xl, m, w, v, lam, d):
    _, b, n_cc, width = xc.shape
    n_lc = xl.shape[2]
    nb = min(S5_NB, b)
    rows = nb * (n_cc + n_lc)
    xspec = lambda n: pl.BlockSpec((S5_CHUNK, nb, n, 128), lambda mi, bi: (0, bi, 0, mi))
    gspec = lambda a: pl.BlockSpec((S5_GPS,) + a.shape[1:], lambda mi, bi: (mi, 0, 0))
    return pl.pallas_call(
        functools.partial(_s5_kernel, nb=nb, n_cc=n_cc, n_lc=n_lc),
        out_shape=[jax.ShapeDtypeStruct(xc.shape, F32), jax.ShapeDtypeStruct(xl.shape, F32)],
        grid=(width // 128, b // nb),
        in_specs=[xspec(n_cc), xspec(n_lc), gspec(m), gspec(w), gspec(v), gspec(lam),
                  pl.BlockSpec((1, 128), lambda mi, bi: (0, mi))],
        out_specs=[xspec(n_cc), xspec(n_lc)],
        scratch_shapes=[pltpu.VMEM((S5_GPS, rows, S5_CW), BF16), pltpu.VMEM((S5_GPS, 2, rows, 128), F32),
                        pltpu.VMEM((S5_GPS, 2, rows, 128), F32), pltpu.VMEM((S5_GPS, 2, rows, 128), F32),
                        pltpu.VMEM((S5_GPS, rows, S5_CW), F32)],
        compiler_params=_cparams("parallel", "parallel"),
        name="s5",
    )(xc, xl, m, w, v, lam, d)


def _merge_kernel(x_ref, m_ref, g_ref, a_ref, b_ref, c_ref, d_ref, wg_ref, wb_ref, wglu_ref, bglu_ref, wo_ref,
                  o_ref, acc_ref, cy_ref):
    x = x_ref[0]
    tm, d = x.shape
    h = _modnorm(x, g_ref[2:3, :], m_ref[0, 3:4, :], m_ref[0, 4:5, :]).astype(BF16)
    for j in range(S5_CHUNK):
        for k in range(cy_ref.shape[0]):
            cy_ref[k, pl.ds(j, tm // S5_CHUNK, stride=S5_CHUNK), :] = c_ref[j, 0, :, k * 128:(k + 1) * 128]
    cy = jnp.concatenate([cy_ref[k] for k in range(cy_ref.shape[0])], axis=1)
    glu = _dot(cy.astype(BF16), wglu_ref[...]) + bglu_ref[...]
    nw = glu.shape[-1] // 2
    c = (glu[:, :nw] * _sigmoid(glu[:, nw:])).astype(BF16)
    for n, br in enumerate((a_ref[0], b_ref[0], c, d_ref[0])):
        gate = _sigmoid(_dot(h, wg_ref[:, n * d:(n + 1) * d]))
        part = gate * _dot(br, wb_ref[n])
        if n == 0:
            acc_ref[...] = part
        else:
            acc_ref[...] += part
    y = _dot(acc_ref[...].astype(BF16), wo_ref[...])
    o_ref[0] = x + m_ref[0, 5:6, :] * _rms(y, g_ref[3:4, :])


def _merge_call(x, mods, mod_row, g, a, b_, c, d_, wg, wb, wglu, bglu, wo):
    b, t, d = x.shape
    tm = min(MERGE_ROWS, t)
    mrow = (lambda bi, ti: (bi, 0, 0)) if mod_row is None else (lambda bi, ti: (mod_row, 0, 0))
    row = lambda n: pl.BlockSpec((1, tm, n), lambda bi, ti: (bi, ti, 0))
    bw = a.shape[-1]
    return pl.pallas_call(
        _merge_kernel,
        out_shape=jax.ShapeDtypeStruct(x.shape, F32),
        grid=(b, t // tm),
        in_specs=[row(d), pl.BlockSpec((1, N_MOD, d), mrow), _const_spec(g.shape),
                  row(bw), row(bw),
                  pl.BlockSpec((S5_CHUNK, 1, tm // S5_CHUNK, bw), lambda bi, ti: (0, bi, ti, 0)), row(bw),
                  _const_spec(wg.shape), _const_spec(wb.shape), _const_spec(wglu.shape), _const_spec(bglu.shape),
                  _const_spec(wo.shape)],
        out_specs=row(d),
        scratch_shapes=[pltpu.VMEM((tm, d), F32), pltpu.VMEM((bw // 128, tm, 128), F32)],
        compiler_params=_cparams("parallel", "parallel"),
        name="merge",
    )(x, mods, g, a, b_, c, d_, wg, wb, wglu, bglu, wo)


def _rope_table(n_tok):
    q = MLA_ROPE // 4
    t = jnp.arange(n_tok)
    pos = jnp.stack([t // GRID_W, t % GRID_W], axis=-1).astype(F32)
    inv = ROPE_BASE ** (-jnp.arange(q, dtype=F32) / q)
    ang = pos[:, :, None] * inv
    cos, sin = jnp.cos(ang), jnp.sin(ang)
    cos_t = jnp.stack([cos, cos], axis=2).reshape(n_tok, MLA_ROPE)
    sin_t = jnp.stack([-sin, sin], axis=2).reshape(n_tok, MLA_ROPE)
    return jnp.concatenate([cos_t, sin_t], axis=-1)


def _rope_swap(w):
    q = MLA_ROPE // 4
    return w.reshape(w.shape[:-1] + (2, 2, q))[..., ::-1, :].reshape(w.shape)


def _dft_mats(t):
    n = 2 * t
    nb = 64 if t % 64 == 0 else 1
    tt = jnp.arange(t, dtype=jnp.int32)[None, :]
    ang = lambda f: ((f[:, None] * tt) % n).astype(F32) * (2.0 * math.pi / n)
    aa = ang(jnp.arange(t // nb, dtype=jnp.int32) * nb)[:, None, :]
    ab = ang(jnp.arange(nb, dtype=jnp.int32))[None, :, :]
    ca, sa, cb, sb = lax.optimization_barrier((jnp.cos(aa), jnp.sin(aa), jnp.cos(ab), jnp.sin(ab)))
    cm = (ca * cb - sa * sb).reshape(t, t)
    sm = (sa * cb + ca * sb).reshape(t, t)
    return cm.astype(BF16), sm.astype(BF16)


def _hyena_consts(n_tok):
    bands = (HY_EMB - 1) // 2
    t = jnp.arange(n_tok, dtype=F32)
    t01 = jnp.linspace(0.0, 1.0, n_tok, dtype=F32)[:, None]
    ang = (2.0 * math.pi * t / n_tok)[:, None] * jnp.linspace(1e-4, bands - 1, bands, dtype=F32)
    z = jnp.concatenate([t01, jnp.cos(ang), -jnp.sin(ang)], axis=-1)
    z = jnp.pad(z, ((0, 0), (0, 128 - HY_EMB)))
    max_decay = math.log(HY_TARGET) / HY_FAST_PCT
    min_decay = math.log(HY_TARGET) / HY_SLOW_PCT
    deltas = jnp.abs(jnp.linspace(min_decay, max_decay, HY_W, dtype=F32))
    return z, jnp.exp(-t01 * deltas)


def _s5_operators(lam_re, lam_im, log_dt, b_re, b_im, c_re, c_im):
    tc = S5_CHUNK
    lam = lax.complex(jnp.minimum(lam_re, -1e-4), lam_im)
    lam_dt = lam * jnp.exp(log_dt)[..., None]
    lam_bar = jnp.exp(lam_dt)
    b_bar = ((lam_bar - 1.0) / lam)[..., None] * lax.complex(b_re, b_im)
    cm = lax.complex(c_re, c_im)
    pw = jnp.exp(lam_dt[..., None] * jnp.arange(tc + 1, dtype=F32))
    g = lam_re.shape[1]
    kern = jnp.real(jnp.einsum('dgap,dgpk,dgph->dkgah', cm, pw[..., :tc], b_bar,
                               precision=lax.Precision.HIGHEST))
    s = jnp.arange(tc)[:, None, None]
    j = jnp.arange(tc)[None, :, None]
    lag = jnp.arange(tc)[None, None, :]
    place = jnp.stack([(j - s == lag), (s - j == lag)]).astype(F32)
    m = jnp.einsum('dsjk,dkgah->gshja', place, kern, precision=lax.Precision.HIGHEST)
    m = m.reshape(g, S5_CW, S5_CW)
    pf = pw[0][..., :tc][..., ::-1]
    pb = pw[1][..., :tc]
    wf = jnp.einsum('gps,gph->gshp', pf, b_bar[0]).reshape(g, S5_CW, S5_P)
    wb = jnp.einsum('gps,gph->gshp', pb, b_bar[1]).reshape(g, S5_CW, S5_P)
    w = jnp.concatenate([wf.real, wb.real, wf.imag, wb.imag], axis=-1)
    gf = jnp.einsum('gap,gpj->gpja', cm[0], pw[0][..., 1:tc + 1]).reshape(g, S5_P, S5_CW)
    gb = jnp.einsum('gap,gpj->gpja', cm[1], pw[1][..., 1:tc + 1][..., ::-1]).reshape(g, S5_P, S5_CW)
    v = jnp.concatenate([gf.real, gb.real, -gf.imag, -gb.imag], axis=1)
    a16 = pw[..., tc]
    lam16 = jnp.stack([jnp.concatenate([a16[0].real, a16[1].real], -1),
                       jnp.concatenate([a16[0].imag, a16[1].imag], -1)], axis=1)
    lam16 = jnp.pad(lam16, ((0, 0), (0, 6), (0, 0)))
    return m.astype(BF16), w.astype(BF16), v.astype(BF16), lam16


def _prep_layer(i, p):
    wi = p['ffn_w_in'][i].astype(BF16)
    wo = p['ffn_w_out'][i].astype(BF16)
    w_in = p['w_in'][i]
    seg = lambda k: w_in[:, sum(IN_SPLITS[:k]):sum(IN_SPLITS[:k + 1])]
    kr = seg(1)
    w_proj = jnp.concatenate([seg(0), seg(5), kr, _rope_swap(kr), seg(6), seg(2), seg(3), seg(4), seg(7)],
                             axis=1).astype(BF16)
    w_gate = seg(8).astype(BF16)
    wuq = p['mla_w_uq'][i]
    rope = wuq[..., MLA_NOPE:]
    wq = jnp.concatenate([wuq[..., :MLA_NOPE], rope, _rope_swap(rope)], axis=-1)
    wq = wq.reshape(MLA_RANK, MLA_HEADS * MLA_KW).astype(BF16)
    wukv = p['mla_w_ukv'][i]
    wkv = jnp.concatenate([wukv[..., :MLA_NOPE].reshape(MLA_RANK, -1), wukv[..., MLA_NOPE:].reshape(MLA_RANK, -1)],
                          axis=1).astype(BF16)
    s5m, s5w, s5v, s5lam = (a[i] for a in p['s5_ops'])
    s5d = p['s5_d'][i].reshape(1, -1)
    return dict(
        g=p['norm_g'][i], wi=wi, wo=wo, w_proj=w_proj, w_gate=w_gate,
        gq=p['mla_g_q'][i].reshape(1, -1), gkv=p['mla_g_kv'][i].reshape(1, -1), wq=wq, wkv=wkv,
        bias=_na_bias_table(p['na_rpb'][i]),
        hy_mlp=(jnp.pad(p['hy_w1'][i], ((0, 128 - HY_EMB), (0, 0))), p['hy_b1'][i].reshape(1, -1),
                p['hy_freq1'][i].reshape(1, -1), p['hy_w2'][i], p['hy_b2'][i].reshape(1, -1),
                p['hy_freq2'][i].reshape(1, -1), p['hy_w3'][i]),
        conv_w=p['hy_conv_w'][i], conv_b=p['hy_conv_b'][i], hy_bias=p['hy_bias'][i],
        s5m=s5m, s5w=s5w, s5v=s5v, s5lam=s5lam, s5d=s5d,
        wglu=p['s5_w_glu'][i].astype(BF16), bglu=p['s5_b_glu'][i].reshape(1, -1),
        wb=p['w_branch'][i].astype(BF16), w_out=p['w_out'][i].astype(BF16),
    )


def _mixer(xc, xl, mods, w, consts, ctx_out):
    b, n_lat, _ = xl.shape
    n_ctx = xc.shape[1]
    ctx_row = b
    mla_l, nq_l, nkv_l, u_l, hy_l = _inproj_call(xl, mods, None, w['g'], w['w_proj'])
    mla_c, nq_c, nkv_c, u_c, hy_c = _inproj_call(xc, mods, ctx_row, w['g'], w['w_proj'])

    scale = (MLA_NOPE + MLA_ROPE) ** -0.5 * math.log2(math.e)
    rope_l = consts['rope_l']
    tabq_l = jnp.concatenate([jnp.ones((n_lat, MLA_NOPE), F32), rope_l], axis=-1) * scale
    a_l = _mla_call(mla_l, mla_c, mla_l, tabq_l, rope_l, w['gq'], w['gkv'], w['wq'], w['wkv'])
    d_l = _na_call(nq_l, nkv_c, nkv_l, w['bias'])
    cy_c, cy_l = _s5_call(u_c, u_l, w['s5m'], w['s5w'], w['s5v'], w['s5lam'], w['s5d'])
    hre, him, hny = _hyfilt_call(consts['hy_z_l'], *w['hy_mlp'], consts['hy_dec_l'], *consts['dft_l'])
    b_l = _hyena_call(hy_l, w['conv_w'], w['conv_b'], w['hy_bias'], *consts['dft_l'], hre, him, hny)
    merge_w = (w['w_gate'], w['wb'], w['wglu'], w['bglu'], w['w_out'])
    xl = _merge_call(xl, mods, None, w['g'], a_l, b_l, cy_l, d_l, *merge_w)
    if not ctx_out:
        return None, xl

    tabq_c = jnp.concatenate([jnp.ones((n_ctx, MLA_NOPE + MLA_ROPE), F32), jnp.zeros((n_ctx, MLA_ROPE), F32)],
                             axis=-1) * scale
    a_c = _mla_call(mla_c, mla_c, None, tabq_c, None, w['gq'], w['gkv'], w['wq'], w['wkv'])
    d_c = _na_call(nq_c, nkv_c, None, None)
    hre, him, hny = _hyfilt_call(consts['hy_z_c'], *w['hy_mlp'], consts['hy_dec_c'], *consts['dft_c'])
    b_c = _hyena_call(hy_c, w['conv_w'], w['conv_b'], w['hy_bias'], *consts['dft_c'], hre, him, hny)
    xc = _merge_call(xc, mods, ctx_row, w['g'], a_c, b_c, cy_c, d_c, *merge_w)
    return xc, xl


def kernel(x, c, ctx, c_ctx, w_mod, b_mod, norm_g, ffn_w_in, ffn_w_out, w_in, mla_g_q, mla_g_kv, mla_w_uq, mla_w_ukv, na_rpb, hy_conv_w, hy_conv_b, hy_bias, hy_w1, hy_b1, hy_freq1, hy_w2, hy_b2, hy_freq2, hy_w3, s5_lam_re, s5_lam_im, s5_log_dt, s5_b_re, s5_b_im, s5_c_re, s5_c_im, s5_d, s5_w_glu, s5_b_glu, w_branch, w_out):
    p = dict(norm_g=norm_g, ffn_w_in=ffn_w_in, ffn_w_out=ffn_w_out, w_in=w_in, mla_g_q=mla_g_q, mla_g_kv=mla_g_kv,
             mla_w_uq=mla_w_uq, mla_w_ukv=mla_w_ukv, na_rpb=na_rpb, hy_conv_w=hy_conv_w, hy_conv_b=hy_conv_b,
             hy_bias=hy_bias, hy_w1=hy_w1, hy_b1=hy_b1, hy_freq1=hy_freq1, hy_w2=hy_w2, hy_b2=hy_b2,
             hy_freq2=hy_freq2, hy_w3=hy_w3, s5_lam_re=s5_lam_re, s5_lam_im=s5_lam_im, s5_log_dt=s5_log_dt,
             s5_b_re=s5_b_re, s5_b_im=s5_b_im, s5_c_re=s5_c_re, s5_c_im=s5_c_im, s5_d=s5_d, s5_w_glu=s5_w_glu,
             s5_b_glu=s5_b_glu, w_branch=w_branch, w_out=w_out)
    p['s5_ops'] = jax.vmap(_s5_operators)(s5_lam_re, s5_lam_im, s5_log_dt, s5_b_re, s5_b_im, s5_c_re, s5_c_im)
    b, n_lat, d = x.shape
    n_ctx = ctx.shape[1]
    depth = w_mod.shape[0]
    assert b % 8 == 0 and n_lat % 256 == 0 and n_ctx % 256 == 0

    rows = -(-(b + 1) // 8) * 8
    acts = jnp.concatenate([c, c_ctx[None, :], jnp.zeros((rows - b - 1, d), F32)], axis=0)
    mods = _mod_call(acts, w_mod, b_mod).reshape(depth, rows, N_MOD, d)

    z_l, dec_l = _hyena_consts(n_lat)
    z_c, dec_c = _hyena_consts(n_ctx)
    consts = dict(rope_l=_rope_table(n_lat), dft_l=_dft_mats(n_lat), dft_c=_dft_mats(n_ctx),
                  hy_z_l=z_l, hy_dec_l=dec_l, hy_z_c=z_c, hy_dec_c=dec_c)

    xc, xl = ctx, x
    for i in range(depth):
        ctx_out = i < depth - 1
        w = _prep_layer(i, p)
        m = mods[i]
        xl = _ffn_call(xl, m, None, w['g'], w['wi'][0], w['wo'][0], 0, 0)
        xc = _ffn_call(xc, m, b, w['g'], w['wi'][0], w['wo'][0], 0, 0)
        xc, xl = _mixer(xc, xl, m, w, consts, ctx_out)
        xl = _ffn_call(xl, m, None, w['g'], w['wi'][1], w['wo'][1], 6, 4)
        if ctx_out:
            xc = _ffn_call(xc, m, b, w['g'], w['wi'][1], w['wo'][1], 6, 4)
    return xl
```

```python
import functools
import math

import jax
import jax.numpy as jnp
from jax import lax
from jax.experimental import pallas as pl
from jax.experimental.pallas import tpu as pltpu

F32 = jnp.float32
BF16 = jnp.bfloat16

EPS = 1e-6
GRID_W = 64
ROPE_BASE = 10000.0
N_MOD = 9
MLA_HEADS, MLA_NOPE, MLA_ROPE, MLA_V = 4, 128, 64, 128
MLA_RANK = 256
NA_HEADS, NA_DIM, NA_WIN_R, NA_WIN_C = 8, 64, 8, 16
HY_W, HY_EMB, HY_FILT = 512, 33, 64
HY_FAST_PCT, HY_SLOW_PCT, HY_TARGET = 0.3, 1.5, 1e-2
S5_H, S5_P = 16, 64
S5_CHUNK = 16
IN_SPLITS = (256, 64, 512, 512, 512, 256, 512, 1536, 4096)
FFN_CHUNK = 256
VMEM_LIMIT = 56 * 1024 * 1024
FFN_ROWS, INPROJ_ROWS, MERGE_ROWS = 1024, 1024, 512


def _cparams(*sem):
    return pltpu.CompilerParams(dimension_semantics=sem, vmem_limit_bytes=VMEM_LIMIT)


def _const_spec(shape):
    nd = len(shape)
    return pl.BlockSpec(shape, lambda *_: (0,) * nd, pipeline_mode=pl.Buffered(1))


class _Sel:
    def __init__(self, arr, idx):
        self.arr, self.idx, self.shape = arr, tuple(idx), arr.shape[len(idx):]


def _wspec(w):
    if not isinstance(w, _Sel):
        return _const_spec(w.shape)
    idx, nd = w.idx, len(w.shape)
    return pl.BlockSpec((None,) * len(idx) + w.shape, lambda *_: idx + (0,) * nd, pipeline_mode=pl.Buffered(1))


def _warr(w):
    return w.arr if isinstance(w, _Sel) else w


def _dot(a, b):
    return jnp.dot(a, b, preferred_element_type=F32)


def _dot_nt(a, b):
    return lax.dot_general(a, b, (((1,), (1,)), ((), ())), preferred_element_type=F32)


def _sigmoid(x):
    return 1.0 / (1.0 + jnp.exp(-x))


def _rms(x, g):
    return x * lax.rsqrt(jnp.mean(x * x, axis=-1, keepdims=True) + EPS) * g


def _modnorm(x, g, shift, scale):
    return _rms(x, g) * (1.0 + scale) + shift


def _mod_kernel(act_ref, w_ref, b_ref, o_ref):
    a = act_ref[...]
    a = a * _sigmoid(a)
    a_hi = a.astype(BF16)
    a_lo = (a - a_hi.astype(F32)).astype(BF16)
    w = w_ref[0]
    w_hi = w.astype(BF16)
    w_lo = (w - w_hi.astype(F32)).astype(BF16)
    o_ref[0] = _dot(a_hi, w_hi) + _dot(a_lo, w_hi) + _dot(a_hi, w_lo) + b_ref[0]


def _mod_call(acts, w_mod, b_mod):
    nl, d, nd = w_mod.shape
    rows = acts.shape[0]
    tn = 1152
    return pl.pallas_call(
        _mod_kernel,
        out_shape=jax.ShapeDtypeStruct((nl, rows, nd), F32),
        grid=(nl, nd // tn),
        in_specs=[pl.BlockSpec((rows, d), lambda l, j: (0, 0)),
                  pl.BlockSpec((1, d, tn), lambda l, j: (l, 0, j)),
                  pl.BlockSpec((1, 1, tn), lambda l, j: (l, 0, j))],
        out_specs=pl.BlockSpec((1, rows, tn), lambda l, j: (l, 0, j)),
        compiler_params=_cparams("parallel", "parallel"),
        name="mod",
    )(acts, w_mod, b_mod.reshape(nl, 1, nd))


def _ffn_kernel(x_ref, m_ref, g_ref, wi_ref, wo_ref, o_ref, acc_ref, *, base, gidx, n_chunks):
    x = x_ref[0]
    shift, scale, gate = (m_ref[0, base + k:base + k + 1, :] for k in range(3))
    h = _modnorm(x, g_ref[gidx:gidx + 1, :], shift, scale).astype(BF16)
    f = n_chunks * FFN_CHUNK
    for j in range(n_chunks):
        cs = slice(j * FFN_CHUNK, (j + 1) * FFN_CHUNK)
        a = _dot(h, wi_ref[:, cs])
        b = _dot(h, wi_ref[:, f + j * FFN_CHUNK:f + (j + 1) * FFN_CHUNK])
        act = (a * _sigmoid(a) * b).astype(BF16)
        part = _dot(act, wo_ref[cs, :])
        if j == 0:
            acc_ref[...] = part
        else:
            acc_ref[...] += part
    y = acc_ref[...]
    o_ref[0] = x + 0.5 * gate * _rms(y, g_ref[gidx + 1:gidx + 2, :])


def _ffn_call(x, mods, mod_row, g, wi, wo, base, gidx):
    b, t, d = x.shape
    tm = min(FFN_ROWS, t)
    n_chunks = wo.shape[0] // FFN_CHUNK
    mrow = (lambda bi, ti: (bi, 0, 0)) if mod_row is None else (lambda bi, ti: (mod_row, 0, 0))
    return pl.pallas_call(
        functools.partial(_ffn_kernel, base=base, gidx=gidx, n_chunks=n_chunks),
        out_shape=jax.ShapeDtypeStruct(x.shape, F32),
        grid=(b, t // tm),
        in_specs=[pl.BlockSpec((1, tm, d), lambda bi, ti: (bi, ti, 0)),
                  pl.BlockSpec((1, N_MOD, d), mrow),
                  _const_spec(g.shape), _wspec(wi), _wspec(wo)],
        out_specs=pl.BlockSpec((1, tm, d), lambda bi, ti: (bi, ti, 0)),
        scratch_shapes=[pltpu.VMEM((tm, d), F32)],
        compiler_params=_cparams("parallel", "parallel"),
        name="ffn",
    )(x, mods, g, _warr(wi), _warr(wo))


INPROJ_COLS = (640, 512, 1024, 512, 1536)


def _inproj_kernel(x_ref, m_ref, g_ref, w_ref, o_mla, o_nq, o_nkv, o_u, o_hy, u_ref):
    x = x_ref[0]
    tm = x.shape[0]
    h = _modnorm(x, g_ref[2:3, :], m_ref[0, 3:4, :], m_ref[0, 4:5, :]).astype(BF16)
    off = 0
    for o_ref, n in zip((o_mla, o_nq, o_nkv, o_u, o_hy), INPROJ_COLS):
        z = _dot(h, w_ref[:, off:off + n])
        if o_ref is o_u:
            for k in range(n // 128):
                u_ref[k] = z[:, k * 128:(k + 1) * 128]
            for j in range(S5_CHUNK):
                for k in range(n // 128):
                    o_ref[j, 0, :, k * 128:(k + 1) * 128] = u_ref[k, pl.ds(j, tm // S5_CHUNK, stride=S5_CHUNK), :]
        else:
            o_ref[0] = z.astype(o_ref.dtype)
        off += n


def _inproj_call(x, mods, mod_row, g, w):
    b, t, d = x.shape
    tm = min(INPROJ_ROWS, t)
    mrow = (lambda bi, ti: (bi, 0, 0)) if mod_row is None else (lambda bi, ti: (mod_row, 0, 0))
    dts = (F32, BF16, BF16, F32, F32)
    out_shape = [jax.ShapeDtypeStruct((b, t, n), dt) for n, dt in zip(INPROJ_COLS, dts)]
    out_specs = [pl.BlockSpec((1, tm, n), lambda bi, ti: (bi, ti, 0)) for n in INPROJ_COLS]
    nu = INPROJ_COLS[3]
    out_shape[3] = jax.ShapeDtypeStruct((S5_CHUNK, b, t // S5_CHUNK, nu), F32)
    out_specs[3] = pl.BlockSpec((S5_CHUNK, 1, tm // S5_CHUNK, nu), lambda bi, ti: (0, bi, ti, 0))
    return pl.pallas_call(
        _inproj_kernel,
        out_shape=out_shape,
        grid=(b, t // tm),
        in_specs=[pl.BlockSpec((1, tm, d), lambda bi, ti: (bi, ti, 0)),
                  pl.BlockSpec((1, N_MOD, d), mrow),
                  _const_spec(g.shape), _wspec(w)],
        out_specs=out_specs,
        scratch_shapes=[pltpu.VMEM((nu // 128, tm, 128), F32)],
        compiler_params=_cparams("parallel", "parallel"),
        name="inproj",
    )(x, mods, g, _warr(w))


MLA_KW = 256


def _mla_kernel(*refs, n_ctx, n_lat, tq):
    if n_lat:
        zq_ref, zc_ref, zl_ref, tabq_ref, tabk_ref, gq_ref, gkv_ref, wq_ref, wkv_ref, o_ref, k_ref, v_ref = refs
    else:
        zq_ref, zc_ref, tabq_ref, gq_ref, gkv_ref, wq_ref, wkv_ref, o_ref, k_ref, v_ref = refs
    half = lax.broadcasted_iota(jnp.int32, (1, 128), 1) < MLA_ROPE

    def fill_kv(z, row0, tab):
        n = z.shape[0]
        kv = _dot(_rms(z[:, 0:MLA_RANK], gkv_ref[...]).astype(BF16), wkv_ref[...])
        r = z[:, 2 * MLA_RANK:2 * MLA_RANK + 128]
        if tab is None:
            rot = jnp.where(half, r, pltpu.roll(r, MLA_ROPE, 1))
        else:
            r = r * tab
            rot = r + pltpu.roll(r, MLA_ROPE, 1)
        rot = rot.astype(BF16)
        for h in range(MLA_HEADS):
            k_ref[h, pl.ds(row0, n), 0:MLA_NOPE] = kv[:, h * MLA_NOPE:(h + 1) * MLA_NOPE].astype(BF16)
            k_ref[h, pl.ds(row0, n), MLA_NOPE:MLA_KW] = rot
        v_ref[pl.ds(row0, n), :] = kv[:, MLA_HEADS * MLA_NOPE:].astype(BF16)

    @pl.when(pl.program_id(1) == 0)
    def _():
        fill_kv(zc_ref[0], 0, None)
        if n_lat:
            ck = 256

            def body(i, carry):
                r0 = pl.multiple_of(i * ck, ck)
                fill_kv(zl_ref[0, pl.ds(r0, ck), :], pl.multiple_of(n_ctx + r0, 16), tabk_ref[pl.ds(r0, ck), :])
                return carry

            lax.fori_loop(0, n_lat // ck, body, 0)

    zq = zq_ref[0]
    q = _dot(_rms(zq[:, MLA_RANK:2 * MLA_RANK], gq_ref[...]).astype(BF16), wq_ref[...])
    tabq = tabq_ref[...]
    for h in range(MLA_HEADS):
        qh = (q[:, h * MLA_KW:(h + 1) * MLA_KW] * tabq).astype(BF16)
        s = _dot_nt(qh, k_ref[h])
        m = jnp.max(s, axis=-1, keepdims=True)
        p = jnp.exp2(s - m)
        l = jnp.sum(p, axis=-1, keepdims=True)
        o = _dot(p.astype(BF16), v_ref[:, h * MLA_V:(h + 1) * MLA_V])
        o_ref[0, :, h * MLA_V:(h + 1) * MLA_V] = (o / l).astype(o_ref.dtype)


def _mla_call(zq, zc, zl, tabq, tabk, gq, gkv, wq, wkv):
    b, t, w = zq.shape
    n_ctx = zc.shape[1]
    n_lat = 0 if zl is None else zl.shape[1]
    tq = min(512, t)
    n_keys = n_ctx + n_lat
    args = [zq, zc] + ([zl] if n_lat else []) + [tabq] + ([tabk] if n_lat else []) + [gq, gkv, wq, wkv]
    in_specs = [pl.BlockSpec((1, tq, w), lambda bi, qi: (bi, qi, 0)),
                pl.BlockSpec((1, n_ctx, w), lambda bi, qi: (bi, 0, 0))]
    if n_lat:
        in_specs.append(pl.BlockSpec((1, n_lat, w), lambda bi, qi: (bi, 0, 0)))
    in_specs.append(pl.BlockSpec((tq, MLA_KW), lambda bi, qi: (qi, 0)))
    if n_lat:
        in_specs.append(_const_spec(tabk.shape))
    in_specs += [_const_spec(gq.shape), _const_spec(gkv.shape), _const_spec(wq.shape), _const_spec(wkv.shape)]
    return pl.pallas_call(
        functools.partial(_mla_kernel, n_ctx=n_ctx, n_lat=n_lat, tq=tq),
        out_shape=jax.ShapeDtypeStruct((b, t, MLA_HEADS * MLA_V), BF16),
        grid=(b, t // tq),
        in_specs=in_specs,
        out_specs=pl.BlockSpec((1, tq, MLA_HEADS * MLA_V), lambda bi, qi: (bi, qi, 0)),
        scratch_shapes=[pltpu.VMEM((MLA_HEADS, n_keys, MLA_KW), BF16),
                        pltpu.VMEM((n_keys, MLA_HEADS * MLA_V), BF16)],
        compiler_params=_cparams("parallel", "arbitrary"),
        name="mla",
    )(*args)


NA_W = NA_HEADS * NA_DIM
NA_WIN = NA_WIN_R * GRID_W


def _na_kernel(*refs, local, rows):
    if local:
        q_ref, kvc_ref, kvl_ref, bias_ref, o_ref = refs
    else:
        q_ref, kvc_ref, o_ref = refs
    tq = q_ref.shape[1]
    lane_lo = lax.broadcasted_iota(jnp.int32, (1, 128), 1) < NA_DIM
    sub = GRID_W if local else tq
    wins, offs = [], []
    for rr in range(tq // sub if local else 0):
        r = pl.program_id(1) * (tq // sub) + rr
        start = jnp.clip(r - NA_WIN_R // 2, 0, rows - NA_WIN_R)
        offs.append(r - start)
        wins.append(pl.ds(pl.multiple_of(start * GRID_W, GRID_W), NA_WIN))
    scale = NA_DIM ** -0.5
    for j in range(NA_HEADS // 2):
        cols = slice(j * 128, (j + 1) * 128)
        vcols = slice(NA_W + j * 128, NA_W + (j + 1) * 128)
        k_c, v_c = kvc_ref[0, :, cols], kvc_ref[0, :, vcols]
        pieces = []
        for rr in range(tq // sub):
            qs = q_ref[0, rr * sub:(rr + 1) * sub, cols] * jnp.asarray(scale, BF16)
            pieces += [jnp.where(lane_lo, qs, jnp.zeros_like(qs)), jnp.where(lane_lo, jnp.zeros_like(qs), qs)]
        q_all = jnp.concatenate(pieces, axis=0)
        s_c = _dot_nt(q_all, k_c)
        m_c = jnp.max(s_c, axis=-1, keepdims=True)
        p_cs, p_ls, ls = [], [], []
        for rr in range(tq // sub):
            rs = slice(rr * 2 * sub, (rr + 1) * 2 * sub)
            m = m_c[rs]
            if local:
                bias = bias_ref[offs[rr], 2 * j:2 * j + 2].reshape(2 * sub, NA_WIN)
                s_l = _dot_nt(q_all[rs], kvl_ref[0, wins[rr], cols]) + bias
                m = jnp.maximum(m, jnp.max(s_l, axis=-1, keepdims=True))
            p_c = jnp.exp(s_c[rs] - m)
            l = jnp.sum(p_c, axis=-1, keepdims=True)
            if local:
                p_l = jnp.exp(s_l - m)
                l = l + jnp.sum(p_l, axis=-1, keepdims=True)
                p_ls.append(p_l.astype(BF16))
            p_cs.append(p_c.astype(BF16))
            ls.append(l)
        o_c = _dot(jnp.concatenate(p_cs, axis=0), v_c)
        for rr in range(tq // sub):
            rs = slice(rr * 2 * sub, (rr + 1) * 2 * sub)
            o = o_c[rs]
            if local:
                o = o + _dot(p_ls[rr], kvl_ref[0, wins[rr], vcols])
            o = o / ls[rr]
            o_ref[0, rr * sub:(rr + 1) * sub, cols] = jnp.where(lane_lo, o[:sub], o[sub:]).astype(o_ref.dtype)


NA_ROWS_PER_STEP = 4


def _nabias_kernel(r_ref, e_ref, m_ref, o_ref):
    o_ref[...] = jnp.dot(r_ref[...], e_ref[...], precision=lax.Precision.HIGHEST,
                         preferred_element_type=F32) + m_ref[...]


def _na_bias_table(rpb):
    h, na, nb = rpb.shape
    col = jnp.arange(GRID_W)
    c0 = jnp.clip(col - NA_WIN_C // 2, 0, GRID_W - NA_WIN_C)
    in_win = (col[None, :] >= c0[:, None]) & (col[None, :] < c0[:, None] + NA_WIN_C)
    dc = col[None, :] - col[:, None] + NA_WIN_C - 1
    onehot = (jnp.arange(128)[:, None, None] == dc[None]) & in_win[None]
    onehot = onehot.astype(F32).reshape(128, GRID_W * GRID_W)
    mask = jnp.where(in_win, 0.0, -jnp.inf).astype(F32).reshape(1, GRID_W * GRID_W)
    r = jnp.pad(rpb.reshape(h * na, nb), ((0, 128 - h * na), (0, 128 - nb)))
    t = pl.pallas_call(
        _nabias_kernel,
        out_shape=jax.ShapeDtypeStruct((128, GRID_W * GRID_W), F32),
        name="nabias",
    )(r, onehot, mask)
    t = t[:h * na].reshape(h, na, GRID_W, GRID_W)
    tabs = [t[:, NA_WIN_R - 1 - o:2 * NA_WIN_R - 1 - o].transpose(0, 2, 1, 3).reshape(h, GRID_W, NA_WIN)
            for o in range(NA_WIN_R)]
    return jnp.stack(tabs, axis=0)


def _na_call(q, kvc, kvl, bias):
    b, t, _ = q.shape
    n_ctx = kvc.shape[1]
    local = kvl is not None
    rows = t // GRID_W
    if local:
        assert rows >= NA_WIN_R and rows % NA_ROWS_PER_STEP == 0
    tq = GRID_W * NA_ROWS_PER_STEP if local else t
    args = [q, kvc] + ([kvl, bias] if local else [])
    in_specs = [pl.BlockSpec((1, tq, NA_W), lambda bi, ri: (bi, ri, 0)),
                pl.BlockSpec((1, n_ctx, 2 * NA_W), lambda bi, ri: (bi, 0, 0))]
    if local:
        in_specs += [pl.BlockSpec((1, t, 2 * NA_W), lambda bi, ri: (bi, 0, 0)), _const_spec(bias.shape)]
    return pl.pallas_call(
        functools.partial(_na_kernel, local=local, rows=rows),
        out_shape=jax.ShapeDtypeStruct((b, t, NA_W), BF16),
        grid=(b, t // tq),
        in_specs=in_specs,
        out_specs=pl.BlockSpec((1, tq, NA_W), lambda bi, ri: (bi, ri, 0)),
        compiler_params=_cparams("parallel", "parallel"),
        name="na",
    )(*args)


def _hyfilt_kernel(z_ref, w1_ref, b1_ref, f1_ref, w2_ref, b2_ref, f2_ref, w3_ref, dec_ref, cm_ref, sm_ref,
                   hre_ref, him_ref, hny_ref):
    hi = lax.Precision.HIGHEST
    t = z_ref.shape[0]
    n_fft = 2 * t
    h = jnp.sin(f1_ref[...] * (jnp.dot(z_ref[...], w1_ref[...], precision=hi, preferred_element_type=F32) + b1_ref[...]))
    h = jnp.sin(f2_ref[...] * (jnp.dot(h, w2_ref[...], precision=hi, preferred_element_type=F32) + b2_ref[...]))
    taps = jnp.dot(h, w3_ref[...], precision=hi, preferred_element_type=F32)
    dec = dec_ref[...]
    hf = taps[:, :HY_W] * dec
    hb = taps[:, HY_W:] * dec
    row = lax.broadcasted_iota(jnp.int32, (t, 1), 0)
    wgt = jnp.where(row == 0, 1.0 / n_fft, 2.0 / n_fft)
    hsum = hf + hb
    hre_ref[...] = _dot(cm_ref[...], hsum.astype(BF16)) * wgt
    him_ref[...] = _dot(sm_ref[...], (hb - hf).astype(BF16)) * wgt
    alt = (1 - 2 * (row & 1)).astype(F32)
    hny = jnp.sum(hsum * alt, axis=0, keepdims=True) * (1.0 / n_fft)
    hny_ref[...] = jnp.broadcast_to(hny, hny_ref.shape)


def _hyfilt_call(z, w1, b1, f1, w2, b2, f2, w3, dec, cm, sm):
    t = z.shape[0]
    args = (z, w1, b1, f1, w2, b2, f2, w3, dec, cm, sm)
    return pl.pallas_call(
        _hyfilt_kernel,
        out_shape=[jax.ShapeDtypeStruct((t, HY_W), F32), jax.ShapeDtypeStruct((t, HY_W), F32),
                   jax.ShapeDtypeStruct((8, HY_W), F32)],
        grid=(1,),
        in_specs=[_const_spec(a.shape) for a in args],
        out_specs=[pl.BlockSpec((t, HY_W), lambda i: (0, 0)), pl.BlockSpec((t, HY_W), lambda i: (0, 0)),
                   pl.BlockSpec((8, HY_W), lambda i: (0, 0))],
        compiler_params=_cparams("arbitrary"),
        name="hyfilt",
    )(*args)


HY_CB = 256
HY_FB = 512


def _hyena_kernel(zv_ref, z1_ref, z0_ref, wv_ref, w1_ref, w0_ref, bv_ref, b1_ref, b0_ref, bd_ref,
                  cm_ref, sm_ref, hre_ref, him_ref, hny_ref, o_ref, y_ref):
    t = zv_ref.shape[1]
    row = lax.broadcasted_iota(jnp.int32, (t, 1), 0)

    def conv3(z_ref, w_ref, b_ref):
        z = z_ref[0]
        prev = jnp.where(row == 0, 0.0, pltpu.roll(z, 1, 0))
        nxt = jnp.where(row == t - 1, 0.0, pltpu.roll(z, t - 1, 0))
        return w_ref[0:1, :] * prev + w_ref[1:2, :] * z + w_ref[2:3, :] * nxt + b_ref[...]

    s = conv3(zv_ref, wv_ref, bv_ref) * conv3(z1_ref, w1_ref, b1_ref)
    alt = (1 - 2 * (row & 1)).astype(F32)
    sb = s.astype(BF16)
    xny = jnp.sum(s * alt, axis=0, keepdims=True)
    y_ref[...] = alt * (xny * hny_ref[0:1, :]) + s * bd_ref[...]
    fb = min(HY_FB, t)
    for f0 in range(0, t, fb):
        fr = slice(f0, f0 + fb)
        a = _dot(cm_ref[fr, :], sb)
        bq = _dot(sm_ref[fr, :], sb)
        hre = hre_ref[fr, :]
        him = him_ref[fr, :]
        yc = (a * hre + bq * him).astype(BF16)
        ys = (bq * hre - a * him).astype(BF16)
        y_ref[...] += _dot(cm_ref[:, fr], yc) + _dot(sm_ref[:, fr], ys)
    o_ref[0] = (conv3(z0_ref, w0_ref, b0_ref) * y_ref[...]).astype(o_ref.dtype)


def _hyena_call(z, conv_w, conv_b, bias_d, cm, sm, hre, him, hny):
    b, t, _ = z.shape
    nb = HY_W // HY_CB
    zspec = lambda k: pl.BlockSpec((1, t, HY_CB), lambda bi, ci: (bi, 0, k * nb + ci))
    wspec = lambda k: pl.BlockSpec((3, HY_CB), lambda bi, ci: (0, k * nb + ci))
    bspec = lambda k: pl.BlockSpec((1, HY_CB), lambda bi, ci: (0, k * nb + ci))
    cspec = lambda rows: pl.BlockSpec((rows, HY_CB), lambda bi, ci: (0, ci))
    conv_b = conv_b.reshape(1, -1)
    return pl.pallas_call(
        _hyena_kernel,
        out_shape=jax.ShapeDtypeStruct((b, t, HY_W), BF16),
        grid=(b, nb),
        in_specs=[zspec(0), zspec(1), zspec(2), wspec(0), wspec(1), wspec(2), bspec(0), bspec(1), bspec(2),
                  cspec(1), _const_spec(cm.shape), _const_spec(sm.shape), cspec(t), cspec(t), cspec(8)],
        out_specs=pl.BlockSpec((1, t, HY_CB), lambda bi, ci: (bi, 0, ci)),
        scratch_shapes=[pltpu.VMEM((t, HY_CB), F32)],
        compiler_params=_cparams("parallel", "parallel"),
        name="hyena",
    )(z, z, z, conv_w, conv_w, conv_w, conv_b, conv_b, conv_b, bias_d.reshape(1, -1), cm, sm, hre, him, hny)


S5_CW = S5_CHUNK * S5_H


S5_GPS = 128 // S5_H
S5_NB = 4
S5_RB = 64


def _block_transpose(xs, blk):
    xs = list(xs)
    n = len(xs)
    d = n // 2
    while d:
        keep = (blk & d) == 0
        new = list(xs)
        for i in range(n):
            if i & d:
                continue
            a, b = xs[i], xs[i + d]
            new[i] = jnp.where(keep, a, pltpu.roll(b, d * S5_H, 1))
            new[i + d] = jnp.where(keep, pltpu.roll(a, 128 - d * S5_H, 1), b)
        xs = new
        d //= 2
    return xs


def _s5_kernel(xc_ref, xl_ref, m_ref, w_ref, v_ref, lam_ref, d_ref, yc_ref, yl_ref,
               u_ref, s_ref, xpf_ref, xpb_ref, y_ref, *, nb, n_cc, n_lc):
    parts = ((xc_ref, yc_ref, 0, n_cc), (xl_ref, yl_ref, nb * n_cc, n_lc))
    blk = lax.broadcasted_iota(jnp.int32, (1, 128), 1) // S5_H
    fwd = lax.broadcasted_iota(jnp.int32, (1, 128), 1) < S5_P

    def row_blocks(n_chunks):
        rb = min(S5_RB, n_chunks)
        return [(c0, rb) for c0 in range(0, n_chunks, rb)]

    def gather_rows(b, carry):
        for x_ref, _, base, n_chunks in parts:
            for c0, cn in row_blocks(n_chunks):
                xs = [x_ref[j, b, c0:c0 + cn, :] for j in range(S5_CHUNK)]
                r0 = pl.multiple_of(base + b * n_chunks + c0, 16)
                halves = [_block_transpose(xs[h * S5_GPS:(h + 1) * S5_GPS], blk) for h in range(2)]
                for g in range(S5_GPS):
                    u_ref[g, pl.ds(r0, cn), :] = jnp.concatenate([halves[0][g], halves[1][g]], axis=1).astype(BF16)
        return carry

    lax.fori_loop(0, nb, gather_rows, 0)

    for g in range(S5_GPS):
        s = _dot(u_ref[g], w_ref[g])
        s_ref[g, 0] = s[:, 0:128]
        s_ref[g, 1] = s[:, 128:256]

    def scan_part(carry, base, n_chunks):
        def body(k, carry):
            kb = n_chunks - 1 - k
            rf = pl.ds(base + k, nb, stride=n_chunks)
            rb = pl.ds(base + kb, nb, stride=n_chunks)
            new = []
            for g in range(S5_GPS):
                xa, xb = carry[2 * g], carry[2 * g + 1]
                ar, ai = lam_ref[g, 0:1, :], lam_ref[g, 1:2, :]
                xpf_ref[g, 0, rf, :] = xa
                xpf_ref[g, 1, rf, :] = xb
                xpb_ref[g, 0, rb, :] = xa
                xpb_ref[g, 1, rb, :] = xb
                in_a = jnp.where(fwd, s_ref[g, 0, rf, :], s_ref[g, 0, rb, :])
                in_b = jnp.where(fwd, s_ref[g, 1, rf, :], s_ref[g, 1, rb, :])
                new += [ar * xa - ai * xb + in_a, ar * xb + ai * xa + in_b]
            return tuple(new)

        return lax.fori_loop(0, n_chunks, body, carry)

    carry = tuple(jnp.zeros((nb, 128), F32) for _ in range(2 * S5_GPS))
    for _, _, base, n_chunks in parts:
        carry = scan_part(carry, base, n_chunks)

    for g in range(S5_GPS):
        xp = jnp.concatenate([jnp.where(fwd, xpf_ref[g, h], xpb_ref[g, h]) for h in range(2)], axis=1).astype(BF16)
        y_ref[g] = _dot(u_ref[g], m_ref[g]) + _dot(xp, v_ref[g])

    def scatter_rows(b, carry):
        for x_ref, o_ref, base, n_chunks in parts:
            for c0, cn in row_blocks(n_chunks):
                r0 = pl.multiple_of(base + b * n_chunks + c0, 16)
                ys = [y_ref[g, pl.ds(r0, cn), :] for g in range(S5_GPS)]
                halves = [_block_transpose([y[:, h * 128:(h + 1) * 128] for y in ys], blk) for h in range(2)]
                for j in range(S5_CHUNK):
                    half, jj = divmod(j, S5_GPS)
                    y = halves[half][jj] + d_ref[...] * x_ref[j, b, c0:c0 + cn, :]
                    cdf = 0.5 * (1.0 + jnp.tanh(math.sqrt(2.0 / math.pi) * (y + 0.044715 * (y * y * y))))
                    o_ref[j, b, c0:c0 + cn, :] = y * cdf
        return carry

    lax.fori_loop(0, nb, scatter_rows, 0)


def _s5_call(xc, xl, m, w, v, lam, d):
    _, b, n_cc, width = xc.shape
    n_lc = xl.shape[2]
    nb = min(S5_NB, b)
    rows = nb * (n_cc + n_lc)
    xspec = lambda n: pl.BlockSpec((S5_CHUNK, nb, n, 128), lambda mi, bi: (0, bi, 0, mi))
    gspec = lambda a: pl.BlockSpec((S5_GPS,) + a.shape[1:], lambda mi, bi: (mi, 0, 0))
    return pl.pallas_call(
        functools.partial(_s5_kernel, nb=nb, n_cc=n_cc, n_lc=n_lc),
        out_shape=[jax.ShapeDtypeStruct(xc.shape, F32), jax.ShapeDtypeStruct(xl.shape, F32)],
        grid=(width // 128, b // nb),
        in_specs=[xspec(n_cc), xspec(n_lc), gspec(m), gspec(w), gspec(v), gspec(lam),
                  pl.BlockSpec((1, 128), lambda mi, bi: (0, mi))],
        out_specs=[xspec(n_cc), xspec(n_lc)],
        scratch_shapes=[pltpu.VMEM((S5_GPS, rows, S5_CW), BF16), pltpu.VMEM((S5_GPS, 2, rows, 128), F32),
                        pltpu.VMEM((S5_GPS, 2, rows, 128), F32), pltpu.VMEM((S5_GPS, 2, rows, 128), F32),
                        pltpu.VMEM((S5_GPS, rows, S5_CW), F32)],
        compiler_params=_cparams("parallel", "parallel"),
        name="s5",
    )(xc, xl, m, w, v, lam, d)


def _merge_kernel(x_ref, m_ref, g_ref, a_ref, b_ref, c_ref, d_ref, wg_ref, wb_ref, wglu_ref, bglu_ref, wo_ref,
                  o_ref, acc_ref, cy_ref):
    x = x_ref[0]
    tm, d = x.shape
    h = _modnorm(x, g_ref[2:3, :], m_ref[0, 3:4, :], m_ref[0, 4:5, :]).astype(BF16)
    for j in range(S5_CHUNK):
        for k in range(cy_ref.shape[0]):
            cy_ref[k, pl.ds(j, tm // S5_CHUNK, stride=S5_CHUNK), :] = c_ref[j, 0, :, k * 128:(k + 1) * 128]
    cy = jnp.concatenate([cy_ref[k] for k in range(cy_ref.shape[0])], axis=1)
    glu = _dot(cy.astype(BF16), wglu_ref[...]) + bglu_ref[...]
    nw = glu.shape[-1] // 2
    c = (glu[:, :nw] * _sigmoid(glu[:, nw:])).astype(BF16)
    for n, br in enumerate((a_ref[0], b_ref[0], c, d_ref[0])):
        gate = _sigmoid(_dot(h, wg_ref[:, n * d:(n + 1) * d]))
        part = gate * _dot(br, wb_ref[n])
        if n == 0:
            acc_ref[...] = part
        else:
            acc_ref[...] += part
    y = _dot(acc_ref[...].astype(BF16), wo_ref[...])
    o_ref[0] = x + m_ref[0, 5:6, :] * _rms(y, g_ref[3:4, :])


def _merge_call(x, mods, mod_row, g, a, b_, c, d_, wg, wb, wglu, bglu, wo):
    b, t, d = x.shape
    tm = min(MERGE_ROWS, t)
    mrow = (lambda bi, ti: (bi, 0, 0)) if mod_row is None else (lambda bi, ti: (mod_row, 0, 0))
    row = lambda n: pl.BlockSpec((1, tm, n), lambda bi, ti: (bi, ti, 0))
    bw = a.shape[-1]
    return pl.pallas_call(
        _merge_kernel,
        out_shape=jax.ShapeDtypeStruct(x.shape, F32),
        grid=(b, t // tm),
        in_specs=[row(d), pl.BlockSpec((1, N_MOD, d), mrow), _const_spec(g.shape),
                  row(bw), row(bw),
                  pl.BlockSpec((S5_CHUNK, 1, tm // S5_CHUNK, bw), lambda bi, ti: (0, bi, ti, 0)), row(bw),
                  _wspec(wg), _wspec(wb), _wspec(wglu), _const_spec(bglu.shape), _wspec(wo)],
        out_specs=row(d),
        scratch_shapes=[pltpu.VMEM((tm, d), F32), pltpu.VMEM((bw // 128, tm, 128), F32)],
        compiler_params=_cparams("parallel", "parallel"),
        name="merge",
    )(x, mods, g, a, b_, c, d_, _warr(wg), _warr(wb), _warr(wglu), bglu, _warr(wo))


def _rope_table(n_tok):
    q = MLA_ROPE // 4
    t = jnp.arange(n_tok)
    pos = jnp.stack([t // GRID_W, t % GRID_W], axis=-1).astype(F32)
    inv = ROPE_BASE ** (-jnp.arange(q, dtype=F32) / q)
    ang = pos[:, :, None] * inv
    cos, sin = jnp.cos(ang), jnp.sin(ang)
    cos_t = jnp.stack([cos, cos], axis=2).reshape(n_tok, MLA_ROPE)
    sin_t = jnp.stack([-sin, sin], axis=2).reshape(n_tok, MLA_ROPE)
    return jnp.concatenate([cos_t, sin_t], axis=-1)


def _rope_swap(w):
    q = MLA_ROPE // 4
    return w.reshape(w.shape[:-1] + (2, 2, q))[..., ::-1, :].reshape(w.shape)


def _dft_mats(t):
    n = 2 * t
    nb = 64 if t % 64 == 0 else 1
    tt = jnp.arange(t, dtype=jnp.int32)[None, :]
    ang = lambda f: ((f[:, None] * tt) % n).astype(F32) * (2.0 * math.pi / n)
    aa = ang(jnp.arange(t // nb, dtype=jnp.int32) * nb)[:, None, :]
    ab = ang(jnp.arange(nb, dtype=jnp.int32))[None, :, :]
    ca, sa, cb, sb = lax.optimization_barrier((jnp.cos(aa), jnp.sin(aa), jnp.cos(ab), jnp.sin(ab)))
    cm = (ca * cb - sa * sb).reshape(t, t)
    sm = (sa * cb + ca * sb).reshape(t, t)
    return cm.astype(BF16), sm.astype(BF16)


def _hyena_consts(n_tok):
    bands = (HY_EMB - 1) // 2
    t = jnp.arange(n_tok, dtype=F32)
    t01 = jnp.linspace(0.0, 1.0, n_tok, dtype=F32)[:, None]
    ang = (2.0 * math.pi * t / n_tok)[:, None] * jnp.linspace(1e-4, bands - 1, bands, dtype=F32)
    z = jnp.concatenate([t01, jnp.cos(ang), -jnp.sin(ang)], axis=-1)
    z = jnp.pad(z, ((0, 0), (0, 128 - HY_EMB)))
    max_decay = math.log(HY_TARGET) / HY_FAST_PCT
    min_decay = math.log(HY_TARGET) / HY_SLOW_PCT
    deltas = jnp.abs(jnp.linspace(min_decay, max_decay, HY_W, dtype=F32))
    return z, jnp.exp(-t01 * deltas)


def _s5_operators(lam_re, lam_im, log_dt, b_re, b_im, c_re, c_im):
    tc = S5_CHUNK
    lam = lax.complex(jnp.minimum(lam_re, -1e-4), lam_im)
    lam_dt = lam * jnp.exp(log_dt)[..., None]
    lam_bar = jnp.exp(lam_dt)
    b_bar = ((lam_bar - 1.0) / lam)[..., None] * lax.complex(b_re, b_im)
    cm = lax.complex(c_re, c_im)
    pw = jnp.exp(lam_dt[..., None] * jnp.arange(tc + 1, dtype=F32))
    g = lam_re.shape[1]
    kern = jnp.real(jnp.einsum('dgap,dgpk,dgph->dkgah', cm, pw[..., :tc], b_bar,
                               precision=lax.Precision.HIGHEST))
    s = jnp.arange(tc)[:, None, None]
    j = jnp.arange(tc)[None, :, None]
    lag = jnp.arange(tc)[None, None, :]
    place = jnp.stack([(j - s == lag), (s - j == lag)]).astype(F32)
    m = jnp.einsum('dsjk,dkgah->gshja', place, kern, precision=lax.Precision.HIGHEST)
    m = m.reshape(g, S5_CW, S5_CW)
    pf = pw[0][..., :tc][..., ::-1]
    pb = pw[1][..., :tc]
    wf = jnp.einsum('gps,gph->gshp', pf, b_bar[0]).reshape(g, S5_CW, S5_P)
    wb = jnp.einsum('gps,gph->gshp', pb, b_bar[1]).reshape(g, S5_CW, S5_P)
    w = jnp.concatenate([wf.real, wb.real, wf.imag, wb.imag], axis=-1)
    gf = jnp.einsum('gap,gpj->gpja', cm[0], pw[0][..., 1:tc + 1]).reshape(g, S5_P, S5_CW)
    gb = jnp.einsum('gap,gpj->gpja', cm[1], pw[1][..., 1:tc + 1][..., ::-1]).reshape(g, S5_P, S5_CW)
    v = jnp.concatenate([gf.real, gb.real, -gf.imag, -gb.imag], axis=1)
    a16 = pw[..., tc]
    lam16 = jnp.stack([jnp.concatenate([a16[0].real, a16[1].real], -1),
                       jnp.concatenate([a16[0].imag, a16[1].imag], -1)], axis=1)
    lam16 = jnp.pad(lam16, ((0, 0), (0, 6), (0, 0)))
    return m.astype(BF16), w.astype(BF16), v.astype(BF16), lam16


def _prep_layer(i, p):
    wi = [_Sel(p['ffn_wi_bf16'], (i, k)) for k in range(2)]
    wo = [_Sel(p['ffn_wo_bf16'], (i, k)) for k in range(2)]
    w_proj = _Sel(p['w_proj_bf16'], (i,))
    w_gate = _Sel(p['w_gate_bf16'], (i,))
    wuq = p['mla_w_uq'][i]
    rope = wuq[..., MLA_NOPE:]
    wq = jnp.concatenate([wuq[..., :MLA_NOPE], rope, _rope_swap(rope)], axis=-1)
    wq = wq.reshape(MLA_RANK, MLA_HEADS * MLA_KW).astype(BF16)
    wukv = p['mla_w_ukv'][i]
    wkv = jnp.concatenate([wukv[..., :MLA_NOPE].reshape(MLA_RANK, -1), wukv[..., MLA_NOPE:].reshape(MLA_RANK, -1)],
                          axis=1).astype(BF16)
    s5m, s5w, s5v, s5lam = (a[i] for a in p['s5_ops'])
    s5d = p['s5_d'][i].reshape(1, -1)
    return dict(
        g=p['norm_g'][i], wi=wi, wo=wo, w_proj=w_proj, w_gate=w_gate,
        gq=p['mla_g_q'][i].reshape(1, -1), gkv=p['mla_g_kv'][i].reshape(1, -1), wq=wq, wkv=wkv,
        bias=_na_bias_table(p['na_rpb'][i]),
        hy_mlp=(jnp.pad(p['hy_w1'][i], ((0, 128 - HY_EMB), (0, 0))), p['hy_b1'][i].reshape(1, -1),
                p['hy_freq1'][i].reshape(1, -1), p['hy_w2'][i], p['hy_b2'][i].reshape(1, -1),
                p['hy_freq2'][i].reshape(1, -1), p['hy_w3'][i]),
        conv_w=p['hy_conv_w'][i], conv_b=p['hy_conv_b'][i], hy_bias=p['hy_bias'][i],
        s5m=s5m, s5w=s5w, s5v=s5v, s5lam=s5lam, s5d=s5d,
        wglu=_Sel(p['wglu_bf16'], (i,)), bglu=p['s5_b_glu'][i].reshape(1, -1),
        wb=_Sel(p['wb_bf16'], (i,)), w_out=_Sel(p['w_out_bf16'], (i,)),
    )


def _stacked_bf16(p):
    w_in = p['w_in']
    seg = lambda k: w_in[:, :, sum(IN_SPLITS[:k]):sum(IN_SPLITS[:k + 1])]
    kr = seg(1)
    w_proj = jnp.concatenate([seg(0), seg(5), kr, _rope_swap(kr), seg(6), seg(2), seg(3), seg(4), seg(7)], axis=2)
    return dict(ffn_wi_bf16=p['ffn_w_in'].astype(BF16), ffn_wo_bf16=p['ffn_w_out'].astype(BF16),
                w_proj_bf16=w_proj.astype(BF16), w_gate_bf16=seg(8).astype(BF16),
                wglu_bf16=p['s5_w_glu'].astype(BF16), wb_bf16=p['w_branch'].astype(BF16),
                w_out_bf16=p['w_out'].astype(BF16))


def _mixer(xc, xl, mods, w, consts, ctx_out):
    b, n_lat, _ = xl.shape
    n_ctx = xc.shape[1]
    ctx_row = b
    mla_l, nq_l, nkv_l, u_l, hy_l = _inproj_call(xl, mods, None, w['g'], w['w_proj'])
    mla_c, nq_c, nkv_c, u_c, hy_c = _inproj_call(xc, mods, ctx_row, w['g'], w['w_proj'])

    scale = (MLA_NOPE + MLA_ROPE) ** -0.5 * math.log2(math.e)
    rope_l = consts['rope_l']
    tabq_l = jnp.concatenate([jnp.ones((n_lat, MLA_NOPE), F32), rope_l], axis=-1) * scale
    a_l = _mla_call(mla_l, mla_c, mla_l, tabq_l, rope_l, w['gq'], w['gkv'], w['wq'], w['wkv'])
    d_l = _na_call(nq_l, nkv_c, nkv_l, w['bias'])
    cy_c, cy_l = _s5_call(u_c, u_l, w['s5m'], w['s5w'], w['s5v'], w['s5lam'], w['s5d'])
    hre, him, hny = _hyfilt_call(consts['hy_z_l'], *w['hy_mlp'], consts['hy_dec_l'], *consts['dft_l'])
    b_l = _hyena_call(hy_l, w['conv_w'], w['conv_b'], w['hy_bias'], *consts['dft_l'], hre, him, hny)
    merge_w = (w['w_gate'], w['wb'], w['wglu'], w['bglu'], w['w_out'])
    xl = _merge_call(xl, mods, None, w['g'], a_l, b_l, cy_l, d_l, *merge_w)
    if not ctx_out:
        return None, xl

    tabq_c = jnp.concatenate([jnp.ones((n_ctx, MLA_NOPE + MLA_ROPE), F32), jnp.zeros((n_ctx, MLA_ROPE), F32)],
                             axis=-1) * scale
    a_c = _mla_call(mla_c, mla_c, None, tabq_c, None, w['gq'], w['gkv'], w['wq'], w['wkv'])
    d_c = _na_call(nq_c, nkv_c, None, None)
    hre, him, hny = _hyfilt_call(consts['hy_z_c'], *w['hy_mlp'], consts['hy_dec_c'], *consts['dft_c'])
    b_c = _hyena_call(hy_c, w['conv_w'], w['conv_b'], w['hy_bias'], *consts['dft_c'], hre, him, hny)
    xc = _merge_call(xc, mods, ctx_row, w['g'], a_c, b_c, cy_c, d_c, *merge_w)
    return xc, xl


def kernel(x, c, ctx, c_ctx, w_mod, b_mod, norm_g, ffn_w_in, ffn_w_out, w_in, mla_g_q, mla_g_kv, mla_w_uq, mla_w_ukv, na_rpb, hy_conv_w, hy_conv_b, hy_bias, hy_w1, hy_b1, hy_freq1, hy_w2, hy_b2, hy_freq2, hy_w3, s5_lam_re, s5_lam_im, s5_log_dt, s5_b_re, s5_b_im, s5_c_re, s5_c_im, s5_d, s5_w_glu, s5_b_glu, w_branch, w_out):
    p = dict(norm_g=norm_g, ffn_w_in=ffn_w_in, ffn_w_out=ffn_w_out, w_in=w_in, mla_g_q=mla_g_q, mla_g_kv=mla_g_kv,
             mla_w_uq=mla_w_uq, mla_w_ukv=mla_w_ukv, na_rpb=na_rpb, hy_conv_w=hy_conv_w, hy_conv_b=hy_conv_b,
             hy_bias=hy_bias, hy_w1=hy_w1, hy_b1=hy_b1, hy_freq1=hy_freq1, hy_w2=hy_w2, hy_b2=hy_b2,
             hy_freq2=hy_freq2, hy_w3=hy_w3, s5_lam_re=s5_lam_re, s5_lam_im=s5_lam_im, s5_log_dt=s5_log_dt,
             s5_b_re=s5_b_re, s5_b_im=s5_b_im, s5_c_re=s5_c_re, s5_c_im=s5_c_im, s5_d=s5_d, s5_w_glu=s5_w_glu,
             s5_b_glu=s5_b_glu, w_branch=w_branch, w_out=w_out)
    p.update(_stacked_bf16(p))
    p['s5_ops'] = jax.vmap(_s5_operators)(s5_lam_re, s5_lam_im, s5_log_dt, s5_b_re, s5_b_im, s5_c_re, s5_c_im)
    b, n_lat, d = x.shape
    n_ctx = ctx.shape[1]
    depth = w_mod.shape[0]
    assert b % 8 == 0 and n_lat % 256 == 0 and n_ctx % 256 == 0

    rows = -(-(b + 1) // 8) * 8
    acts = jnp.concatenate([c, c_ctx[None, :], jnp.zeros((rows - b - 1, d), F32)], axis=0)
    mods = _mod_call(acts, w_mod, b_mod).reshape(depth, rows, N_MOD, d)

    z_l, dec_l = _hyena_consts(n_lat)
    z_c, dec_c = _hyena_consts(n_ctx)
    consts = dict(rope_l=_rope_table(n_lat), dft_l=_dft_mats(n_lat), dft_c=_dft_mats(n_ctx),
                  hy_z_l=z_l, hy_dec_l=dec_l, hy_z_c=z_c, hy_dec_c=dec_c)

    xc, xl = ctx, x
    for i in range(depth):
        ctx_out = i < depth - 1
        w = _prep_layer(i, p)
        m = mods[i]
        xl = _ffn_call(xl, m, None, w['g'], w['wi'][0], w['wo'][0], 0, 0)
        xc = _ffn_call(xc, m, b, w['g'], w['wi'][0], w['wo'][0], 0, 0)
        xc, xl = _mixer(xc, xl, m, w, consts, ctx_out)
        xl = _ffn_call(xl, m, None, w['g'], w['wi'][1], w['wo'][1], 6, 4)
        if ctx_out:
            xc = _ffn_call(xc, m, b, w['g'], w['wi'][1], w['wo'][1], 6, 4)
    return xl
```

```python
import functools
import math

import jax
import jax.numpy as jnp
from jax import lax
from jax.experimental import pallas as pl
from jax.experimental.pallas import tpu as pltpu

F32 = jnp.float32
BF16 = jnp.bfloat16

EPS = 1e-6
GRID_W = 64
ROPE_BASE = 10000.0
N_MOD = 9
MLA_HEADS, MLA_NOPE, MLA_ROPE, MLA_V = 4, 128, 64, 128
MLA_RANK = 256
NA_HEADS, NA_DIM, NA_WIN_R, NA_WIN_C = 8, 64, 8, 16
HY_W, HY_EMB, HY_FILT = 512, 33, 64
HY_FAST_PCT, HY_SLOW_PCT, HY_TARGET = 0.3, 1.5, 1e-2
S5_H, S5_P = 16, 64
S5_CHUNK = 16
IN_SPLITS = (256, 64, 512, 512, 512, 256, 512, 1536, 4096)
FFN_CHUNK = 256
VMEM_LIMIT = 56 * 1024 * 1024
FFN_ROWS, INPROJ_ROWS, MERGE_ROWS = 1024, 1024, 512


def _cparams(*sem):
    return pltpu.CompilerParams(dimension_semantics=sem, vmem_limit_bytes=VMEM_LIMIT)


def _const_spec(shape):
    nd = len(shape)
    return pl.BlockSpec(shape, lambda *_: (0,) * nd, pipeline_mode=pl.Buffered(1))


class _Sel:
    def __init__(self, arr, idx):
        self.arr, self.idx, self.shape = arr, tuple(idx), arr.shape[len(idx):]


def _wspec(w):
    if not isinstance(w, _Sel):
        return _const_spec(w.shape)
    idx, nd = w.idx, len(w.shape)
    return pl.BlockSpec((None,) * len(idx) + w.shape, lambda *_: idx + (0,) * nd, pipeline_mode=pl.Buffered(1))


def _warr(w):
    return w.arr if isinstance(w, _Sel) else w


def _dot(a, b):
    return jnp.dot(a, b, preferred_element_type=F32)


def _dot_nt(a, b):
    return lax.dot_general(a, b, (((1,), (1,)), ((), ())), preferred_element_type=F32)


def _sigmoid(x):
    return 1.0 / (1.0 + jnp.exp(-x))


def _rms(x, g):
    return x * lax.rsqrt(jnp.mean(x * x, axis=-1, keepdims=True) + EPS) * g


def _modnorm(x, g, shift, scale):
    return _rms(x, g) * (1.0 + scale) + shift


def _mod_kernel(act_ref, w_ref, b_ref, o_ref):
    a = act_ref[...]
    a = a * _sigmoid(a)
    a_hi = a.astype(BF16)
    a_lo = (a - a_hi.astype(F32)).astype(BF16)
    w = w_ref[0]
    w_hi = w.astype(BF16)
    w_lo = (w - w_hi.astype(F32)).astype(BF16)
    o_ref[0] = _dot(a_hi, w_hi) + _dot(a_lo, w_hi) + _dot(a_hi, w_lo) + b_ref[0]


def _mod_call(acts, w_mod, b_mod):
    nl, d, nd = w_mod.shape
    rows = acts.shape[0]
    tn = 1152
    return pl.pallas_call(
        _mod_kernel,
        out_shape=jax.ShapeDtypeStruct((nl, rows, nd), F32),
        grid=(nl, nd // tn),
        in_specs=[pl.BlockSpec((rows, d), lambda l, j: (0, 0)),
                  pl.BlockSpec((1, d, tn), lambda l, j: (l, 0, j)),
                  pl.BlockSpec((1, 1, tn), lambda l, j: (l, 0, j))],
        out_specs=pl.BlockSpec((1, rows, tn), lambda l, j: (l, 0, j)),
        compiler_params=_cparams("parallel", "parallel"),
        name="mod",
    )(acts, w_mod, b_mod.reshape(nl, 1, nd))


def _ffn_kernel(x_ref, m_ref, g_ref, wi_ref, wo_ref, o_ref, acc_ref, *, base, gidx, n_chunks):
    x = x_ref[0]
    shift, scale, gate = (m_ref[0, base + k:base + k + 1, :] for k in range(3))
    h = _modnorm(x, g_ref[gidx:gidx + 1, :], shift, scale).astype(BF16)
    f = n_chunks * FFN_CHUNK
    for j in range(n_chunks):
        cs = slice(j * FFN_CHUNK, (j + 1) * FFN_CHUNK)
        a = _dot(h, wi_ref[:, cs])
        b = _dot(h, wi_ref[:, f + j * FFN_CHUNK:f + (j + 1) * FFN_CHUNK])
        act = (a * _sigmoid(a) * b).astype(BF16)
        part = _dot(act, wo_ref[cs, :])
        if j == 0:
            acc_ref[...] = part
        else:
            acc_ref[...] += part
    y = acc_ref[...]
    o_ref[0] = x + 0.5 * gate * _rms(y, g_ref[gidx + 1:gidx + 2, :])


def _ffn_call(x, mods, mod_row, g, wi, wo, base, gidx):
    b, t, d = x.shape
    tm = min(FFN_ROWS, t)
    n_chunks = wo.shape[0] // FFN_CHUNK
    mrow = (lambda bi, ti: (bi, 0, 0)) if mod_row is None else (lambda bi, ti: (mod_row, 0, 0))
    return pl.pallas_call(
        functools.partial(_ffn_kernel, base=base, gidx=gidx, n_chunks=n_chunks),
        out_shape=jax.ShapeDtypeStruct(x.shape, F32),
        grid=(b, t // tm),
        in_specs=[pl.BlockSpec((1, tm, d), lambda bi, ti: (bi, ti, 0)),
                  pl.BlockSpec((1, N_MOD, d), mrow),
                  _const_spec(g.shape), _wspec(wi), _wspec(wo)],
        out_specs=pl.BlockSpec((1, tm, d), lambda bi, ti: (bi, ti, 0)),
        scratch_shapes=[pltpu.VMEM((tm, d), F32)],
        compiler_params=_cparams("parallel", "parallel"),
        name="ffn",
    )(x, mods, g, _warr(wi), _warr(wo))


INPROJ_COLS = (640, 512, 1024, 512, 1536)


def _inproj_kernel(x_ref, m_ref, g_ref, w_ref, o_mla, o_nq, o_nkv, o_u, o_hy, u_ref):
    x = x_ref[0]
    tm = x.shape[0]
    h = _modnorm(x, g_ref[2:3, :], m_ref[0, 3:4, :], m_ref[0, 4:5, :]).astype(BF16)
    off = 0
    for o_ref, n in zip((o_mla, o_nq, o_nkv, o_u, o_hy), INPROJ_COLS):
        z = _dot(h, w_ref[:, off:off + n])
        if o_ref is o_u:
            for k in range(n // 128):
                u_ref[k] = z[:, k * 128:(k + 1) * 128]
            for j in range(S5_CHUNK):
                for k in range(n // 128):
                    o_ref[j, 0, :, k * 128:(k + 1) * 128] = u_ref[k, pl.ds(j, tm // S5_CHUNK, stride=S5_CHUNK), :]
        else:
            o_ref[0] = z.astype(o_ref.dtype)
        off += n


def _inproj_call(x, mods, mod_row, g, w):
    b, t, d = x.shape
    tm = min(INPROJ_ROWS, t)
    mrow = (lambda bi, ti: (bi, 0, 0)) if mod_row is None else (lambda bi, ti: (mod_row, 0, 0))
    dts = (F32, BF16, BF16, F32, F32)
    out_shape = [jax.ShapeDtypeStruct((b, t, n), dt) for n, dt in zip(INPROJ_COLS, dts)]
    out_specs = [pl.BlockSpec((1, tm, n), lambda bi, ti: (bi, ti, 0)) for n in INPROJ_COLS]
    nu = INPROJ_COLS[3]
    out_shape[3] = jax.ShapeDtypeStruct((S5_CHUNK, b, t // S5_CHUNK, nu), F32)
    out_specs[3] = pl.BlockSpec((S5_CHUNK, 1, tm // S5_CHUNK, nu), lambda bi, ti: (0, bi, ti, 0))
    return pl.pallas_call(
        _inproj_kernel,
        out_shape=out_shape,
        grid=(b, t // tm),
        in_specs=[pl.BlockSpec((1, tm, d), lambda bi, ti: (bi, ti, 0)),
                  pl.BlockSpec((1, N_MOD, d), mrow),
                  _const_spec(g.shape), _wspec(w)],
        out_specs=out_specs,
        scratch_shapes=[pltpu.VMEM((nu // 128, tm, 128), F32)],
        compiler_params=_cparams("parallel", "parallel"),
        name="inproj",
    )(x, mods, g, _warr(w))


MLA_KW = 256
MLA_FILL_ROWS = 512


def _mla_kernel(*refs, n_ctx, n_lat, tq):
    if n_lat:
        zq_ref, zc_ref, zl_ref, tabq_ref, tabk_ref, gq_ref, gkv_ref, wq_ref, wkv_ref, o_ref, k_ref, v_ref = refs
    else:
        zq_ref, zc_ref, tabq_ref, gq_ref, gkv_ref, wq_ref, wkv_ref, o_ref, k_ref, v_ref = refs
    half = lax.broadcasted_iota(jnp.int32, (1, 128), 1) < MLA_ROPE

    def fill_kv(z, row0, tab):
        n = z.shape[0]
        kv = _dot(_rms(z[:, 0:MLA_RANK], gkv_ref[...]).astype(BF16), wkv_ref[...])
        r = z[:, 2 * MLA_RANK:2 * MLA_RANK + 128]
        if tab is None:
            rot = jnp.where(half, r, pltpu.roll(r, MLA_ROPE, 1))
        else:
            r = r * tab
            rot = r + pltpu.roll(r, MLA_ROPE, 1)
        rot = rot.astype(BF16)
        for h in range(MLA_HEADS):
            k_ref[h, pl.ds(row0, n), 0:MLA_NOPE] = kv[:, h * MLA_NOPE:(h + 1) * MLA_NOPE].astype(BF16)
            k_ref[h, pl.ds(row0, n), MLA_NOPE:MLA_KW] = rot
        v_ref[pl.ds(row0, n), :] = kv[:, MLA_HEADS * MLA_NOPE:].astype(BF16)

    @pl.when(pl.program_id(1) == 0)
    def _():
        fill_kv(zc_ref[0], 0, None)
        if n_lat:
            ck = min(MLA_FILL_ROWS, n_lat)
            for r0 in range(0, n_lat, ck):
                fill_kv(zl_ref[0, r0:r0 + ck, :], n_ctx + r0, tabk_ref[r0:r0 + ck, :])

    zq = zq_ref[0]
    q = _dot(_rms(zq[:, MLA_RANK:2 * MLA_RANK], gq_ref[...]).astype(BF16), wq_ref[...])
    tabq = tabq_ref[...]
    for h in range(MLA_HEADS):
        qh = (q[:, h * MLA_KW:(h + 1) * MLA_KW] * tabq).astype(BF16)
        s = _dot_nt(qh, k_ref[h])
        m = jnp.max(s, axis=-1, keepdims=True)
        p = jnp.exp2(s - m)
        l = jnp.sum(p, axis=-1, keepdims=True)
        o = _dot(p.astype(BF16), v_ref[:, h * MLA_V:(h + 1) * MLA_V])
        o_ref[0, :, h * MLA_V:(h + 1) * MLA_V] = (o / l).astype(o_ref.dtype)


def _mla_call(zq, zc, zl, tabq, tabk, gq, gkv, wq, wkv):
    b, t, w = zq.shape
    n_ctx = zc.shape[1]
    n_lat = 0 if zl is None else zl.shape[1]
    tq = min(512, t)
    n_keys = n_ctx + n_lat
    args = [zq, zc] + ([zl] if n_lat else []) + [tabq] + ([tabk] if n_lat else []) + [gq, gkv, wq, wkv]
    in_specs = [pl.BlockSpec((1, tq, w), lambda bi, qi: (bi, qi, 0)),
                pl.BlockSpec((1, n_ctx, w), lambda bi, qi: (bi, 0, 0))]
    if n_lat:
        in_specs.append(pl.BlockSpec((1, n_lat, w), lambda bi, qi: (bi, 0, 0)))
    in_specs.append(pl.BlockSpec((tq, MLA_KW), lambda bi, qi: (qi, 0)))
    if n_lat:
        in_specs.append(_const_spec(tabk.shape))
    in_specs += [_const_spec(gq.shape), _const_spec(gkv.shape), _const_spec(wq.shape), _const_spec(wkv.shape)]
    return pl.pallas_call(
        functools.partial(_mla_kernel, n_ctx=n_ctx, n_lat=n_lat, tq=tq),
        out_shape=jax.ShapeDtypeStruct((b, t, MLA_HEADS * MLA_V), BF16),
        grid=(b, t // tq),
        in_specs=in_specs,
        out_specs=pl.BlockSpec((1, tq, MLA_HEADS * MLA_V), lambda bi, qi: (bi, qi, 0)),
        scratch_shapes=[pltpu.VMEM((MLA_HEADS, n_keys, MLA_KW), BF16),
                        pltpu.VMEM((n_keys, MLA_HEADS * MLA_V), BF16)],
        compiler_params=_cparams("parallel", "arbitrary"),
        name="mla",
    )(*args)


NA_W = NA_HEADS * NA_DIM
NA_WIN = NA_WIN_R * GRID_W


def _na_kernel(*refs, local, rows):
    if local:
        q_ref, kvc_ref, kvl_ref, bias_ref, o_ref = refs
    else:
        q_ref, kvc_ref, o_ref = refs
    tq = q_ref.shape[1]
    lane_lo = lax.broadcasted_iota(jnp.int32, (1, 128), 1) < NA_DIM
    sub = GRID_W if local else tq
    wins, offs = [], []
    for rr in range(tq // sub if local else 0):
        r = pl.program_id(1) * (tq // sub) + rr
        start = jnp.clip(r - NA_WIN_R // 2, 0, rows - NA_WIN_R)
        offs.append(r - start)
        wins.append(pl.ds(pl.multiple_of(start * GRID_W, GRID_W), NA_WIN))
    scale = NA_DIM ** -0.5
    for j in range(NA_HEADS // 2):
        cols = slice(j * 128, (j + 1) * 128)
        vcols = slice(NA_W + j * 128, NA_W + (j + 1) * 128)
        k_c, v_c = kvc_ref[0, :, cols], kvc_ref[0, :, vcols]
        pieces = []
        for rr in range(tq // sub):
            qs = q_ref[0, rr * sub:(rr + 1) * sub, cols] * jnp.asarray(scale, BF16)
            pieces += [jnp.where(lane_lo, qs, jnp.zeros_like(qs)), jnp.where(lane_lo, jnp.zeros_like(qs), qs)]
        q_all = jnp.concatenate(pieces, axis=0)
        s_c = _dot_nt(q_all, k_c)
        m_c = jnp.max(s_c, axis=-1, keepdims=True)
        p_cs, p_ls, ls = [], [], []
        for rr in range(tq // sub):
            rs = slice(rr * 2 * sub, (rr + 1) * 2 * sub)
            m = m_c[rs]
            if local:
                bias = bias_ref[offs[rr], 2 * j:2 * j + 2].reshape(2 * sub, NA_WIN)
                s_l = _dot_nt(q_all[rs], kvl_ref[0, wins[rr], cols]) + bias
                m = jnp.maximum(m, jnp.max(s_l, axis=-1, keepdims=True))
            p_c = jnp.exp(s_c[rs] - m)
            l = jnp.sum(p_c, axis=-1, keepdims=True)
            if local:
                p_l = jnp.exp(s_l - m)
                l = l + jnp.sum(p_l, axis=-1, keepdims=True)
                p_ls.append(p_l.astype(BF16))
            p_cs.append(p_c.astype(BF16))
            ls.append(l)
        o_c = _dot(jnp.concatenate(p_cs, axis=0), v_c)
        for rr in range(tq // sub):
            rs = slice(rr * 2 * sub, (rr + 1) * 2 * sub)
            o = o_c[rs]
            if local:
                o = o + _dot(p_ls[rr], kvl_ref[0, wins[rr], vcols])
            o = o / ls[rr]
            o_ref[0, rr * sub:(rr + 1) * sub, cols] = jnp.where(lane_lo, o[:sub], o[sub:]).astype(o_ref.dtype)


NA_ROWS_PER_STEP = 4


def _nabias_kernel(r_ref, e_ref, m_ref, o_ref):
    o_ref[...] = jnp.dot(r_ref[...], e_ref[...], precision=lax.Precision.HIGHEST,
                         preferred_element_type=F32) + m_ref[...]


def _na_bias_table(rpb):
    h, na, nb = rpb.shape
    col = jnp.arange(GRID_W)
    c0 = jnp.clip(col - NA_WIN_C // 2, 0, GRID_W - NA_WIN_C)
    in_win = (col[None, :] >= c0[:, None]) & (col[None, :] < c0[:, None] + NA_WIN_C)
    dc = col[None, :] - col[:, None] + NA_WIN_C - 1
    onehot = (jnp.arange(128)[:, None, None] == dc[None]) & in_win[None]
    onehot = onehot.astype(F32).reshape(128, GRID_W * GRID_W)
    mask = jnp.where(in_win, 0.0, -jnp.inf).astype(F32).reshape(1, GRID_W * GRID_W)
    r = jnp.pad(rpb.reshape(h * na, nb), ((0, 128 - h * na), (0, 128 - nb)))
    t = pl.pallas_call(
        _nabias_kernel,
        out_shape=jax.ShapeDtypeStruct((128, GRID_W * GRID_W), F32),
        name="nabias",
    )(r, onehot, mask)
    t = t[:h * na].reshape(h, na, GRID_W, GRID_W)
    tabs = [t[:, NA_WIN_R - 1 - o:2 * NA_WIN_R - 1 - o].transpose(0, 2, 1, 3).reshape(h, GRID_W, NA_WIN)
            for o in range(NA_WIN_R)]
    return jnp.stack(tabs, axis=0)


def _na_call(q, kvc, kvl, bias):
    b, t, _ = q.shape
    n_ctx = kvc.shape[1]
    local = kvl is not None
    rows = t // GRID_W
    if local:
        assert rows >= NA_WIN_R and rows % NA_ROWS_PER_STEP == 0
    tq = GRID_W * NA_ROWS_PER_STEP if local else t
    args = [q, kvc] + ([kvl, bias] if local else [])
    in_specs = [pl.BlockSpec((1, tq, NA_W), lambda bi, ri: (bi, ri, 0)),
                pl.BlockSpec((1, n_ctx, 2 * NA_W), lambda bi, ri: (bi, 0, 0))]
    if local:
        in_specs += [pl.BlockSpec((1, t, 2 * NA_W), lambda bi, ri: (bi, 0, 0)), _const_spec(bias.shape)]
    return pl.pallas_call(
        functools.partial(_na_kernel, local=local, rows=rows),
        out_shape=jax.ShapeDtypeStruct((b, t, NA_W), BF16),
        grid=(b, t // tq),
        in_specs=in_specs,
        out_specs=pl.BlockSpec((1, tq, NA_W), lambda bi, ri: (bi, ri, 0)),
        compiler_params=_cparams("parallel", "parallel"),
        name="na",
    )(*args)


def _hyfilt_kernel(z_ref, w1_ref, b1_ref, f1_ref, w2_ref, b2_ref, f2_ref, w3_ref, dec_ref, cm_ref, sm_ref,
                   hre_ref, him_ref, hny_ref):
    hi = lax.Precision.HIGHEST
    t = z_ref.shape[0]
    n_fft = 2 * t
    h = jnp.sin(f1_ref[...] * (jnp.dot(z_ref[...], w1_ref[...], precision=hi, preferred_element_type=F32) + b1_ref[...]))
    h = jnp.sin(f2_ref[...] * (jnp.dot(h, w2_ref[...], precision=hi, preferred_element_type=F32) + b2_ref[...]))
    taps = jnp.dot(h, w3_ref[...], precision=hi, preferred_element_type=F32)
    dec = dec_ref[...]
    hf = taps[:, :HY_W] * dec
    hb = taps[:, HY_W:] * dec
    row = lax.broadcasted_iota(jnp.int32, (t, 1), 0)
    wgt = jnp.where(row == 0, 1.0 / n_fft, 2.0 / n_fft)
    hsum = hf + hb
    hre_ref[...] = _dot(cm_ref[...], hsum.astype(BF16)) * wgt
    him_ref[...] = _dot(sm_ref[...], (hb - hf).astype(BF16)) * wgt
    alt = (1 - 2 * (row & 1)).astype(F32)
    hny = jnp.sum(hsum * alt, axis=0, keepdims=True) * (1.0 / n_fft)
    hny_ref[...] = jnp.broadcast_to(hny, hny_ref.shape)


def _hyfilt_call(z, w1, b1, f1, w2, b2, f2, w3, dec, cm, sm):
    t = z.shape[0]
    args = (z, w1, b1, f1, w2, b2, f2, w3, dec, cm, sm)
    return pl.pallas_call(
        _hyfilt_kernel,
        out_shape=[jax.ShapeDtypeStruct((t, HY_W), F32), jax.ShapeDtypeStruct((t, HY_W), F32),
                   jax.ShapeDtypeStruct((8, HY_W), F32)],
        grid=(1,),
        in_specs=[_const_spec(a.shape) for a in args],
        out_specs=[pl.BlockSpec((t, HY_W), lambda i: (0, 0)), pl.BlockSpec((t, HY_W), lambda i: (0, 0)),
                   pl.BlockSpec((8, HY_W), lambda i: (0, 0))],
        compiler_params=_cparams("arbitrary"),
        name="hyfilt",
    )(*args)


HY_CB = 256
HY_FB = 512


def _hyena_kernel(zv_ref, z1_ref, z0_ref, wv_ref, w1_ref, w0_ref, bv_ref, b1_ref, b0_ref, bd_ref,
                  cm_ref, sm_ref, hre_ref, him_ref, hny_ref, o_ref, y_ref):
    t = zv_ref.shape[1]
    row = lax.broadcasted_iota(jnp.int32, (t, 1), 0)

    def conv3(z_ref, w_ref, b_ref):
        z = z_ref[0]
        prev = jnp.where(row == 0, 0.0, pltpu.roll(z, 1, 0))
        nxt = jnp.where(row == t - 1, 0.0, pltpu.roll(z, t - 1, 0))
        return w_ref[0:1, :] * prev + w_ref[1:2, :] * z + w_ref[2:3, :] * nxt + b_ref[...]

    s = conv3(zv_ref, wv_ref, bv_ref) * conv3(z1_ref, w1_ref, b1_ref)
    alt = (1 - 2 * (row & 1)).astype(F32)
    sb = s.astype(BF16)
    xny = jnp.sum(s * alt, axis=0, keepdims=True)
    y_ref[...] = alt * (xny * hny_ref[0:1, :]) + s * bd_ref[...]
    fb = min(HY_FB, t)
    for f0 in range(0, t, fb):
        fr = slice(f0, f0 + fb)
        a = _dot(cm_ref[fr, :], sb)
        bq = _dot(sm_ref[fr, :], sb)
        hre = hre_ref[fr, :]
        him = him_ref[fr, :]
        yc = (a * hre + bq * him).astype(BF16)
        ys = (bq * hre - a * him).astype(BF16)
        y_ref[...] += _dot(cm_ref[:, fr], yc) + _dot(sm_ref[:, fr], ys)
    o_ref[0] = (conv3(z0_ref, w0_ref, b0_ref) * y_ref[...]).astype(o_ref.dtype)


def _hyena_call(z, conv_w, conv_b, bias_d, cm, sm, hre, him, hny):
    b, t, _ = z.shape
    nb = HY_W // HY_CB
    zspec = lambda k: pl.BlockSpec((1, t, HY_CB), lambda bi, ci: (bi, 0, k * nb + ci))
    wspec = lambda k: pl.BlockSpec((3, HY_CB), lambda bi, ci: (0, k * nb + ci))
    bspec = lambda k: pl.BlockSpec((1, HY_CB), lambda bi, ci: (0, k * nb + ci))
    cspec = lambda rows: pl.BlockSpec((rows, HY_CB), lambda bi, ci: (0, ci))
    conv_b = conv_b.reshape(1, -1)
    return pl.pallas_call(
        _hyena_kernel,
        out_shape=jax.ShapeDtypeStruct((b, t, HY_W), BF16),
        grid=(b, nb),
        in_specs=[zspec(0), zspec(1), zspec(2), wspec(0), wspec(1), wspec(2), bspec(0), bspec(1), bspec(2),
                  cspec(1), _const_spec(cm.shape), _const_spec(sm.shape), cspec(t), cspec(t), cspec(8)],
        out_specs=pl.BlockSpec((1, t, HY_CB), lambda bi, ci: (bi, 0, ci)),
        scratch_shapes=[pltpu.VMEM((t, HY_CB), F32)],
        compiler_params=_cparams("parallel", "parallel"),
        name="hyena",
    )(z, z, z, conv_w, conv_w, conv_w, conv_b, conv_b, conv_b, bias_d.reshape(1, -1), cm, sm, hre, him, hny)


S5_CW = S5_CHUNK * S5_H


S5_GPS = 128 // S5_H
S5_NB = 4
S5_RB = 64


def _block_transpose(xs, blk):
    xs = list(xs)
    n = len(xs)
    d = n // 2
    while d:
        keep = (blk & d) == 0
        new = list(xs)
        for i in range(n):
            if i & d:
                continue
            a, b = xs[i], xs[i + d]
            new[i] = jnp.where(keep, a, pltpu.roll(b, d * S5_H, 1))
            new[i + d] = jnp.where(keep, pltpu.roll(a, 128 - d * S5_H, 1), b)
        xs = new
        d //= 2
    return xs


def _s5_kernel(xc_ref, xl_ref, m_ref, w_ref, v_ref, lam_ref, d_ref, yc_ref, yl_ref,
               u_ref, s_ref, xpf_ref, xpb_ref, y_ref, *, nb, n_cc, n_lc):
    parts = ((xc_ref, yc_ref, 0, n_cc), (xl_ref, yl_ref, nb * n_cc, n_lc))
    blk = lax.broadcasted_iota(jnp.int32, (1, 128), 1) // S5_H
    fwd = lax.broadcasted_iota(jnp.int32, (1, 128), 1) < S5_P

    def row_blocks(n_chunks):
        rb = min(S5_RB, n_chunks)
        return [(c0, rb) for c0 in range(0, n_chunks, rb)]

    def gather_rows(b, carry):
        for x_ref, _, base, n_chunks in parts:
            for c0, cn in row_blocks(n_chunks):
                xs = [x_ref[j, b, c0:c0 + cn, :] for j in range(S5_CHUNK)]
                rows_b = pl.ds(base + c0 * nb + b, cn, stride=nb)
                for h in range(2):
                    half = _block_transpose(xs[h * S5_GPS:(h + 1) * S5_GPS], blk)
                    for g in range(S5_GPS):
                        u_ref[g, h, rows_b, :] = half[g]
        return carry

    lax.fori_loop(0, nb, gather_rows, 0)

    def group_inputs(g):
        return jnp.concatenate([u_ref[g, 0], u_ref[g, 1]], axis=1).astype(BF16)

    for g in range(S5_GPS):
        s = _dot(group_inputs(g), w_ref[g])
        s_ref[g, 0] = s[:, 0:128]
        s_ref[g, 1] = s[:, 128:256]

    def scan_part(carry, base, n_chunks):
        def body(k, carry):
            kb = n_chunks - 1 - k
            rf = pl.ds(pl.multiple_of(base + k * nb, nb), nb)
            rb = pl.ds(pl.multiple_of(base + kb * nb, nb), nb)
            new = []
            for g in range(S5_GPS):
                xa, xb = carry[2 * g], carry[2 * g + 1]
                ar, ai = lam_ref[g, 0:1, :], lam_ref[g, 1:2, :]
                xpf_ref[g, 0, rf, :] = xa
                xpf_ref[g, 1, rf, :] = xb
                xpb_ref[g, 0, rb, :] = xa
                xpb_ref[g, 1, rb, :] = xb
                in_a = jnp.where(fwd, s_ref[g, 0, rf, :], s_ref[g, 0, rb, :])
                in_b = jnp.where(fwd, s_ref[g, 1, rf, :], s_ref[g, 1, rb, :])
                new += [ar * xa - ai * xb + in_a, ar * xb + ai * xa + in_b]
            return tuple(new)

        return lax.fori_loop(0, n_chunks, body, carry)

    carry = tuple(jnp.zeros((nb, 128), F32) for _ in range(2 * S5_GPS))
    for _, _, base, n_chunks in parts:
        carry = scan_part(carry, base, n_chunks)

    for g in range(S5_GPS):
        xp = jnp.concatenate([jnp.where(fwd, xpf_ref[g, h], xpb_ref[g, h]) for h in range(2)], axis=1).astype(BF16)
        y = _dot(group_inputs(g), m_ref[g]) + _dot(xp, v_ref[g])
        y_ref[g, 0] = y[:, 0:128]
        y_ref[g, 1] = y[:, 128:256]

    def scatter_rows(b, carry):
        for x_ref, o_ref, base, n_chunks in parts:
            for c0, cn in row_blocks(n_chunks):
                rows_b = pl.ds(base + c0 * nb + b, cn, stride=nb)
                halves = [_block_transpose([y_ref[g, h, rows_b, :] for g in range(S5_GPS)], blk) for h in range(2)]
                for j in range(S5_CHUNK):
                    half, jj = divmod(j, S5_GPS)
                    y = halves[half][jj] + d_ref[...] * x_ref[j, b, c0:c0 + cn, :]
                    cdf = 0.5 * (1.0 + jnp.tanh(math.sqrt(2.0 / math.pi) * (y + 0.044715 * (y * y * y))))
                    o_ref[j, b, c0:c0 + cn, :] = y * cdf
        return carry

    lax.fori_loop(0, nb, scatter_rows, 0)


def _s5_call(xc, xl, m, w, v, lam, d):
    _, b, n_cc, width = xc.shape
    n_lc = xl.shape[2]
    nb = min(S5_NB, b)
    rows = nb * (n_cc + n_lc)
    xspec = lambda n: pl.BlockSpec((S5_CHUNK, nb, n, 128), lambda mi, bi: (0, bi, 0, mi))
    gspec = lambda a: pl.BlockSpec((S5_GPS,) + a.shape[1:], lambda mi, bi: (mi, 0, 0))
    return pl.pallas_call(
        functools.partial(_s5_kernel, nb=nb, n_cc=n_cc, n_lc=n_lc),
        out_shape=[jax.ShapeDtypeStruct(xc.shape, F32), jax.ShapeDtypeStruct(xl.shape, F32)],
        grid=(width // 128, b // nb),
        in_specs=[xspec(n_cc), xspec(n_lc), gspec(m), gspec(w), gspec(v), gspec(lam),
                  pl.BlockSpec((1, 128), lambda mi, bi: (0, mi))],
        out_specs=[xspec(n_cc), xspec(n_lc)],
        scratch_shapes=[pltpu.VMEM((S5_GPS, 2, rows, 128), F32) for _ in range(5)],
        compiler_params=_cparams("parallel", "parallel"),
        name="s5",
    )(xc, xl, m, w, v, lam, d)


def _merge_kernel(x_ref, m_ref, g_ref, a_ref, b_ref, c_ref, d_ref, wg_ref, wb_ref, wglu_ref, bglu_ref, wo_ref,
                  o_ref, acc_ref, cy_ref):
    x = x_ref[0]
    tm, d = x.shape
    h = _modnorm(x, g_ref[2:3, :], m_ref[0, 3:4, :], m_ref[0, 4:5, :]).astype(BF16)
    for j in range(S5_CHUNK):
        for k in range(cy_ref.shape[0]):
            cy_ref[k, pl.ds(j, tm // S5_CHUNK, stride=S5_CHUNK), :] = c_ref[j, 0, :, k * 128:(k + 1) * 128]
    cy = jnp.concatenate([cy_ref[k] for k in range(cy_ref.shape[0])], axis=1)
    glu = _dot(cy.astype(BF16), wglu_ref[...]) + bglu_ref[...]
    nw = glu.shape[-1] // 2
    c = (glu[:, :nw] * _sigmoid(glu[:, nw:])).astype(BF16)
    for n, br in enumerate((a_ref[0], b_ref[0], c, d_ref[0])):
        gate = _sigmoid(_dot(h, wg_ref[:, n * d:(n + 1) * d]))
        part = gate * _dot(br, wb_ref[n])
        if n == 0:
            acc_ref[...] = part
        else:
            acc_ref[...] += part
    y = _dot(acc_ref[...].astype(BF16), wo_ref[...])
    o_ref[0] = x + m_ref[0, 5:6, :] * _rms(y, g_ref[3:4, :])


def _merge_call(x, mods, mod_row, g, a, b_, c, d_, wg, wb, wglu, bglu, wo):
    b, t, d = x.shape
    tm = min(MERGE_ROWS, t)
    mrow = (lambda bi, ti: (bi, 0, 0)) if mod_row is None else (lambda bi, ti: (mod_row, 0, 0))
    row = lambda n: pl.BlockSpec((1, tm, n), lambda bi, ti: (bi, ti, 0))
    bw = a.shape[-1]
    return pl.pallas_call(
        _merge_kernel,
        out_shape=jax.ShapeDtypeStruct(x.shape, F32),
        grid=(b, t // tm),
        in_specs=[row(d), pl.BlockSpec((1, N_MOD, d), mrow), _const_spec(g.shape),
                  row(bw), row(bw),
                  pl.BlockSpec((S5_CHUNK, 1, tm // S5_CHUNK, bw), lambda bi, ti: (0, bi, ti, 0)), row(bw),
                  _wspec(wg), _wspec(wb), _wspec(wglu), _const_spec(bglu.shape), _wspec(wo)],
        out_specs=row(d),
        scratch_shapes=[pltpu.VMEM((tm, d), F32), pltpu.VMEM((bw // 128, tm, 128), F32)],
        compiler_params=_cparams("parallel", "parallel"),
        name="merge",
    )(x, mods, g, a, b_, c, d_, _warr(wg), _warr(wb), _warr(wglu), bglu, _warr(wo))


def _rope_table(n_tok):
    q = MLA_ROPE // 4
    t = jnp.arange(n_tok)
    pos = jnp.stack([t // GRID_W, t % GRID_W], axis=-1).astype(F32)
    inv = ROPE_BASE ** (-jnp.arange(q, dtype=F32) / q)
    ang = pos[:, :, None] * inv
    cos, sin = jnp.cos(ang), jnp.sin(ang)
    cos_t = jnp.stack([cos, cos], axis=2).reshape(n_tok, MLA_ROPE)
    sin_t = jnp.stack([-sin, sin], axis=2).reshape(n_tok, MLA_ROPE)
    return jnp.concatenate([cos_t, sin_t], axis=-1)


def _rope_swap(w):
    q = MLA_ROPE // 4
    return w.reshape(w.shape[:-1] + (2, 2, q))[..., ::-1, :].reshape(w.shape)


def _dft_mats(t):
    n = 2 * t
    nb = 64 if t % 64 == 0 else 1
    tt = jnp.arange(t, dtype=jnp.int32)[None, :]
    ang = lambda f: ((f[:, None] * tt) % n).astype(F32) * (2.0 * math.pi / n)
    aa = ang(jnp.arange(t // nb, dtype=jnp.int32) * nb)[:, None, :]
    ab = ang(jnp.arange(nb, dtype=jnp.int32))[None, :, :]
    ca, sa, cb, sb = lax.optimization_barrier((jnp.cos(aa), jnp.sin(aa), jnp.cos(ab), jnp.sin(ab)))
    cm = (ca * cb - sa * sb).reshape(t, t)
    sm = (sa * cb + ca * sb).reshape(t, t)
    return cm.astype(BF16), sm.astype(BF16)


def _hyena_consts(n_tok):
    bands = (HY_EMB - 1) // 2
    t = jnp.arange(n_tok, dtype=F32)
    t01 = jnp.linspace(0.0, 1.0, n_tok, dtype=F32)[:, None]
    ang = (2.0 * math.pi * t / n_tok)[:, None] * jnp.linspace(1e-4, bands - 1, bands, dtype=F32)
    z = jnp.concatenate([t01, jnp.cos(ang), -jnp.sin(ang)], axis=-1)
    z = jnp.pad(z, ((0, 0), (0, 128 - HY_EMB)))
    max_decay = math.log(HY_TARGET) / HY_FAST_PCT
    min_decay = math.log(HY_TARGET) / HY_SLOW_PCT
    deltas = jnp.abs(jnp.linspace(min_decay, max_decay, HY_W, dtype=F32))
    return z, jnp.exp(-t01 * deltas)


def _s5_operators(lam_re, lam_im, log_dt, b_re, b_im, c_re, c_im):
    tc = S5_CHUNK
    lam = lax.complex(jnp.minimum(lam_re, -1e-4), lam_im)
    lam_dt = lam * jnp.exp(log_dt)[..., None]
    lam_bar = jnp.exp(lam_dt)
    b_bar = ((lam_bar - 1.0) / lam)[..., None] * lax.complex(b_re, b_im)
    cm = lax.complex(c_re, c_im)
    pw = jnp.exp(lam_dt[..., None] * jnp.arange(tc + 1, dtype=F32))
    g = lam_re.shape[1]
    kern = jnp.real(jnp.einsum('dgap,dgpk,dgph->dkgah', cm, pw[..., :tc], b_bar,
                               precision=lax.Precision.HIGHEST))
    s = jnp.arange(tc)[:, None, None]
    j = jnp.arange(tc)[None, :, None]
    lag = jnp.arange(tc)[None, None, :]
    place = jnp.stack([(j - s == lag), (s - j == lag)]).astype(F32)
    m = jnp.einsum('dsjk,dkgah->gshja', place, kern, precision=lax.Precision.HIGHEST)
    m = m.reshape(g, S5_CW, S5_CW)
    pf = pw[0][..., :tc][..., ::-1]
    pb = pw[1][..., :tc]
    wf = jnp.einsum('gps,gph->gshp', pf, b_bar[0]).reshape(g, S5_CW, S5_P)
    wb = jnp.einsum('gps,gph->gshp', pb, b_bar[1]).reshape(g, S5_CW, S5_P)
    w = jnp.concatenate([wf.real, wb.real, wf.imag, wb.imag], axis=-1)
    gf = jnp.einsum('gap,gpj->gpja', cm[0], pw[0][..., 1:tc + 1]).reshape(g, S5_P, S5_CW)
    gb = jnp.einsum('gap,gpj->gpja', cm[1], pw[1][..., 1:tc + 1][..., ::-1]).reshape(g, S5_P, S5_CW)
    v = jnp.concatenate([gf.real, gb.real, -gf.imag, -gb.imag], axis=1)
    a16 = pw[..., tc]
    lam16 = jnp.stack([jnp.concatenate([a16[0].real, a16[1].real], -1),
                       jnp.concatenate([a16[0].imag, a16[1].imag], -1)], axis=1)
    lam16 = jnp.pad(lam16, ((0, 0), (0, 6), (0, 0)))
    return m.astype(BF16), w.astype(BF16), v.astype(BF16), lam16


def _prep_layer(i, p):
    wi = [_Sel(p['ffn_wi_bf16'], (i, k)) for k in range(2)]
    wo = [_Sel(p['ffn_wo_bf16'], (i, k)) for k in range(2)]
    w_proj = _Sel(p['w_proj_bf16'], (i,))
    w_gate = _Sel(p['w_gate_bf16'], (i,))
    wuq = p['mla_w_uq'][i]
    rope = wuq[..., MLA_NOPE:]
    wq = jnp.concatenate([wuq[..., :MLA_NOPE], rope, _rope_swap(rope)], axis=-1)
    wq = wq.reshape(MLA_RANK, MLA_HEADS * MLA_KW).astype(BF16)
    wukv = p['mla_w_ukv'][i]
    wkv = jnp.concatenate([wukv[..., :MLA_NOPE].reshape(MLA_RANK, -1), wukv[..., MLA_NOPE:].reshape(MLA_RANK, -1)],
                          axis=1).astype(BF16)
    s5m, s5w, s5v, s5lam = (a[i] for a in p['s5_ops'])
    s5d = p['s5_d'][i].reshape(1, -1)
    return dict(
        g=p['norm_g'][i], wi=wi, wo=wo, w_proj=w_proj, w_gate=w_gate,
        gq=p['mla_g_q'][i].reshape(1, -1), gkv=p['mla_g_kv'][i].reshape(1, -1), wq=wq, wkv=wkv,
        bias=_na_bias_table(p['na_rpb'][i]),
        hy_mlp=(jnp.pad(p['hy_w1'][i], ((0, 128 - HY_EMB), (0, 0))), p['hy_b1'][i].reshape(1, -1),
                p['hy_freq1'][i].reshape(1, -1), p['hy_w2'][i], p['hy_b2'][i].reshape(1, -1),
                p['hy_freq2'][i].reshape(1, -1), p['hy_w3'][i]),
        conv_w=p['hy_conv_w'][i], conv_b=p['hy_conv_b'][i], hy_bias=p['hy_bias'][i],
        s5m=s5m, s5w=s5w, s5v=s5v, s5lam=s5lam, s5d=s5d,
        wglu=_Sel(p['wglu_bf16'], (i,)), bglu=p['s5_b_glu'][i].reshape(1, -1),
        wb=_Sel(p['wb_bf16'], (i,)), w_out=_Sel(p['w_out_bf16'], (i,)),
    )


def _stacked_bf16(p):
    w_in = p['w_in'].astype(BF16)
    seg = lambda k: w_in[:, :, sum(IN_SPLITS[:k]):sum(IN_SPLITS[:k + 1])]
    kr = seg(1)
    w_proj = jnp.concatenate([seg(0), seg(5), kr, _rope_swap(kr), seg(6), seg(2), seg(3), seg(4), seg(7)], axis=2)
    return dict(ffn_wi_bf16=p['ffn_w_in'].astype(BF16), ffn_wo_bf16=p['ffn_w_out'].astype(BF16),
                w_proj_bf16=w_proj, w_gate_bf16=seg(8),
                wglu_bf16=p['s5_w_glu'].astype(BF16), wb_bf16=p['w_branch'].astype(BF16),
                w_out_bf16=p['w_out'].astype(BF16))


def _mixer(xc, xl, mods, w, consts, ctx_out):
    b, n_lat, _ = xl.shape
    n_ctx = xc.shape[1]
    ctx_row = b
    mla_l, nq_l, nkv_l, u_l, hy_l = _inproj_call(xl, mods, None, w['g'], w['w_proj'])
    mla_c, nq_c, nkv_c, u_c, hy_c = _inproj_call(xc, mods, ctx_row, w['g'], w['w_proj'])

    scale = (MLA_NOPE + MLA_ROPE) ** -0.5 * math.log2(math.e)
    rope_l = consts['rope_l']
    tabq_l = jnp.concatenate([jnp.ones((n_lat, MLA_NOPE), F32), rope_l], axis=-1) * scale
    a_l = _mla_call(mla_l, mla_c, mla_l, tabq_l, rope_l, w['gq'], w['gkv'], w['wq'], w['wkv'])
    d_l = _na_call(nq_l, nkv_c, nkv_l, w['bias'])
    cy_c, cy_l = _s5_call(u_c, u_l, w['s5m'], w['s5w'], w['s5v'], w['s5lam'], w['s5d'])
    hre, him, hny = _hyfilt_call(consts['hy_z_l'], *w['hy_mlp'], consts['hy_dec_l'], *consts['dft_l'])
    b_l = _hyena_call(hy_l, w['conv_w'], w['conv_b'], w['hy_bias'], *consts['dft_l'], hre, him, hny)
    merge_w = (w['w_gate'], w['wb'], w['wglu'], w['bglu'], w['w_out'])
    xl = _merge_call(xl, mods, None, w['g'], a_l, b_l, cy_l, d_l, *merge_w)
    if not ctx_out:
        return None, xl

    tabq_c = jnp.concatenate([jnp.ones((n_ctx, MLA_NOPE + MLA_ROPE), F32), jnp.zeros((n_ctx, MLA_ROPE), F32)],
                             axis=-1) * scale
    a_c = _mla_call(mla_c, mla_c, None, tabq_c, None, w['gq'], w['gkv'], w['wq'], w['wkv'])
    d_c = _na_call(nq_c, nkv_c, None, None)
    hre, him, hny = _hyfilt_call(consts['hy_z_c'], *w['hy_mlp'], consts['hy_dec_c'], *consts['dft_c'])
    b_c = _hyena_call(hy_c, w['conv_w'], w['conv_b'], w['hy_bias'], *consts['dft_c'], hre, him, hny)
    xc = _merge_call(xc, mods, ctx_row, w['g'], a_c, b_c, cy_c, d_c, *merge_w)
    return xc, xl


def kernel(x, c, ctx, c_ctx, w_mod, b_mod, norm_g, ffn_w_in, ffn_w_out, w_in, mla_g_q, mla_g_kv, mla_w_uq, mla_w_ukv, na_rpb, hy_conv_w, hy_conv_b, hy_bias, hy_w1, hy_b1, hy_freq1, hy_w2, hy_b2, hy_freq2, hy_w3, s5_lam_re, s5_lam_im, s5_log_dt, s5_b_re, s5_b_im, s5_c_re, s5_c_im, s5_d, s5_w_glu, s5_b_glu, w_branch, w_out):
    p = dict(norm_g=norm_g, ffn_w_in=ffn_w_in, ffn_w_out=ffn_w_out, w_in=w_in, mla_g_q=mla_g_q, mla_g_kv=mla_g_kv,
             mla_w_uq=mla_w_uq, mla_w_ukv=mla_w_ukv, na_rpb=na_rpb, hy_conv_w=hy_conv_w, hy_conv_b=hy_conv_b,
             hy_bias=hy_bias, hy_w1=hy_w1, hy_b1=hy_b1, hy_freq1=hy_freq1, hy_w2=hy_w2, hy_b2=hy_b2,
             hy_freq2=hy_freq2, hy_w3=hy_w3, s5_lam_re=s5_lam_re, s5_lam_im=s5_lam_im, s5_log_dt=s5_log_dt,
             s5_b_re=s5_b_re, s5_b_im=s5_b_im, s5_c_re=s5_c_re, s5_c_im=s5_c_im, s5_d=s5_d, s5_w_glu=s5_w_glu,
             s5_b_glu=s5_b_glu, w_branch=w_branch, w_out=w_out)
    p.update(_stacked_bf16(p))
    p['s5_ops'] = jax.vmap(_s5_operators)(s5_lam_re, s5_lam_im, s5_log_dt, s5_b_re, s5_b_im, s5_c_re, s5_c_im)
    b, n_lat, d = x.shape
    n_ctx = ctx.shape[1]
    depth = w_mod.shape[0]
    assert b % 8 == 0 and n_lat % 256 == 0 and n_ctx % 256 == 0

    rows = -(-(b + 1) // 8) * 8
    acts = jnp.concatenate([c, c_ctx[None, :], jnp.zeros((rows - b - 1, d), F32)], axis=0)
    mods = _mod_call(acts, w_mod, b_mod).reshape(depth, rows, N_MOD, d)

    z_l, dec_l = _hyena_consts(n_lat)
    z_c, dec_c = _hyena_consts(n_ctx)
    consts = dict(rope_l=_rope_table(n_lat), dft_l=_dft_mats(n_lat), dft_c=_dft_mats(n_ctx),
                  hy_z_l=z_l, hy_dec_l=dec_l, hy_z_c=z_c, hy_dec_c=dec_c)

    xc, xl = ctx, x
    for i in range(depth):
        ctx_out = i < depth - 1
        w = _prep_layer(i, p)
        m = mods[i]
        xl = _ffn_call(xl, m, None, w['g'], w['wi'][0], w['wo'][0], 0, 0)
        xc = _ffn_call(xc, m, b, w['g'], w['wi'][0], w['wo'][0], 0, 0)
        xc, xl = _mixer(xc, xl, m, w, consts, ctx_out)
        xl = _ffn_call(xl, m, None, w['g'], w['wi'][1], w['wo'][1], 6, 4)
        if ctx_out:
            xc = _ffn_call(xc, m, b, w['g'], w['wi'][1], w['wo'][1], 6, 4)
    return xl
```

```python
import functools
import math

import jax
import jax.numpy as jnp
from jax import lax
from jax.experimental import pallas as pl
from jax.experimental.pallas import tpu as pltpu

F32 = jnp.float32
BF16 = jnp.bfloat16

EPS = 1e-6
GRID_W = 64
ROPE_BASE = 10000.0
N_MOD = 9
MLA_HEADS, MLA_NOPE, MLA_ROPE, MLA_V = 4, 128, 64, 128
MLA_RANK = 256
NA_HEADS, NA_DIM, NA_WIN_R, NA_WIN_C = 8, 64, 8, 16
HY_W, HY_EMB, HY_FILT = 512, 33, 64
HY_FAST_PCT, HY_SLOW_PCT, HY_TARGET = 0.3, 1.5, 1e-2
S5_H, S5_P = 16, 64
S5_CHUNK = 16
IN_SPLITS = (256, 64, 512, 512, 512, 256, 512, 1536, 4096)
FFN_CHUNK = 256
VMEM_LIMIT = 56 * 1024 * 1024
FFN_ROWS, INPROJ_ROWS, MERGE_ROWS = 1024, 1024, 512


def _cparams(*sem):
    return pltpu.CompilerParams(dimension_semantics=sem, vmem_limit_bytes=VMEM_LIMIT)


def _const_spec(shape):
    nd = len(shape)
    return pl.BlockSpec(shape, lambda *_: (0,) * nd, pipeline_mode=pl.Buffered(1))


class _Sel:
    def __init__(self, arr, idx):
        self.arr, self.idx, self.shape = arr, tuple(idx), arr.shape[len(idx):]


def _wspec(w):
    if not isinstance(w, _Sel):
        return _const_spec(w.shape)
    idx, nd = w.idx, len(w.shape)
    return pl.BlockSpec((None,) * len(idx) + w.shape, lambda *_: idx + (0,) * nd, pipeline_mode=pl.Buffered(1))


def _warr(w):
    return w.arr if isinstance(w, _Sel) else w


def _dot(a, b):
    return jnp.dot(a, b, preferred_element_type=F32)


def _dot_nt(a, b):
    return lax.dot_general(a, b, (((1,), (1,)), ((), ())), preferred_element_type=F32)


def _sigmoid(x):
    return 1.0 / (1.0 + jnp.exp(-x))


def _rms(x, g):
    return x * lax.rsqrt(jnp.mean(x * x, axis=-1, keepdims=True) + EPS) * g


def _modnorm(x, g, shift, scale):
    return _rms(x, g) * (1.0 + scale) + shift


def _mod_kernel(act_ref, w_ref, b_ref, o_ref):
    a = act_ref[...]
    a = a * _sigmoid(a)
    a_hi = a.astype(BF16)
    a_lo = (a - a_hi.astype(F32)).astype(BF16)
    w = w_ref[0]
    w_hi = w.astype(BF16)
    w_lo = (w - w_hi.astype(F32)).astype(BF16)
    o_ref[0] = _dot(a_hi, w_hi) + _dot(a_lo, w_hi) + _dot(a_hi, w_lo) + b_ref[0]


def _mod_call(acts, w_mod, b_mod):
    nl, d, nd = w_mod.shape
    rows = acts.shape[0]
    tn = 1152
    return pl.pallas_call(
        _mod_kernel,
        out_shape=jax.ShapeDtypeStruct((nl, rows, nd), F32),
        grid=(nl, nd // tn),
        in_specs=[pl.BlockSpec((rows, d), lambda l, j: (0, 0)),
                  pl.BlockSpec((1, d, tn), lambda l, j: (l, 0, j)),
                  pl.BlockSpec((1, 1, tn), lambda l, j: (l, 0, j))],
        out_specs=pl.BlockSpec((1, rows, tn), lambda l, j: (l, 0, j)),
        compiler_params=_cparams("parallel", "parallel"),
        name="mod",
    )(acts, w_mod, b_mod.reshape(nl, 1, nd))


def _ffn_kernel(x_ref, m_ref, g_ref, wi_ref, wo_ref, o_ref, acc_ref, *, base, gidx, n_chunks):
    x = x_ref[0]
    shift, scale, gate = (m_ref[0, base + k:base + k + 1, :] for k in range(3))
    h = _modnorm(x, g_ref[gidx:gidx + 1, :], shift, scale).astype(BF16)
    f = n_chunks * FFN_CHUNK
    for j in range(n_chunks):
        cs = slice(j * FFN_CHUNK, (j + 1) * FFN_CHUNK)
        a = _dot(h, wi_ref[:, cs])
        b = _dot(h, wi_ref[:, f + j * FFN_CHUNK:f + (j + 1) * FFN_CHUNK])
        act = (a * _sigmoid(a) * b).astype(BF16)
        part = _dot(act, wo_ref[cs, :])
        if j == 0:
            acc_ref[...] = part
        else:
            acc_ref[...] += part
    y = acc_ref[...]
    o_ref[0] = x + 0.5 * gate * _rms(y, g_ref[gidx + 1:gidx + 2, :])


def _ffn_call(x, mods, mod_row, g, wi, wo, base, gidx):
    b, t, d = x.shape
    tm = min(FFN_ROWS, t)
    n_chunks = wo.shape[0] // FFN_CHUNK
    mrow = (lambda bi, ti: (bi, 0, 0)) if mod_row is None else (lambda bi, ti: (mod_row, 0, 0))
    return pl.pallas_call(
        functools.partial(_ffn_kernel, base=base, gidx=gidx, n_chunks=n_chunks),
        out_shape=jax.ShapeDtypeStruct(x.shape, F32),
        grid=(b, t // tm),
        in_specs=[pl.BlockSpec((1, tm, d), lambda bi, ti: (bi, ti, 0)),
                  pl.BlockSpec((1, N_MOD, d), mrow),
                  _const_spec(g.shape), _wspec(wi), _wspec(wo)],
        out_specs=pl.BlockSpec((1, tm, d), lambda bi, ti: (bi, ti, 0)),
        scratch_shapes=[pltpu.VMEM((tm, d), F32)],
        compiler_params=_cparams("parallel", "parallel"),
        name="ffn",
    )(x, mods, g, _warr(wi), _warr(wo))


INPROJ_COLS = (640, 512, 1024, 512, 1536)


def _inproj_kernel(x_ref, m_ref, g_ref, w_ref, o_mla, o_nq, o_nkv, o_u, o_hy, u_ref):
    x = x_ref[0]
    tm = x.shape[0]
    h = _modnorm(x, g_ref[2:3, :], m_ref[0, 3:4, :], m_ref[0, 4:5, :]).astype(BF16)
    off = 0
    for o_ref, n in zip((o_mla, o_nq, o_nkv, o_u, o_hy), INPROJ_COLS):
        z = _dot(h, w_ref[:, off:off + n])
        if o_ref is o_u:
            for k in range(n // 128):
                u_ref[k] = z[:, k * 128:(k + 1) * 128]
            for j in range(S5_CHUNK):
                for k in range(n // 128):
                    o_ref[j, 0, :, k * 128:(k + 1) * 128] = u_ref[k, pl.ds(j, tm // S5_CHUNK, stride=S5_CHUNK), :]
        else:
            o_ref[0] = z.astype(o_ref.dtype)
        off += n


def _inproj_call(x, mods, mod_row, g, w):
    b, t, d = x.shape
    tm = min(INPROJ_ROWS, t)
    mrow = (lambda bi, ti: (bi, 0, 0)) if mod_row is None else (lambda bi, ti: (mod_row, 0, 0))
    dts = (F32, BF16, BF16, F32, F32)
    out_shape = [jax.ShapeDtypeStruct((b, t, n), dt) for n, dt in zip(INPROJ_COLS, dts)]
    out_specs = [pl.BlockSpec((1, tm, n), lambda bi, ti: (bi, ti, 0)) for n in INPROJ_COLS]
    nu = INPROJ_COLS[3]
    out_shape[3] = jax.ShapeDtypeStruct((S5_CHUNK, b, t // S5_CHUNK, nu), F32)
    out_specs[3] = pl.BlockSpec((S5_CHUNK, 1, tm // S5_CHUNK, nu), lambda bi, ti: (0, bi, ti, 0))
    return pl.pallas_call(
        _inproj_kernel,
        out_shape=out_shape,
        grid=(b, t // tm),
        in_specs=[pl.BlockSpec((1, tm, d), lambda bi, ti: (bi, ti, 0)),
                  pl.BlockSpec((1, N_MOD, d), mrow),
                  _const_spec(g.shape), _wspec(w)],
        out_specs=out_specs,
        scratch_shapes=[pltpu.VMEM((nu // 128, tm, 128), F32)],
        compiler_params=_cparams("parallel", "parallel"),
        name="inproj",
    )(x, mods, g, _warr(w))


MLA_KW = 256
MLA_FILL_ROWS = 512
MLA_QUERY_ROWS = 512


def _mla_kernel(*refs, n_ctx, n_lat, tq):
    if n_lat:
        zq_ref, zc_ref, zl_ref, tabq_ref, tabk_ref, gq_ref, gkv_ref, wq_ref, wkv_ref, o_ref, k_ref, v_ref = refs
    else:
        zq_ref, zc_ref, tabq_ref, gq_ref, gkv_ref, wq_ref, wkv_ref, o_ref, k_ref, v_ref = refs
    half = lax.broadcasted_iota(jnp.int32, (1, 128), 1) < MLA_ROPE

    def fill_kv(z, row0, tab):
        n = z.shape[0]
        kv = _dot(_rms(z[:, 0:MLA_RANK], gkv_ref[...]).astype(BF16), wkv_ref[...])
        r = z[:, 2 * MLA_RANK:2 * MLA_RANK + 128]
        if tab is None:
            rot = jnp.where(half, r, pltpu.roll(r, MLA_ROPE, 1))
        else:
            r = r * tab
            rot = r + pltpu.roll(r, MLA_ROPE, 1)
        rot = rot.astype(BF16)
        for h in range(MLA_HEADS):
            k_ref[h, pl.ds(row0, n), 0:MLA_NOPE] = kv[:, h * MLA_NOPE:(h + 1) * MLA_NOPE].astype(BF16)
            k_ref[h, pl.ds(row0, n), MLA_NOPE:MLA_KW] = rot
        v_ref[pl.ds(row0, n), :] = kv[:, MLA_HEADS * MLA_NOPE:].astype(BF16)

    @pl.when(pl.program_id(1) == 0)
    def _():
        fill_kv(zc_ref[0], 0, None)
        if n_lat:
            ck = min(MLA_FILL_ROWS, n_lat)
            for r0 in range(0, n_lat, ck):
                fill_kv(zl_ref[0, r0:r0 + ck, :], n_ctx + r0, tabk_ref[r0:r0 + ck, :])

    zq = zq_ref[0]
    q = _dot(_rms(zq[:, MLA_RANK:2 * MLA_RANK], gq_ref[...]).astype(BF16), wq_ref[...])
    tabq = tabq_ref[...]
    for h in range(MLA_HEADS):
        qh = (q[:, h * MLA_KW:(h + 1) * MLA_KW] * tabq).astype(BF16)
        s = _dot_nt(qh, k_ref[h])
        m = jnp.max(s, axis=-1, keepdims=True)
        p = jnp.exp2(s - m)
        l = jnp.sum(p, axis=-1, keepdims=True)
        o = _dot(p.astype(BF16), v_ref[:, h * MLA_V:(h + 1) * MLA_V])
        o_ref[0, :, h * MLA_V:(h + 1) * MLA_V] = (o / l).astype(o_ref.dtype)


def _mla_call(zq, zc, zl, tabq, tabk, gq, gkv, wq, wkv):
    b, t, w = zq.shape
    n_ctx = zc.shape[1]
    n_lat = 0 if zl is None else zl.shape[1]
    tq = min(MLA_QUERY_ROWS, t)
    n_keys = n_ctx + n_lat
    args = [zq, zc] + ([zl] if n_lat else []) + [tabq] + ([tabk] if n_lat else []) + [gq, gkv, wq, wkv]
    in_specs = [pl.BlockSpec((1, tq, w), lambda bi, qi: (bi, qi, 0)),
                pl.BlockSpec((1, n_ctx, w), lambda bi, qi: (bi, 0, 0))]
    if n_lat:
        in_specs.append(pl.BlockSpec((1, n_lat, w), lambda bi, qi: (bi, 0, 0)))
    in_specs.append(pl.BlockSpec((tq, MLA_KW), lambda bi, qi: (qi, 0)))
    if n_lat:
        in_specs.append(_const_spec(tabk.shape))
    in_specs += [_const_spec(gq.shape), _const_spec(gkv.shape), _const_spec(wq.shape), _const_spec(wkv.shape)]
    return pl.pallas_call(
        functools.partial(_mla_kernel, n_ctx=n_ctx, n_lat=n_lat, tq=tq),
        out_shape=jax.ShapeDtypeStruct((b, t, MLA_HEADS * MLA_V), BF16),
        grid=(b, t // tq),
        in_specs=in_specs,
        out_specs=pl.BlockSpec((1, tq, MLA_HEADS * MLA_V), lambda bi, qi: (bi, qi, 0)),
        scratch_shapes=[pltpu.VMEM((MLA_HEADS, n_keys, MLA_KW), BF16),
                        pltpu.VMEM((n_keys, MLA_HEADS * MLA_V), BF16)],
        compiler_params=_cparams("parallel", "arbitrary"),
        name="mla",
    )(*args)


NA_W = NA_HEADS * NA_DIM
NA_WIN = NA_WIN_R * GRID_W


def _na_kernel(*refs, local, rows):
    if local:
        q_ref, kvc_ref, kvl_ref, bias_ref, o_ref = refs
    else:
        q_ref, kvc_ref, o_ref = refs
    tq = q_ref.shape[1]
    lane_lo = lax.broadcasted_iota(jnp.int32, (1, 128), 1) < NA_DIM
    sub = GRID_W if local else tq
    wins, offs = [], []
    for rr in range(tq // sub if local else 0):
        r = pl.program_id(1) * (tq // sub) + rr
        start = jnp.clip(r - NA_WIN_R // 2, 0, rows - NA_WIN_R)
        offs.append(r - start)
        wins.append(pl.ds(pl.multiple_of(start * GRID_W, GRID_W), NA_WIN))
    scale = NA_DIM ** -0.5
    for j in range(NA_HEADS // 2):
        cols = slice(j * 128, (j + 1) * 128)
        vcols = slice(NA_W + j * 128, NA_W + (j + 1) * 128)
        k_c, v_c = kvc_ref[0, :, cols], kvc_ref[0, :, vcols]
        pieces = []
        for rr in range(tq // sub):
            qs = q_ref[0, rr * sub:(rr + 1) * sub, cols] * jnp.asarray(scale, BF16)
            pieces += [jnp.where(lane_lo, qs, jnp.zeros_like(qs)), jnp.where(lane_lo, jnp.zeros_like(qs), qs)]
        q_all = jnp.concatenate(pieces, axis=0)
        s_c = _dot_nt(q_all, k_c)
        m_c = jnp.max(s_c, axis=-1, keepdims=True)
        p_cs, p_ls, ls = [], [], []
        for rr in range(tq // sub):
            rs = slice(rr * 2 * sub, (rr + 1) * 2 * sub)
            m = m_c[rs]
            if local:
                bias = bias_ref[offs[rr], 2 * j:2 * j + 2].reshape(2 * sub, NA_WIN)
                s_l = _dot_nt(q_all[rs], kvl_ref[0, wins[rr], cols]) + bias
                m = jnp.maximum(m, jnp.max(s_l, axis=-1, keepdims=True))
            p_c = jnp.exp(s_c[rs] - m)
            l = jnp.sum(p_c, axis=-1, keepdims=True)
            if local:
                p_l = jnp.exp(s_l - m)
                l = l + jnp.sum(p_l, axis=-1, keepdims=True)
                p_ls.append(p_l.astype(BF16))
            p_cs.append(p_c.astype(BF16))
            ls.append(l)
        o_c = _dot(jnp.concatenate(p_cs, axis=0), v_c)
        for rr in range(tq // sub):
            rs = slice(rr * 2 * sub, (rr + 1) * 2 * sub)
            o = o_c[rs]
            if local:
                o = o + _dot(p_ls[rr], kvl_ref[0, wins[rr], vcols])
            o = o / ls[rr]
            o_ref[0, rr * sub:(rr + 1) * sub, cols] = jnp.where(lane_lo, o[:sub], o[sub:]).astype(o_ref.dtype)


NA_ROWS_PER_STEP = 8


def _nabias_kernel(r_ref, m_ref, o_ref, *, heads, n_dr):
    lane_lo = lax.broadcasted_iota(jnp.int32, (1, 128), 1) < GRID_W
    mask = m_ref[...]
    shift = (128 - (NA_WIN_C - 1), GRID_W - (NA_WIN_C - 1))

    def body(off, carry):
        for h in range(heads):
            for kp in range(NA_WIN_R // 2):
                halves = []
                for e in range(2):
                    a = 2 * kp + e - off + NA_WIN_R - 1
                    row = jnp.broadcast_to(r_ref[pl.ds(h * n_dr + a, 1), :], (GRID_W, 128))
                    halves.append(pltpu.roll(row, shift[e], 1, stride=1, stride_axis=0))
                o_ref[off, h, :, kp * 128:(kp + 1) * 128] = jnp.where(lane_lo, halves[0], halves[1]) + mask
        return carry

    lax.fori_loop(0, NA_WIN_R, body, 0)


def _na_bias_table(rpb):
    h, na, nb = rpb.shape
    col = jnp.arange(GRID_W)
    c0 = jnp.clip(col - NA_WIN_C // 2, 0, GRID_W - NA_WIN_C)
    in_win = (col[None, :] >= c0[:, None]) & (col[None, :] < c0[:, None] + NA_WIN_C)
    mask = jnp.tile(jnp.where(in_win, 0.0, -jnp.inf).astype(F32), (1, 2))
    r = jnp.pad(rpb.reshape(h * na, nb), ((0, 128 - h * na), (0, 128 - nb)))
    return pl.pallas_call(
        functools.partial(_nabias_kernel, heads=h, n_dr=na),
        out_shape=jax.ShapeDtypeStruct((NA_WIN_R, h, GRID_W, NA_WIN), F32),
        name="nabias",
    )(r, mask)


def _na_call(q, kvc, kvl, bias):
    b, t, _ = q.shape
    n_ctx = kvc.shape[1]
    local = kvl is not None
    rows = t // GRID_W
    if local:
        assert rows >= NA_WIN_R and rows % NA_ROWS_PER_STEP == 0
    tq = GRID_W * NA_ROWS_PER_STEP if local else t
    args = [q, kvc] + ([kvl, bias] if local else [])
    in_specs = [pl.BlockSpec((1, tq, NA_W), lambda bi, ri: (bi, ri, 0)),
                pl.BlockSpec((1, n_ctx, 2 * NA_W), lambda bi, ri: (bi, 0, 0))]
    if local:
        in_specs += [pl.BlockSpec((1, t, 2 * NA_W), lambda bi, ri: (bi, 0, 0)), _const_spec(bias.shape)]
    return pl.pallas_call(
        functools.partial(_na_kernel, local=local, rows=rows),
        out_shape=jax.ShapeDtypeStruct((b, t, NA_W), BF16),
        grid=(b, t // tq),
        in_specs=in_specs,
        out_specs=pl.BlockSpec((1, tq, NA_W), lambda bi, ri: (bi, ri, 0)),
        compiler_params=_cparams("parallel", "parallel"),
        name="na",
    )(*args)


def _hyfilt_kernel(z_ref, w1_ref, b1_ref, f1_ref, w2_ref, b2_ref, f2_ref, w3_ref, dec_ref, cm_ref, sm_ref,
                   hre_ref, him_ref, hny_ref):
    hi = lax.Precision.HIGHEST
    t = z_ref.shape[0]
    n_fft = 2 * t
    h = jnp.sin(f1_ref[...] * (jnp.dot(z_ref[...], w1_ref[...], precision=hi, preferred_element_type=F32) + b1_ref[...]))
    h = jnp.sin(f2_ref[...] * (jnp.dot(h, w2_ref[...], precision=hi, preferred_element_type=F32) + b2_ref[...]))
    taps = jnp.dot(h, w3_ref[...], precision=hi, preferred_element_type=F32)
    dec = dec_ref[...]
    hf = taps[:, :HY_W] * dec
    hb = taps[:, HY_W:] * dec
    row = lax.broadcasted_iota(jnp.int32, (t, 1), 0)
    wgt = jnp.where(row == 0, 1.0 / n_fft, 2.0 / n_fft)
    hsum = hf + hb
    hre_ref[...] = _dot(cm_ref[...], hsum.astype(BF16)) * wgt
    him_ref[...] = _dot(sm_ref[...], (hb - hf).astype(BF16)) * wgt
    alt = (1 - 2 * (row & 1)).astype(F32)
    hny = jnp.sum(hsum * alt, axis=0, keepdims=True) * (1.0 / n_fft)
    hny_ref[...] = jnp.broadcast_to(hny, hny_ref.shape)


def _hyfilt_call(z, w1, b1, f1, w2, b2, f2, w3, dec, cm, sm):
    t = z.shape[0]
    args = (z, w1, b1, f1, w2, b2, f2, w3, dec, cm, sm)
    return pl.pallas_call(
        _hyfilt_kernel,
        out_shape=[jax.ShapeDtypeStruct((t, HY_W), F32), jax.ShapeDtypeStruct((t, HY_W), F32),
                   jax.ShapeDtypeStruct((8, HY_W), F32)],
        grid=(1,),
        in_specs=[_const_spec(a.shape) for a in args],
        out_specs=[pl.BlockSpec((t, HY_W), lambda i: (0, 0)), pl.BlockSpec((t, HY_W), lambda i: (0, 0)),
                   pl.BlockSpec((8, HY_W), lambda i: (0, 0))],
        compiler_params=_cparams("arbitrary"),
        name="hyfilt",
    )(*args)


HY_CB = 256
HY_FB = 512


def _hyena_kernel(zv_ref, z1_ref, z0_ref, wv_ref, w1_ref, w0_ref, bv_ref, b1_ref, b0_ref, bd_ref,
                  cm_ref, sm_ref, hre_ref, him_ref, hny_ref, o_ref, y_ref):
    t = zv_ref.shape[1]
    row = lax.broadcasted_iota(jnp.int32, (t, 1), 0)

    def conv3(z_ref, w_ref, b_ref):
        z = z_ref[0]
        prev = jnp.where(row == 0, 0.0, pltpu.roll(z, 1, 0))
        nxt = jnp.where(row == t - 1, 0.0, pltpu.roll(z, t - 1, 0))
        return w_ref[0:1, :] * prev + w_ref[1:2, :] * z + w_ref[2:3, :] * nxt + b_ref[...]

    s = conv3(zv_ref, wv_ref, bv_ref) * conv3(z1_ref, w1_ref, b1_ref)
    alt = (1 - 2 * (row & 1)).astype(F32)
    sb = s.astype(BF16)
    xny = jnp.sum(s * alt, axis=0, keepdims=True)
    y_ref[...] = alt * (xny * hny_ref[0:1, :]) + s * bd_ref[...]
    fb = min(HY_FB, t)
    for f0 in range(0, t, fb):
        fr = slice(f0, f0 + fb)
        a = _dot(cm_ref[fr, :], sb)
        bq = _dot(sm_ref[fr, :], sb)
        hre = hre_ref[fr, :]
        him = him_ref[fr, :]
        yc = (a * hre + bq * him).astype(BF16)
        ys = (bq * hre - a * him).astype(BF16)
        y_ref[...] += _dot(cm_ref[:, fr], yc) + _dot(sm_ref[:, fr], ys)
    o_ref[0] = (conv3(z0_ref, w0_ref, b0_ref) * y_ref[...]).astype(o_ref.dtype)


def _hyena_call(z, conv_w, conv_b, bias_d, cm, sm, hre, him, hny):
    b, t, _ = z.shape
    nb = HY_W // HY_CB
    zspec = lambda k: pl.BlockSpec((1, t, HY_CB), lambda bi, ci: (bi, 0, k * nb + ci))
    wspec = lambda k: pl.BlockSpec((3, HY_CB), lambda bi, ci: (0, k * nb + ci))
    bspec = lambda k: pl.BlockSpec((1, HY_CB), lambda bi, ci: (0, k * nb + ci))
    cspec = lambda rows: pl.BlockSpec((rows, HY_CB), lambda bi, ci: (0, ci))
    conv_b = conv_b.reshape(1, -1)
    return pl.pallas_call(
        _hyena_kernel,
        out_shape=jax.ShapeDtypeStruct((b, t, HY_W), BF16),
        grid=(b, nb),
        in_specs=[zspec(0), zspec(1), zspec(2), wspec(0), wspec(1), wspec(2), bspec(0), bspec(1), bspec(2),
                  cspec(1), _const_spec(cm.shape), _const_spec(sm.shape), cspec(t), cspec(t), cspec(8)],
        out_specs=pl.BlockSpec((1, t, HY_CB), lambda bi, ci: (bi, 0, ci)),
        scratch_shapes=[pltpu.VMEM((t, HY_CB), F32)],
        compiler_params=_cparams("parallel", "parallel"),
        name="hyena",
    )(z, z, z, conv_w, conv_w, conv_w, conv_b, conv_b, conv_b, bias_d.reshape(1, -1), cm, sm, hre, him, hny)


S5_CW = S5_CHUNK * S5_H


S5_GPS = 128 // S5_H
S5_NB = 4
S5_RB = 64


def _block_transpose(xs, blk):
    xs = list(xs)
    n = len(xs)
    d = n // 2
    while d:
        keep = (blk & d) == 0
        new = list(xs)
        for i in range(n):
            if i & d:
                continue
            a, b = xs[i], xs[i + d]
            new[i] = jnp.where(keep, a, pltpu.roll(b, d * S5_H, 1))
            new[i + d] = jnp.where(keep, pltpu.roll(a, 128 - d * S5_H, 1), b)
        xs = new
        d //= 2
    return xs


def _s5_kernel(xc_ref, xl_ref, m_ref, w_ref, v_ref, lam_ref, d_ref, yc_ref, yl_ref,
               u_ref, s_ref, xpf_ref, xpb_ref, y_ref, *, nb, n_cc, n_lc):
    parts = ((xc_ref, yc_ref, 0, n_cc), (xl_ref, yl_ref, nb * n_cc, n_lc))
    blk = lax.broadcasted_iota(jnp.int32, (1, 128), 1) // S5_H
    fwd = lax.broadcasted_iota(jnp.int32, (1, 128), 1) < S5_P

    def row_blocks(n_chunks):
        rb = min(S5_RB, n_chunks)
        return [(c0, rb) for c0 in range(0, n_chunks, rb)]

    def gather_rows(b, carry):
        for x_ref, _, base, n_chunks in parts:
            for c0, cn in row_blocks(n_chunks):
                xs = [x_ref[j, b, c0:c0 + cn, :] for j in range(S5_CHUNK)]
                rows_b = pl.ds(base + c0 * nb + b, cn, stride=nb)
                for h in range(2):
                    half = _block_transpose(xs[h * S5_GPS:(h + 1) * S5_GPS], blk)
                    for g in range(S5_GPS):
                        u_ref[g, h, rows_b, :] = half[g]
        return carry

    lax.fori_loop(0, nb, gather_rows, 0)

    def group_inputs(g):
        return jnp.concatenate([u_ref[g, 0], u_ref[g, 1]], axis=1).astype(BF16)

    for g in range(S5_GPS):
        s = _dot(group_inputs(g), w_ref[g])
        s_ref[g, 0] = s[:, 0:128]
        s_ref[g, 1] = s[:, 128:256]

    def scan_part(carry, base, n_chunks):
        def body(k, carry):
            kb = n_chunks - 1 - k
            rf = pl.ds(pl.multiple_of(base + k * nb, nb), nb)
            rb = pl.ds(pl.multiple_of(base + kb * nb, nb), nb)
            new = []
            for g in range(S5_GPS):
                xa, xb = carry[2 * g], carry[2 * g + 1]
                ar, ai = lam_ref[g, 0:1, :], lam_ref[g, 1:2, :]
                xpf_ref[g, 0, rf, :] = xa
                xpf_ref[g, 1, rf, :] = xb
                xpb_ref[g, 0, rb, :] = xa
                xpb_ref[g, 1, rb, :] = xb
                in_a = jnp.where(fwd, s_ref[g, 0, rf, :], s_ref[g, 0, rb, :])
                in_b = jnp.where(fwd, s_ref[g, 1, rf, :], s_ref[g, 1, rb, :])
                new += [ar * xa - ai * xb + in_a, ar * xb + ai * xa + in_b]
            return tuple(new)

        return lax.fori_loop(0, n_chunks, body, carry)

    carry = tuple(jnp.zeros((nb, 128), F32) for _ in range(2 * S5_GPS))
    for _, _, base, n_chunks in parts:
        carry = scan_part(carry, base, n_chunks)

    for g in range(S5_GPS):
        xp = jnp.concatenate([jnp.where(fwd, xpf_ref[g, h], xpb_ref[g, h]) for h in range(2)], axis=1).astype(BF16)
        y = _dot(group_inputs(g), m_ref[g]) + _dot(xp, v_ref[g])
        y_ref[g, 0] = y[:, 0:128]
        y_ref[g, 1] = y[:, 128:256]

    def scatter_rows(b, carry):
        for x_ref, o_ref, base, n_chunks in parts:
            for c0, cn in row_blocks(n_chunks):
                rows_b = pl.ds(base + c0 * nb + b, cn, stride=nb)
                halves = [_block_transpose([y_ref[g, h, rows_b, :] for g in range(S5_GPS)], blk) for h in range(2)]
                for j in range(S5_CHUNK):
                    half, jj = divmod(j, S5_GPS)
                    y = halves[half][jj] + d_ref[...] * x_ref[j, b, c0:c0 + cn, :]
                    cdf = 0.5 * (1.0 + jnp.tanh(math.sqrt(2.0 / math.pi) * (y + 0.044715 * (y * y * y))))
                    o_ref[j, b, c0:c0 + cn, :] = y * cdf
        return carry

    lax.fori_loop(0, nb, scatter_rows, 0)


def _s5_call(xc, xl, m, w, v, lam, d):
    _, b, n_cc, width = xc.shape
    n_lc = xl.shape[2]
    nb = min(S5_NB, b)
    rows = nb * (n_cc + n_lc)
    xspec = lambda n: pl.BlockSpec((S5_CHUNK, nb, n, 128), lambda mi, bi: (0, bi, 0, mi))
    gspec = lambda a: pl.BlockSpec((S5_GPS,) + a.shape[1:], lambda mi, bi: (mi, 0, 0))
    return pl.pallas_call(
        functools.partial(_s5_kernel, nb=nb, n_cc=n_cc, n_lc=n_lc),
        out_shape=[jax.ShapeDtypeStruct(xc.shape, F32), jax.ShapeDtypeStruct(xl.shape, F32)],
        grid=(width // 128, b // nb),
        in_specs=[xspec(n_cc), xspec(n_lc), gspec(m), gspec(w), gspec(v), gspec(lam),
                  pl.BlockSpec((1, 128), lambda mi, bi: (0, mi))],
        out_specs=[xspec(n_cc), xspec(n_lc)],
        scratch_shapes=[pltpu.VMEM((S5_GPS, 2, rows, 128), F32) for _ in range(5)],
        compiler_params=_cparams("parallel", "parallel"),
        name="s5",
    )(xc, xl, m, w, v, lam, d)


def _merge_kernel(x_ref, m_ref, g_ref, a_ref, b_ref, c_ref, d_ref, wg_ref, wb_ref, wglu_ref, bglu_ref, wo_ref,
                  o_ref, acc_ref, cy_ref):
    x = x_ref[0]
    tm, d = x.shape
    h = _modnorm(x, g_ref[2:3, :], m_ref[0, 3:4, :], m_ref[0, 4:5, :]).astype(BF16)
    for j in range(S5_CHUNK):
        for k in range(cy_ref.shape[0]):
            cy_ref[k, pl.ds(j, tm // S5_CHUNK, stride=S5_CHUNK), :] = c_ref[j, 0, :, k * 128:(k + 1) * 128]
    cy = jnp.concatenate([cy_ref[k] for k in range(cy_ref.shape[0])], axis=1)
    glu = _dot(cy.astype(BF16), wglu_ref[...]) + bglu_ref[...]
    nw = glu.shape[-1] // 2
    c = (glu[:, :nw] * _sigmoid(glu[:, nw:])).astype(BF16)
    for n, br in enumerate((a_ref[0], b_ref[0], c, d_ref[0])):
        gate = _sigmoid(_dot(h, wg_ref[:, n * d:(n + 1) * d]))
        part = gate * _dot(br, wb_ref[n])
        if n == 0:
            acc_ref[...] = part
        else:
            acc_ref[...] += part
    y = _dot(acc_ref[...].astype(BF16), wo_ref[...])
    o_ref[0] = x + m_ref[0, 5:6, :] * _rms(y, g_ref[3:4, :])


def _merge_call(x, mods, mod_row, g, a, b_, c, d_, wg, wb, wglu, bglu, wo):
    b, t, d = x.shape
    tm = min(MERGE_ROWS, t)
    mrow = (lambda bi, ti: (bi, 0, 0)) if mod_row is None else (lambda bi, ti: (mod_row, 0, 0))
    row = lambda n: pl.BlockSpec((1, tm, n), lambda bi, ti: (bi, ti, 0))
    bw = a.shape[-1]
    return pl.pallas_call(
        _merge_kernel,
        out_shape=jax.ShapeDtypeStruct(x.shape, F32),
        grid=(b, t // tm),
        in_specs=[row(d), pl.BlockSpec((1, N_MOD, d), mrow), _const_spec(g.shape),
                  row(bw), row(bw),
                  pl.BlockSpec((S5_CHUNK, 1, tm // S5_CHUNK, bw), lambda bi, ti: (0, bi, ti, 0)), row(bw),
                  _wspec(wg), _wspec(wb), _wspec(wglu), _const_spec(bglu.shape), _wspec(wo)],
        out_specs=row(d),
        scratch_shapes=[pltpu.VMEM((tm, d), F32), pltpu.VMEM((bw // 128, tm, 128), F32)],
        compiler_params=_cparams("parallel", "parallel"),
        name="merge",
    )(x, mods, g, a, b_, c, d_, _warr(wg), _warr(wb), _warr(wglu), bglu, _warr(wo))


def _rope_table(n_tok):
    q = MLA_ROPE // 4
    t = jnp.arange(n_tok)
    pos = jnp.stack([t // GRID_W, t % GRID_W], axis=-1).astype(F32)
    inv = ROPE_BASE ** (-jnp.arange(q, dtype=F32) / q)
    ang = pos[:, :, None] * inv
    cos, sin = jnp.cos(ang), jnp.sin(ang)
    cos_t = jnp.stack([cos, cos], axis=2).reshape(n_tok, MLA_ROPE)
    sin_t = jnp.stack([-sin, sin], axis=2).reshape(n_tok, MLA_ROPE)
    return jnp.concatenate([cos_t, sin_t], axis=-1)


def _rope_swap(w):
    q = MLA_ROPE // 4
    return w.reshape(w.shape[:-1] + (2, 2, q))[..., ::-1, :].reshape(w.shape)


def _dft_mats(t):
    n = 2 * t
    nb = 64 if t % 64 == 0 else 1
    tt = jnp.arange(t, dtype=jnp.int32)[None, :]
    ang = lambda f: ((f[:, None] * tt) % n).astype(F32) * (2.0 * math.pi / n)
    aa = ang(jnp.arange(t // nb, dtype=jnp.int32) * nb)[:, None, :]
    ab = ang(jnp.arange(nb, dtype=jnp.int32))[None, :, :]
    ca, sa, cb, sb = lax.optimization_barrier((jnp.cos(aa), jnp.sin(aa), jnp.cos(ab), jnp.sin(ab)))
    cm = (ca * cb - sa * sb).reshape(t, t)
    sm = (sa * cb + ca * sb).reshape(t, t)
    return cm.astype(BF16), sm.astype(BF16)


def _hyena_consts(n_tok):
    bands = (HY_EMB - 1) // 2
    t = jnp.arange(n_tok, dtype=F32)
    t01 = jnp.linspace(0.0, 1.0, n_tok, dtype=F32)[:, None]
    ang = (2.0 * math.pi * t / n_tok)[:, None] * jnp.linspace(1e-4, bands - 1, bands, dtype=F32)
    z = jnp.concatenate([t01, jnp.cos(ang), -jnp.sin(ang)], axis=-1)
    z = jnp.pad(z, ((0, 0), (0, 128 - HY_EMB)))
    max_decay = math.log(HY_TARGET) / HY_FAST_PCT
    min_decay = math.log(HY_TARGET) / HY_SLOW_PCT
    deltas = jnp.abs(jnp.linspace(min_decay, max_decay, HY_W, dtype=F32))
    return z, jnp.exp(-t01 * deltas)


def _s5_operators(lam_re, lam_im, log_dt, b_re, b_im, c_re, c_im):
    tc = S5_CHUNK
    lam = lax.complex(jnp.minimum(lam_re, -1e-4), lam_im)
    lam_dt = lam * jnp.exp(log_dt)[..., None]
    lam_bar = jnp.exp(lam_dt)
    b_bar = ((lam_bar - 1.0) / lam)[..., None] * lax.complex(b_re, b_im)
    cm = lax.complex(c_re, c_im)
    pw = jnp.exp(lam_dt[..., None] * jnp.arange(tc + 1, dtype=F32))
    g = lam_re.shape[1]
    kern = jnp.real(jnp.einsum('dgap,dgpk,dgph->dkgah', cm, pw[..., :tc], b_bar,
                               precision=lax.Precision.HIGHEST))
    s = jnp.arange(tc)[:, None, None]
    j = jnp.arange(tc)[None, :, None]
    lag = jnp.arange(tc)[None, None, :]
    place = jnp.stack([(j - s == lag), (s - j == lag)]).astype(F32)
    m = jnp.einsum('dsjk,dkgah->gshja', place, kern, precision=lax.Precision.HIGHEST)
    m = m.reshape(g, S5_CW, S5_CW)
    pf = pw[0][..., :tc][..., ::-1]
    pb = pw[1][..., :tc]
    wf = jnp.einsum('gps,gph->gshp', pf, b_bar[0]).reshape(g, S5_CW, S5_P)
    wb = jnp.einsum('gps,gph->gshp', pb, b_bar[1]).reshape(g, S5_CW, S5_P)
    w = jnp.concatenate([wf.real, wb.real, wf.imag, wb.imag], axis=-1)
    gf = jnp.einsum('gap,gpj->gpja', cm[0], pw[0][..., 1:tc + 1]).reshape(g, S5_P, S5_CW)
    gb = jnp.einsum('gap,gpj->gpja', cm[1], pw[1][..., 1:tc + 1][..., ::-1]).reshape(g, S5_P, S5_CW)
    v = jnp.concatenate([gf.real, gb.real, -gf.imag, -gb.imag], axis=1)
    a16 = pw[..., tc]
    lam16 = jnp.stack([jnp.concatenate([a16[0].real, a16[1].real], -1),
                       jnp.concatenate([a16[0].imag, a16[1].imag], -1)], axis=1)
    lam16 = jnp.pad(lam16, ((0, 0), (0, 6), (0, 0)))
    return m.astype(BF16), w.astype(BF16), v.astype(BF16), lam16


def _prep_layer(i, p):
    wi = [_Sel(p['ffn_wi_bf16'], (i, k)) for k in range(2)]
    wo = [_Sel(p['ffn_wo_bf16'], (i, k)) for k in range(2)]
    w_proj = _Sel(p['w_proj_bf16'], (i,))
    w_gate = _Sel(p['w_gate_bf16'], (i,))
    wuq = p['mla_w_uq'][i]
    rope = wuq[..., MLA_NOPE:]
    wq = jnp.concatenate([wuq[..., :MLA_NOPE], rope, _rope_swap(rope)], axis=-1)
    wq = wq.reshape(MLA_RANK, MLA_HEADS * MLA_KW).astype(BF16)
    wukv = p['mla_w_ukv'][i]
    wkv = jnp.concatenate([wukv[..., :MLA_NOPE].reshape(MLA_RANK, -1), wukv[..., MLA_NOPE:].reshape(MLA_RANK, -1)],
                          axis=1).astype(BF16)
    s5m, s5w, s5v, s5lam = (a[i] for a in p['s5_ops'])
    s5d = p['s5_d'][i].reshape(1, -1)
    return dict(
        g=p['norm_g'][i], wi=wi, wo=wo, w_proj=w_proj, w_gate=w_gate,
        gq=p['mla_g_q'][i].reshape(1, -1), gkv=p['mla_g_kv'][i].reshape(1, -1), wq=wq, wkv=wkv,
        bias=_na_bias_table(p['na_rpb'][i]),
        hy_mlp=(jnp.pad(p['hy_w1'][i], ((0, 128 - HY_EMB), (0, 0))), p['hy_b1'][i].reshape(1, -1),
                p['hy_freq1'][i].reshape(1, -1), p['hy_w2'][i], p['hy_b2'][i].reshape(1, -1),
                p['hy_freq2'][i].reshape(1, -1), p['hy_w3'][i]),
        conv_w=p['hy_conv_w'][i], conv_b=p['hy_conv_b'][i], hy_bias=p['hy_bias'][i],
        s5m=s5m, s5w=s5w, s5v=s5v, s5lam=s5lam, s5d=s5d,
        wglu=_Sel(p['wglu_bf16'], (i,)), bglu=p['s5_b_glu'][i].reshape(1, -1),
        wb=_Sel(p['wb_bf16'], (i,)), w_out=_Sel(p['w_out_bf16'], (i,)),
    )


def _stacked_bf16(p):
    w_in = p['w_in'].astype(BF16)
    seg = lambda k: w_in[:, :, sum(IN_SPLITS[:k]):sum(IN_SPLITS[:k + 1])]
    kr = seg(1)
    w_proj = jnp.concatenate([seg(0), seg(5), kr, _rope_swap(kr), seg(6), seg(2), seg(3), seg(4), seg(7)], axis=2)
    return dict(ffn_wi_bf16=p['ffn_w_in'].astype(BF16), ffn_wo_bf16=p['ffn_w_out'].astype(BF16),
                w_proj_bf16=w_proj, w_gate_bf16=seg(8),
                wglu_bf16=p['s5_w_glu'].astype(BF16), wb_bf16=p['w_branch'].astype(BF16),
                w_out_bf16=p['w_out'].astype(BF16))


def _mixer(xc, xl, mods, w, consts, ctx_out):
    b, n_lat, _ = xl.shape
    n_ctx = xc.shape[1]
    ctx_row = b
    mla_l, nq_l, nkv_l, u_l, hy_l = _inproj_call(xl, mods, None, w['g'], w['w_proj'])
    mla_c, nq_c, nkv_c, u_c, hy_c = _inproj_call(xc, mods, ctx_row, w['g'], w['w_proj'])

    scale = (MLA_NOPE + MLA_ROPE) ** -0.5 * math.log2(math.e)
    rope_l = consts['rope_l']
    tabq_l = jnp.concatenate([jnp.ones((n_lat, MLA_NOPE), F32), rope_l], axis=-1) * scale
    a_l = _mla_call(mla_l, mla_c, mla_l, tabq_l, rope_l, w['gq'], w['gkv'], w['wq'], w['wkv'])
    d_l = _na_call(nq_l, nkv_c, nkv_l, w['bias'])
    cy_c, cy_l = _s5_call(u_c, u_l, w['s5m'], w['s5w'], w['s5v'], w['s5lam'], w['s5d'])
    hre, him, hny = _hyfilt_call(consts['hy_z_l'], *w['hy_mlp'], consts['hy_dec_l'], *consts['dft_l'])
    b_l = _hyena_call(hy_l, w['conv_w'], w['conv_b'], w['hy_bias'], *consts['dft_l'], hre, him, hny)
    merge_w = (w['w_gate'], w['wb'], w['wglu'], w['bglu'], w['w_out'])
    xl = _merge_call(xl, mods, None, w['g'], a_l, b_l, cy_l, d_l, *merge_w)
    if not ctx_out:
        return None, xl

    tabq_c = jnp.concatenate([jnp.ones((n_ctx, MLA_NOPE + MLA_ROPE), F32), jnp.zeros((n_ctx, MLA_ROPE), F32)],
                             axis=-1) * scale
    a_c = _mla_call(mla_c, mla_c, None, tabq_c, None, w['gq'], w['gkv'], w['wq'], w['wkv'])
    d_c = _na_call(nq_c, nkv_c, None, None)
    hre, him, hny = _hyfilt_call(consts['hy_z_c'], *w['hy_mlp'], consts['hy_dec_c'], *consts['dft_c'])
    b_c = _hyena_call(hy_c, w['conv_w'], w['conv_b'], w['hy_bias'], *consts['dft_c'], hre, him, hny)
    xc = _merge_call(xc, mods, ctx_row, w['g'], a_c, b_c, cy_c, d_c, *merge_w)
    return xc, xl


def kernel(x, c, ctx, c_ctx, w_mod, b_mod, norm_g, ffn_w_in, ffn_w_out, w_in, mla_g_q, mla_g_kv, mla_w_uq, mla_w_ukv, na_rpb, hy_conv_w, hy_conv_b, hy_bias, hy_w1, hy_b1, hy_freq1, hy_w2, hy_b2, hy_freq2, hy_w3, s5_lam_re, s5_lam_im, s5_log_dt, s5_b_re, s5_b_im, s5_c_re, s5_c_im, s5_d, s5_w_glu, s5_b_glu, w_branch, w_out):
    p = dict(norm_g=norm_g, ffn_w_in=ffn_w_in, ffn_w_out=ffn_w_out, w_in=w_in, mla_g_q=mla_g_q, mla_g_kv=mla_g_kv,
             mla_w_uq=mla_w_uq, mla_w_ukv=mla_w_ukv, na_rpb=na_rpb, hy_conv_w=hy_conv_w, hy_conv_b=hy_conv_b,
             hy_bias=hy_bias, hy_w1=hy_w1, hy_b1=hy_b1, hy_freq1=hy_freq1, hy_w2=hy_w2, hy_b2=hy_b2,
             hy_freq2=hy_freq2, hy_w3=hy_w3, s5_lam_re=s5_lam_re, s5_lam_im=s5_lam_im, s5_log_dt=s5_log_dt,
             s5_b_re=s5_b_re, s5_b_im=s5_b_im, s5_c_re=s5_c_re, s5_c_im=s5_c_im, s5_d=s5_d, s5_w_glu=s5_w_glu,
             s5_b_glu=s5_b_glu, w_branch=w_branch, w_out=w_out)
    p.update(_stacked_bf16(p))
    p['s5_ops'] = jax.vmap(_s5_operators)(s5_lam_re, s5_lam_im, s5_log_dt, s5_b_re, s5_b_im, s5_c_re, s5_c_im)
    b, n_lat, d = x.shape
    n_ctx = ctx.shape[1]
    depth = w_mod.shape[0]
    assert b % 8 == 0 and n_lat % 256 == 0 and n_ctx % 256 == 0

    rows = -(-(b + 1) // 8) * 8
    acts = jnp.concatenate([c, c_ctx[None, :], jnp.zeros((rows - b - 1, d), F32)], axis=0)
    mods = _mod_call(acts, w_mod, b_mod).reshape(depth, rows, N_MOD, d)

    z_l, dec_l = _hyena_consts(n_lat)
    z_c, dec_c = _hyena_consts(n_ctx)
    consts = dict(rope_l=_rope_table(n_lat), dft_l=_dft_mats(n_lat), dft_c=_dft_mats(n_ctx),
                  hy_z_l=z_l, hy_dec_l=dec_l, hy_z_c=z_c, hy_dec_c=dec_c)

    xc, xl = ctx, x
    for i in range(depth):
        ctx_out = i < depth - 1
        w = _prep_layer(i, p)
        m = mods[i]
        xl = _ffn_call(xl, m, None, w['g'], w['wi'][0], w['wo'][0], 0, 0)
        xc = _ffn_call(xc, m, b, w['g'], w['wi'][0], w['wo'][0], 0, 0)
        xc, xl = _mixer(xc, xl, m, w, consts, ctx_out)
        xl = _ffn_call(xl, m, None, w['g'], w['wi'][1], w['wo'][1], 6, 4)
        if ctx_out:
            xc = _ffn_call(xc, m, b, w['g'], w['wi'][1], w['wo'][1], 6, 4)
    return xl
```

```python
import functools
import math

import jax
import jax.numpy as jnp
from jax import lax
from jax.experimental import pallas as pl
from jax.experimental.pallas import tpu as pltpu

F32 = jnp.float32
BF16 = jnp.bfloat16

EPS = 1e-6
GRID_W = 64
ROPE_BASE = 10000.0
N_MOD = 9
MLA_HEADS, MLA_NOPE, MLA_ROPE, MLA_V = 4, 128, 64, 128
MLA_RANK = 256
NA_HEADS, NA_DIM, NA_WIN_R, NA_WIN_C = 8, 64, 8, 16
HY_W, HY_EMB, HY_FILT = 512, 33, 64
HY_FAST_PCT, HY_SLOW_PCT, HY_TARGET = 0.3, 1.5, 1e-2
S5_H, S5_P = 16, 64
S5_CHUNK = 16
IN_SPLITS = (256, 64, 512, 512, 512, 256, 512, 1536, 4096)
FFN_CHUNK = 256
VMEM_LIMIT = 56 * 1024 * 1024
FFN_ROWS, INPROJ_ROWS, MERGE_ROWS = 1024, 1024, 512


def _cparams(*sem):
    return pltpu.CompilerParams(dimension_semantics=sem, vmem_limit_bytes=VMEM_LIMIT)


def _const_spec(shape):
    nd = len(shape)
    return pl.BlockSpec(shape, lambda *_: (0,) * nd, pipeline_mode=pl.Buffered(1))


class _Sel:
    def __init__(self, arr, idx):
        self.arr, self.idx, self.shape = arr, tuple(idx), arr.shape[len(idx):]


def _wspec(w, cols=None):
    if not isinstance(w, _Sel):
        return _const_spec(w.shape)
    idx, nd = w.idx, len(w.shape)
    shape = w.shape if cols is None else w.shape[:-1] + (cols,)
    return pl.BlockSpec((None,) * len(idx) + shape, lambda *_: idx + (0,) * nd, pipeline_mode=pl.Buffered(1))


def _warr(w):
    return w.arr if isinstance(w, _Sel) else w


def _dot(a, b):
    return jnp.dot(a, b, preferred_element_type=F32)


def _dot_nt(a, b):
    return lax.dot_general(a, b, (((1,), (1,)), ((), ())), preferred_element_type=F32)


def _sigmoid(x):
    return 1.0 / (1.0 + jnp.exp(-x))


def _rms(x, g):
    return x * lax.rsqrt(jnp.mean(x * x, axis=-1, keepdims=True) + EPS) * g


def _modnorm(x, g, shift, scale):
    return _rms(x, g) * (1.0 + scale) + shift


def _mod_kernel(act_ref, w_ref, b_ref, o_ref):
    a = act_ref[...]
    a = a * _sigmoid(a)
    a_hi = a.astype(BF16)
    a_lo = (a - a_hi.astype(F32)).astype(BF16)
    w = w_ref[0]
    w_hi = w.astype(BF16)
    w_lo = (w - w_hi.astype(F32)).astype(BF16)
    o_ref[0] = _dot(a_hi, w_hi) + _dot(a_lo, w_hi) + _dot(a_hi, w_lo) + b_ref[0]


def _mod_call(acts, w_mod, b_mod):
    nl, d, nd = w_mod.shape
    rows = acts.shape[0]
    tn = 1152
    return pl.pallas_call(
        _mod_kernel,
        out_shape=jax.ShapeDtypeStruct((nl, rows, nd), F32),
        grid=(nl, nd // tn),
        in_specs=[pl.BlockSpec((rows, d), lambda l, j: (0, 0)),
                  pl.BlockSpec((1, d, tn), lambda l, j: (l, 0, j)),
                  pl.BlockSpec((1, 1, tn), lambda l, j: (l, 0, j))],
        out_specs=pl.BlockSpec((1, rows, tn), lambda l, j: (l, 0, j)),
        compiler_params=_cparams("parallel", "parallel"),
        name="mod",
    )(acts, w_mod, b_mod.reshape(nl, 1, nd))


def _ffn_kernel(x_ref, m_ref, g_ref, wi_ref, wo_ref, o_ref, acc_ref, *, base, gidx, n_chunks):
    x = x_ref[0]
    shift, scale, gate = (m_ref[0, base + k:base + k + 1, :] for k in range(3))
    h = _modnorm(x, g_ref[gidx:gidx + 1, :], shift, scale).astype(BF16)
    f = n_chunks * FFN_CHUNK
    for j in range(n_chunks):
        cs = slice(j * FFN_CHUNK, (j + 1) * FFN_CHUNK)
        a = _dot(h, wi_ref[:, cs])
        b = _dot(h, wi_ref[:, f + j * FFN_CHUNK:f + (j + 1) * FFN_CHUNK])
        act = (a * _sigmoid(a) * b).astype(BF16)
        part = _dot(act, wo_ref[cs, :])
        if j == 0:
            acc_ref[...] = part
        else:
            acc_ref[...] += part
    y = acc_ref[...]
    o_ref[0] = x + 0.5 * gate * _rms(y, g_ref[gidx + 1:gidx + 2, :])


def _ffn_call(x, mods, mod_row, g, wi, wo, base, gidx):
    b, t, d = x.shape
    tm = min(FFN_ROWS, t)
    n_chunks = wo.shape[0] // FFN_CHUNK
    mrow = (lambda bi, ti: (bi, 0, 0)) if mod_row is None else (lambda bi, ti: (mod_row, 0, 0))
    return pl.pallas_call(
        functools.partial(_ffn_kernel, base=base, gidx=gidx, n_chunks=n_chunks),
        out_shape=jax.ShapeDtypeStruct(x.shape, F32),
        grid=(b, t // tm),
        in_specs=[pl.BlockSpec((1, tm, d), lambda bi, ti: (bi, ti, 0)),
                  pl.BlockSpec((1, N_MOD, d), mrow),
                  _const_spec(g.shape), _wspec(wi), _wspec(wo)],
        out_specs=pl.BlockSpec((1, tm, d), lambda bi, ti: (bi, ti, 0)),
        scratch_shapes=[pltpu.VMEM((tm, d), F32)],
        compiler_params=_cparams("parallel", "parallel"),
        name="ffn",
    )(x, mods, g, _warr(wi), _warr(wo))


INPROJ_COLS = (640, 512, 1024, 512, 1536)
INPROJ_SRC = (((0, 256), (1920, 2176), (256, 384)), ((2176, 2688),), ((384, 1408),), ((1408, 1920),), ((2688, 4224),))
W_PROJ_COLS = 4224
W_ALL_COLS = W_PROJ_COLS + 4096


def _inproj_kernel(x_ref, m_ref, g_ref, w_ref, o_mla, o_nq, o_nkv, o_u, o_hy, u_ref):
    x = x_ref[0]
    tm = x.shape[0]
    h = _modnorm(x, g_ref[2:3, :], m_ref[0, 3:4, :], m_ref[0, 4:5, :]).astype(BF16)
    for o_ref, pieces in zip((o_mla, o_nq, o_nkv, o_u, o_hy), INPROJ_SRC):
        off = 0
        for lo, hi in pieces:
            n = hi - lo
            z = _dot(h, w_ref[:, lo:hi])
            if o_ref is o_u:
                for k in range(n // 128):
                    u_ref[k] = z[:, k * 128:(k + 1) * 128]
                for j in range(S5_CHUNK):
                    for k in range(n // 128):
                        o_ref[j, 0, :, k * 128:(k + 1) * 128] = u_ref[k, pl.ds(j, tm // S5_CHUNK, stride=S5_CHUNK), :]
            else:
                o_ref[0, :, off:off + n] = z.astype(o_ref.dtype)
            off += n


def _inproj_call(x, mods, mod_row, g, w):
    b, t, d = x.shape
    tm = min(INPROJ_ROWS, t)
    mrow = (lambda bi, ti: (bi, 0, 0)) if mod_row is None else (lambda bi, ti: (mod_row, 0, 0))
    dts = (F32, BF16, BF16, F32, F32)
    out_shape = [jax.ShapeDtypeStruct((b, t, n), dt) for n, dt in zip(INPROJ_COLS, dts)]
    out_specs = [pl.BlockSpec((1, tm, n), lambda bi, ti: (bi, ti, 0)) for n in INPROJ_COLS]
    nu = INPROJ_COLS[3]
    out_shape[3] = jax.ShapeDtypeStruct((S5_CHUNK, b, t // S5_CHUNK, nu), F32)
    out_specs[3] = pl.BlockSpec((S5_CHUNK, 1, tm // S5_CHUNK, nu), lambda bi, ti: (0, bi, ti, 0))
    return pl.pallas_call(
        _inproj_kernel,
        out_shape=out_shape,
        grid=(b, t // tm),
        in_specs=[pl.BlockSpec((1, tm, d), lambda bi, ti: (bi, ti, 0)),
                  pl.BlockSpec((1, N_MOD, d), mrow),
                  _const_spec(g.shape), _wspec(w, W_PROJ_COLS)],
        out_specs=out_specs,
        scratch_shapes=[pltpu.VMEM((nu // 128, tm, 128), F32)],
        compiler_params=_cparams("parallel", "parallel"),
        name="inproj",
    )(x, mods, g, _warr(w))


MLA_KW = 256
MLA_FILL_ROWS = 512
MLA_QUERY_ROWS = 512


def _mla_kernel(*refs, n_ctx, n_lat, tq):
    if n_lat:
        zq_ref, zc_ref, zl_ref, tabq_ref, tabk_ref, gq_ref, gkv_ref, wq_ref, wkv_ref, o_ref, k_ref, v_ref = refs
    else:
        zq_ref, zc_ref, tabq_ref, gq_ref, gkv_ref, wq_ref, wkv_ref, o_ref, k_ref, v_ref = refs
    half = lax.broadcasted_iota(jnp.int32, (1, 128), 1) < MLA_ROPE

    def fill_kv(z, row0, tab):
        n = z.shape[0]
        kv = _dot(_rms(z[:, 0:MLA_RANK], gkv_ref[...]).astype(BF16), wkv_ref[...])
        r = z[:, 2 * MLA_RANK:2 * MLA_RANK + 128]
        if tab is None:
            rot = jnp.where(half, r, pltpu.roll(r, MLA_ROPE, 1))
        else:
            r = r * tab
            rot = r + pltpu.roll(r, MLA_ROPE, 1)
        rot = rot.astype(BF16)
        for h in range(MLA_HEADS):
            k_ref[h, pl.ds(row0, n), 0:MLA_NOPE] = kv[:, h * MLA_NOPE:(h + 1) * MLA_NOPE].astype(BF16)
            k_ref[h, pl.ds(row0, n), MLA_NOPE:MLA_KW] = rot
        v_ref[pl.ds(row0, n), :] = kv[:, MLA_HEADS * MLA_NOPE:].astype(BF16)

    @pl.when(pl.program_id(1) == 0)
    def _():
        fill_kv(zc_ref[0], 0, None)
        if n_lat:
            ck = min(MLA_FILL_ROWS, n_lat)
            for r0 in range(0, n_lat, ck):
                fill_kv(zl_ref[0, r0:r0 + ck, :], n_ctx + r0, tabk_ref[r0:r0 + ck, :])

    zq = zq_ref[0]
    q = _dot(_rms(zq[:, MLA_RANK:2 * MLA_RANK], gq_ref[...]).astype(BF16), wq_ref[...])
    tabq = tabq_ref[...]
    for h in range(MLA_HEADS):
        qh = (q[:, h * MLA_KW:(h + 1) * MLA_KW] * tabq).astype(BF16)
        s = _dot_nt(qh, k_ref[h])
        m = jnp.max(s, axis=-1, keepdims=True)
        p = jnp.exp2(s - m)
        l = jnp.sum(p, axis=-1, keepdims=True)
        o = _dot(p.astype(BF16), v_ref[:, h * MLA_V:(h + 1) * MLA_V])
        o_ref[0, :, h * MLA_V:(h + 1) * MLA_V] = (o / l).astype(o_ref.dtype)


def _mla_call(zq, zc, zl, tabq, tabk, gq, gkv, wq, wkv):
    b, t, w = zq.shape
    n_ctx = zc.shape[1]
    n_lat = 0 if zl is None else zl.shape[1]
    tq = min(MLA_QUERY_ROWS, t)
    n_keys = n_ctx + n_lat
    args = [zq, zc] + ([zl] if n_lat else []) + [tabq] + ([tabk] if n_lat else []) + [gq, gkv, wq, wkv]
    in_specs = [pl.BlockSpec((1, tq, w), lambda bi, qi: (bi, qi, 0)),
                pl.BlockSpec((1, n_ctx, w), lambda bi, qi: (bi, 0, 0))]
    if n_lat:
        in_specs.append(pl.BlockSpec((1, n_lat, w), lambda bi, qi: (bi, 0, 0)))
    in_specs.append(pl.BlockSpec((tq, MLA_KW), lambda bi, qi: (qi, 0)))
    if n_lat:
        in_specs.append(_const_spec(tabk.shape))
    in_specs += [_const_spec(gq.shape), _const_spec(gkv.shape), _const_spec(wq.shape), _const_spec(wkv.shape)]
    return pl.pallas_call(
        functools.partial(_mla_kernel, n_ctx=n_ctx, n_lat=n_lat, tq=tq),
        out_shape=jax.ShapeDtypeStruct((b, t, MLA_HEADS * MLA_V), BF16),
        grid=(b, t // tq),
        in_specs=in_specs,
        out_specs=pl.BlockSpec((1, tq, MLA_HEADS * MLA_V), lambda bi, qi: (bi, qi, 0)),
        scratch_shapes=[pltpu.VMEM((MLA_HEADS, n_keys, MLA_KW), BF16),
                        pltpu.VMEM((n_keys, MLA_HEADS * MLA_V), BF16)],
        compiler_params=_cparams("parallel", "arbitrary"),
        name="mla",
    )(*args)


NA_W = NA_HEADS * NA_DIM
NA_WIN = NA_WIN_R * GRID_W


def _na_kernel(*refs, local, rows):
    if local:
        q_ref, kvc_ref, kvl_ref, bias_ref, o_ref = refs
    else:
        q_ref, kvc_ref, o_ref = refs
    tq = q_ref.shape[1]
    lane_lo = lax.broadcasted_iota(jnp.int32, (1, 128), 1) < NA_DIM
    sub = GRID_W if local else tq
    wins, offs = [], []
    for rr in range(tq // sub if local else 0):
        r = pl.program_id(1) * (tq // sub) + rr
        start = jnp.clip(r - NA_WIN_R // 2, 0, rows - NA_WIN_R)
        offs.append(r - start)
        wins.append(pl.ds(pl.multiple_of(start * GRID_W, GRID_W), NA_WIN))
    scale = NA_DIM ** -0.5
    for j in range(NA_HEADS // 2):
        cols = slice(j * 128, (j + 1) * 128)
        vcols = slice(NA_W + j * 128, NA_W + (j + 1) * 128)
        k_c, v_c = kvc_ref[0, :, cols], kvc_ref[0, :, vcols]
        pieces = []
        for rr in range(tq // sub):
            qs = q_ref[0, rr * sub:(rr + 1) * sub, cols] * jnp.asarray(scale, BF16)
            pieces += [jnp.where(lane_lo, qs, jnp.zeros_like(qs)), jnp.where(lane_lo, jnp.zeros_like(qs), qs)]
        q_all = jnp.concatenate(pieces, axis=0)
        s_c = _dot_nt(q_all, k_c)
        m_c = jnp.max(s_c, axis=-1, keepdims=True)
        p_cs, p_ls, ls = [], [], []
        for rr in range(tq // sub):
            rs = slice(rr * 2 * sub, (rr + 1) * 2 * sub)
            m = m_c[rs]
            if local:
                bias = bias_ref[offs[rr], 2 * j:2 * j + 2].reshape(2 * sub, NA_WIN)
                s_l = _dot_nt(q_all[rs], kvl_ref[0, wins[rr], cols]) + bias
                m = jnp.maximum(m, jnp.max(s_l, axis=-1, keepdims=True))
            p_c = jnp.exp(s_c[rs] - m)
            l = jnp.sum(p_c, axis=-1, keepdims=True)
            if local:
                p_l = jnp.exp(s_l - m)
                l = l + jnp.sum(p_l, axis=-1, keepdims=True)
                p_ls.append(p_l.astype(BF16))
            p_cs.append(p_c.astype(BF16))
            ls.append(l)
        o_c = _dot(jnp.concatenate(p_cs, axis=0), v_c)
        for rr in range(tq // sub):
            rs = slice(rr * 2 * sub, (rr + 1) * 2 * sub)
            o = o_c[rs]
            if local:
                o = o + _dot(p_ls[rr], kvl_ref[0, wins[rr], vcols])
            o = o / ls[rr]
            o_ref[0, rr * sub:(rr + 1) * sub, cols] = jnp.where(lane_lo, o[:sub], o[sub:]).astype(o_ref.dtype)


NA_ROWS_PER_STEP = 8


def _nabias_kernel(r_ref, m_ref, o_ref, *, heads, n_dr):
    lane_lo = lax.broadcasted_iota(jnp.int32, (1, 128), 1) < GRID_W
    mask = m_ref[...]
    shift = (128 - (NA_WIN_C - 1), GRID_W - (NA_WIN_C - 1))

    def body(off, carry):
        for h in range(heads):
            for kp in range(NA_WIN_R // 2):
                halves = []
                for e in range(2):
                    a = 2 * kp + e - off + NA_WIN_R - 1
                    row = jnp.broadcast_to(r_ref[pl.ds(h * n_dr + a, 1), :], (GRID_W, 128))
                    halves.append(pltpu.roll(row, shift[e], 1, stride=1, stride_axis=0))
                o_ref[off, h, :, kp * 128:(kp + 1) * 128] = jnp.where(lane_lo, halves[0], halves[1]) + mask
        return carry

    lax.fori_loop(0, NA_WIN_R, body, 0)


def _na_bias_table(rpb):
    h, na, nb = rpb.shape
    col = jnp.arange(GRID_W)
    c0 = jnp.clip(col - NA_WIN_C // 2, 0, GRID_W - NA_WIN_C)
    in_win = (col[None, :] >= c0[:, None]) & (col[None, :] < c0[:, None] + NA_WIN_C)
    mask = jnp.tile(jnp.where(in_win, 0.0, -jnp.inf).astype(F32), (1, 2))
    r = jnp.pad(rpb.reshape(h * na, nb), ((0, 128 - h * na), (0, 128 - nb)))
    return pl.pallas_call(
        functools.partial(_nabias_kernel, heads=h, n_dr=na),
        out_shape=jax.ShapeDtypeStruct((NA_WIN_R, h, GRID_W, NA_WIN), F32),
        name="nabias",
    )(r, mask)


def _na_call(q, kvc, kvl, bias):
    b, t, _ = q.shape
    n_ctx = kvc.shape[1]
    local = kvl is not None
    rows = t // GRID_W
    if local:
        assert rows >= NA_WIN_R and rows % NA_ROWS_PER_STEP == 0
    tq = GRID_W * NA_ROWS_PER_STEP if local else t
    args = [q, kvc] + ([kvl, bias] if local else [])
    in_specs = [pl.BlockSpec((1, tq, NA_W), lambda bi, ri: (bi, ri, 0)),
                pl.BlockSpec((1, n_ctx, 2 * NA_W), lambda bi, ri: (bi, 0, 0))]
    if local:
        in_specs += [pl.BlockSpec((1, t, 2 * NA_W), lambda bi, ri: (bi, 0, 0)), _const_spec(bias.shape)]
    return pl.pallas_call(
        functools.partial(_na_kernel, local=local, rows=rows),
        out_shape=jax.ShapeDtypeStruct((b, t, NA_W), BF16),
        grid=(b, t // tq),
        in_specs=in_specs,
        out_specs=pl.BlockSpec((1, tq, NA_W), lambda bi, ri: (bi, ri, 0)),
        compiler_params=_cparams("parallel", "parallel"),
        name="na",
    )(*args)


def _hyfilt_kernel(z_ref, w1_ref, b1_ref, f1_ref, w2_ref, b2_ref, f2_ref, w3_ref, dec_ref, cm_ref, sm_ref,
                   hre_ref, him_ref, hny_ref):
    hi = lax.Precision.HIGHEST
    t = z_ref.shape[0]
    n_fft = 2 * t
    h = jnp.sin(f1_ref[...] * (jnp.dot(z_ref[...], w1_ref[...], precision=hi, preferred_element_type=F32) + b1_ref[...]))
    h = jnp.sin(f2_ref[...] * (jnp.dot(h, w2_ref[...], precision=hi, preferred_element_type=F32) + b2_ref[...]))
    taps = jnp.dot(h, w3_ref[...], precision=hi, preferred_element_type=F32)
    dec = dec_ref[...]
    hf = taps[:, :HY_W] * dec
    hb = taps[:, HY_W:] * dec
    row = lax.broadcasted_iota(jnp.int32, (t, 1), 0)
    wgt = jnp.where(row == 0, 1.0 / n_fft, 2.0 / n_fft)
    hsum = hf + hb
    hre_ref[...] = _dot(cm_ref[...], hsum.astype(BF16)) * wgt
    him_ref[...] = _dot(sm_ref[...], (hb - hf).astype(BF16)) * wgt
    alt = (1 - 2 * (row & 1)).astype(F32)
    hny = jnp.sum(hsum * alt, axis=0, keepdims=True) * (1.0 / n_fft)
    hny_ref[...] = jnp.broadcast_to(hny, hny_ref.shape)


def _hyfilt_call(z, w1, b1, f1, w2, b2, f2, w3, dec, cm, sm):
    t = z.shape[0]
    args = (z, w1, b1, f1, w2, b2, f2, w3, dec, cm, sm)
    return pl.pallas_call(
        _hyfilt_kernel,
        out_shape=[jax.ShapeDtypeStruct((t, HY_W), F32), jax.ShapeDtypeStruct((t, HY_W), F32),
                   jax.ShapeDtypeStruct((8, HY_W), F32)],
        grid=(1,),
        in_specs=[_const_spec(a.shape) for a in args],
        out_specs=[pl.BlockSpec((t, HY_W), lambda i: (0, 0)), pl.BlockSpec((t, HY_W), lambda i: (0, 0)),
                   pl.BlockSpec((8, HY_W), lambda i: (0, 0))],
        compiler_params=_cparams("arbitrary"),
        name="hyfilt",
    )(*args)


HY_CB = 256
HY_FB = 512


def _hyena_kernel(zv_ref, z1_ref, z0_ref, wv_ref, w1_ref, w0_ref, bv_ref, b1_ref, b0_ref, bd_ref,
                  cm_ref, sm_ref, hre_ref, him_ref, hny_ref, o_ref, y_ref):
    t = zv_ref.shape[1]
    row = lax.broadcasted_iota(jnp.int32, (t, 1), 0)

    def conv3(z_ref, w_ref, b_ref):
        z = z_ref[0]
        prev = jnp.where(row == 0, 0.0, pltpu.roll(z, 1, 0))
        nxt = jnp.where(row == t - 1, 0.0, pltpu.roll(z, t - 1, 0))
        return w_ref[0:1, :] * prev + w_ref[1:2, :] * z + w_ref[2:3, :] * nxt + b_ref[...]

    s = conv3(zv_ref, wv_ref, bv_ref) * conv3(z1_ref, w1_ref, b1_ref)
    alt = (1 - 2 * (row & 1)).astype(F32)
    sb = s.astype(BF16)
    xny = jnp.sum(s * alt, axis=0, keepdims=True)
    y_ref[...] = alt * (xny * hny_ref[0:1, :]) + s * bd_ref[...]
    fb = min(HY_FB, t)
    for f0 in range(0, t, fb):
        fr = slice(f0, f0 + fb)
        a = _dot(cm_ref[fr, :], sb)
        bq = _dot(sm_ref[fr, :], sb)
        hre = hre_ref[fr, :]
        him = him_ref[fr, :]
        yc = (a * hre + bq * him).astype(BF16)
        ys = (bq * hre - a * him).astype(BF16)
        y_ref[...] += _dot(cm_ref[:, fr], yc) + _dot(sm_ref[:, fr], ys)
    o_ref[0] = (conv3(z0_ref, w0_ref, b0_ref) * y_ref[...]).astype(o_ref.dtype)


def _hyena_call(z, conv_w, conv_b, bias_d, cm, sm, hre, him, hny):
    b, t, _ = z.shape
    nb = HY_W // HY_CB
    zspec = lambda k: pl.BlockSpec((1, t, HY_CB), lambda bi, ci: (bi, 0, k * nb + ci))
    wspec = lambda k: pl.BlockSpec((3, HY_CB), lambda bi, ci: (0, k * nb + ci))
    bspec = lambda k: pl.BlockSpec((1, HY_CB), lambda bi, ci: (0, k * nb + ci))
    cspec = lambda rows: pl.BlockSpec((rows, HY_CB), lambda bi, ci: (0, ci))
    conv_b = conv_b.reshape(1, -1)
    return pl.pallas_call(
        _hyena_kernel,
        out_shape=jax.ShapeDtypeStruct((b, t, HY_W), BF16),
        grid=(b, nb),
        in_specs=[zspec(0), zspec(1), zspec(2), wspec(0), wspec(1), wspec(2), bspec(0), bspec(1), bspec(2),
                  cspec(1), _const_spec(cm.shape), _const_spec(sm.shape), cspec(t), cspec(t), cspec(8)],
        out_specs=pl.BlockSpec((1, t, HY_CB), lambda bi, ci: (bi, 0, ci)),
        scratch_shapes=[pltpu.VMEM((t, HY_CB), F32)],
        compiler_params=_cparams("parallel", "parallel"),
        name="hyena",
    )(z, z, z, conv_w, conv_w, conv_w, conv_b, conv_b, conv_b, bias_d.reshape(1, -1), cm, sm, hre, him, hny)


S5_CW = S5_CHUNK * S5_H


S5_GPS = 128 // S5_H
S5_NB = 4
S5_RB = 64


def _block_transpose(xs, blk):
    xs = list(xs)
    n = len(xs)
    d = n // 2
    while d:
        keep = (blk & d) == 0
        new = list(xs)
        for i in range(n):
            if i & d:
                continue
            a, b = xs[i], xs[i + d]
            new[i] = jnp.where(keep, a, pltpu.roll(b, d * S5_H, 1))
            new[i + d] = jnp.where(keep, pltpu.roll(a, 128 - d * S5_H, 1), b)
        xs = new
        d //= 2
    return xs


def _s5_kernel(xc_ref, xl_ref, m_ref, w_ref, v_ref, lam_ref, d_ref, yc_ref, yl_ref,
               u_ref, s_ref, xpf_ref, xpb_ref, y_ref, *, nb, n_cc, n_lc):
    parts = ((xc_ref, yc_ref, 0, n_cc), (xl_ref, yl_ref, nb * n_cc, n_lc))
    blk = lax.broadcasted_iota(jnp.int32, (1, 128), 1) // S5_H
    fwd = lax.broadcasted_iota(jnp.int32, (1, 128), 1) < S5_P

    def row_blocks(n_chunks):
        rb = min(S5_RB, n_chunks)
        return [(c0, rb) for c0 in range(0, n_chunks, rb)]

    def gather_rows(b, carry):
        for x_ref, _, base, n_chunks in parts:
            for c0, cn in row_blocks(n_chunks):
                xs = [x_ref[j, b, c0:c0 + cn, :] for j in range(S5_CHUNK)]
                rows_b = pl.ds(base + c0 * nb + b, cn, stride=nb)
                for h in range(2):
                    half = _block_transpose(xs[h * S5_GPS:(h + 1) * S5_GPS], blk)
                    for g in range(S5_GPS):
                        u_ref[g, h, rows_b, :] = half[g]
        return carry

    lax.fori_loop(0, nb, gather_rows, 0)

    def group_inputs(g):
        return jnp.concatenate([u_ref[g, 0], u_ref[g, 1]], axis=1).astype(BF16)

    for g in range(S5_GPS):
        s = _dot(group_inputs(g), w_ref[g])
        s_ref[g, 0] = s[:, 0:128]
        s_ref[g, 1] = s[:, 128:256]

    def scan_part(carry, base, n_chunks):
        def body(k, carry):
            kb = n_chunks - 1 - k
            rf = pl.ds(pl.multiple_of(base + k * nb, nb), nb)
            rb = pl.ds(pl.multiple_of(base + kb * nb, nb), nb)
            new = []
            for g in range(S5_GPS):
                xa, xb = carry[2 * g], carry[2 * g + 1]
                ar, ai = lam_ref[g, 0:1, :], lam_ref[g, 1:2, :]
                xpf_ref[g, 0, rf, :] = xa
                xpf_ref[g, 1, rf, :] = xb
                xpb_ref[g, 0, rb, :] = xa
                xpb_ref[g, 1, rb, :] = xb
                in_a = jnp.where(fwd, s_ref[g, 0, rf, :], s_ref[g, 0, rb, :])
                in_b = jnp.where(fwd, s_ref[g, 1, rf, :], s_ref[g, 1, rb, :])
                new += [ar * xa - ai * xb + in_a, ar * xb + ai * xa + in_b]
            return tuple(new)

        return lax.fori_loop(0, n_chunks, body, carry)

    carry = tuple(jnp.zeros((nb, 128), F32) for _ in range(2 * S5_GPS))
    for _, _, base, n_chunks in parts:
        carry = scan_part(carry, base, n_chunks)

    for g in range(S5_GPS):
        xp = jnp.concatenate([jnp.where(fwd, xpf_ref[g, h], xpb_ref[g, h]) for h in range(2)], axis=1).astype(BF16)
        y = _dot(group_inputs(g), m_ref[g]) + _dot(xp, v_ref[g])
        y_ref[g, 0] = y[:, 0:128]
        y_ref[g, 1] = y[:, 128:256]

    def scatter_rows(b, carry):
        for x_ref, o_ref, base, n_chunks in parts:
            for c0, cn in row_blocks(n_chunks):
                rows_b = pl.ds(base + c0 * nb + b, cn, stride=nb)
                halves = [_block_transpose([y_ref[g, h, rows_b, :] for g in range(S5_GPS)], blk) for h in range(2)]
                for j in range(S5_CHUNK):
                    half, jj = divmod(j, S5_GPS)
                    y = halves[half][jj] + d_ref[...] * x_ref[j, b, c0:c0 + cn, :]
                    cdf = 0.5 * (1.0 + jnp.tanh(math.sqrt(2.0 / math.pi) * (y + 0.044715 * (y * y * y))))
                    o_ref[j, b, c0:c0 + cn, :] = y * cdf
        return carry

    lax.fori_loop(0, nb, scatter_rows, 0)


def _s5_call(xc, xl, m, w, v, lam, d):
    _, b, n_cc, width = xc.shape
    n_lc = xl.shape[2]
    nb = min(S5_NB, b)
    rows = nb * (n_cc + n_lc)
    xspec = lambda n: pl.BlockSpec((S5_CHUNK, nb, n, 128), lambda mi, bi: (0, bi, 0, mi))
    gspec = lambda a: pl.BlockSpec((S5_GPS,) + a.shape[1:], lambda mi, bi: (mi, 0, 0))
    return pl.pallas_call(
        functools.partial(_s5_kernel, nb=nb, n_cc=n_cc, n_lc=n_lc),
        out_shape=[jax.ShapeDtypeStruct(xc.shape, F32), jax.ShapeDtypeStruct(xl.shape, F32)],
        grid=(width // 128, b // nb),
        in_specs=[xspec(n_cc), xspec(n_lc), gspec(m), gspec(w), gspec(v), gspec(lam),
                  pl.BlockSpec((1, 128), lambda mi, bi: (0, mi))],
        out_specs=[xspec(n_cc), xspec(n_lc)],
        scratch_shapes=[pltpu.VMEM((S5_GPS, 2, rows, 128), F32) for _ in range(5)],
        compiler_params=_cparams("parallel", "parallel"),
        name="s5",
    )(xc, xl, m, w, v, lam, d)


def _merge_kernel(x_ref, m_ref, g_ref, a_ref, b_ref, c_ref, d_ref, wg_ref, wb_ref, wglu_ref, bglu_ref, wo_ref,
                  o_ref, acc_ref, cy_ref):
    x = x_ref[0]
    tm, d = x.shape
    h = _modnorm(x, g_ref[2:3, :], m_ref[0, 3:4, :], m_ref[0, 4:5, :]).astype(BF16)
    for j in range(S5_CHUNK):
        for k in range(cy_ref.shape[0]):
            cy_ref[k, pl.ds(j, tm // S5_CHUNK, stride=S5_CHUNK), :] = c_ref[j, 0, :, k * 128:(k + 1) * 128]
    cy = jnp.concatenate([cy_ref[k] for k in range(cy_ref.shape[0])], axis=1)
    glu = _dot(cy.astype(BF16), wglu_ref[...]) + bglu_ref[...]
    nw = glu.shape[-1] // 2
    c = (glu[:, :nw] * _sigmoid(glu[:, nw:])).astype(BF16)
    for n, br in enumerate((a_ref[0], b_ref[0], c, d_ref[0])):
        gate = _sigmoid(_dot(h, wg_ref[:, W_PROJ_COLS + n * d:W_PROJ_COLS + (n + 1) * d]))
        part = gate * _dot(br, wb_ref[n])
        if n == 0:
            acc_ref[...] = part
        else:
            acc_ref[...] += part
    y = _dot(acc_ref[...].astype(BF16), wo_ref[...])
    o_ref[0] = x + m_ref[0, 5:6, :] * _rms(y, g_ref[3:4, :])


def _merge_call(x, mods, mod_row, g, a, b_, c, d_, wg, wb, wglu, bglu, wo):
    b, t, d = x.shape
    tm = min(MERGE_ROWS, t)
    mrow = (lambda bi, ti: (bi, 0, 0)) if mod_row is None else (lambda bi, ti: (mod_row, 0, 0))
    row = lambda n: pl.BlockSpec((1, tm, n), lambda bi, ti: (bi, ti, 0))
    bw = a.shape[-1]
    return pl.pallas_call(
        _merge_kernel,
        out_shape=jax.ShapeDtypeStruct(x.shape, F32),
        grid=(b, t // tm),
        in_specs=[row(d), pl.BlockSpec((1, N_MOD, d), mrow), _const_spec(g.shape),
                  row(bw), row(bw),
                  pl.BlockSpec((S5_CHUNK, 1, tm // S5_CHUNK, bw), lambda bi, ti: (0, bi, ti, 0)), row(bw),
                  _wspec(wg), _wspec(wb), _wspec(wglu), _const_spec(bglu.shape), _wspec(wo)],
        out_specs=row(d),
        scratch_shapes=[pltpu.VMEM((tm, d), F32), pltpu.VMEM((bw // 128, tm, 128), F32)],
        compiler_params=_cparams("parallel", "parallel"),
        name="merge",
    )(x, mods, g, a, b_, c, d_, _warr(wg), _warr(wb), _warr(wglu), bglu, _warr(wo))


def _rope_table(n_tok):
    q = MLA_ROPE // 4
    t = jnp.arange(n_tok)
    pos = jnp.stack([t // GRID_W, t % GRID_W], axis=-1).astype(F32)
    inv = ROPE_BASE ** (-jnp.arange(q, dtype=F32) / q)
    ang = pos[:, :, None] * inv
    cos, sin = jnp.cos(ang), jnp.sin(ang)
    cos_t = jnp.stack([cos, cos], axis=2).reshape(n_tok, MLA_ROPE)
    sin_t = jnp.stack([-sin, sin], axis=2).reshape(n_tok, MLA_ROPE)
    return jnp.concatenate([cos_t, sin_t], axis=-1)


def _rope_swap(w):
    q = MLA_ROPE // 4
    return w.reshape(w.shape[:-1] + (2, 2, q))[..., ::-1, :].reshape(w.shape)


def _dft_mats(t):
    n = 2 * t
    nb = 64 if t % 64 == 0 else 1
    tt = jnp.arange(t, dtype=jnp.int32)[None, :]
    ang = lambda f: ((f[:, None] * tt) % n).astype(F32) * (2.0 * math.pi / n)
    aa = ang(jnp.arange(t // nb, dtype=jnp.int32) * nb)[:, None, :]
    ab = ang(jnp.arange(nb, dtype=jnp.int32))[None, :, :]
    ca, sa, cb, sb = lax.optimization_barrier((jnp.cos(aa), jnp.sin(aa), jnp.cos(ab), jnp.sin(ab)))
    cm = (ca * cb - sa * sb).reshape(t, t)
    sm = (sa * cb + ca * sb).reshape(t, t)
    return cm.astype(BF16), sm.astype(BF16)


def _hyena_consts(n_tok):
    bands = (HY_EMB - 1) // 2
    t = jnp.arange(n_tok, dtype=F32)
    t01 = jnp.linspace(0.0, 1.0, n_tok, dtype=F32)[:, None]
    ang = (2.0 * math.pi * t / n_tok)[:, None] * jnp.linspace(1e-4, bands - 1, bands, dtype=F32)
    z = jnp.concatenate([t01, jnp.cos(ang), -jnp.sin(ang)], axis=-1)
    z = jnp.pad(z, ((0, 0), (0, 128 - HY_EMB)))
    max_decay = math.log(HY_TARGET) / HY_FAST_PCT
    min_decay = math.log(HY_TARGET) / HY_SLOW_PCT
    deltas = jnp.abs(jnp.linspace(min_decay, max_decay, HY_W, dtype=F32))
    return z, jnp.exp(-t01 * deltas)


def _s5_operators(lam_re, lam_im, log_dt, b_re, b_im, c_re, c_im):
    tc = S5_CHUNK
    lam = lax.complex(jnp.minimum(lam_re, -1e-4), lam_im)
    lam_dt = lam * jnp.exp(log_dt)[..., None]
    lam_bar = jnp.exp(lam_dt)
    b_bar = ((lam_bar - 1.0) / lam)[..., None] * lax.complex(b_re, b_im)
    cm = lax.complex(c_re, c_im)
    pw = jnp.exp(lam_dt[..., None] * jnp.arange(tc + 1, dtype=F32))
    g = lam_re.shape[1]
    kern = jnp.real(jnp.einsum('dgap,dgpk,dgph->dkgah', cm, pw[..., :tc], b_bar,
                               precision=lax.Precision.HIGHEST))
    s = jnp.arange(tc)[:, None, None]
    j = jnp.arange(tc)[None, :, None]
    lag = jnp.arange(tc)[None, None, :]
    place = jnp.stack([(j - s == lag), (s - j == lag)]).astype(F32)
    m = jnp.einsum('dsjk,dkgah->gshja', place, kern, precision=lax.Precision.HIGHEST)
    m = m.reshape(g, S5_CW, S5_CW)
    pf = pw[0][..., :tc][..., ::-1]
    pb = pw[1][..., :tc]
    wf = jnp.einsum('gps,gph->gshp', pf, b_bar[0]).reshape(g, S5_CW, S5_P)
    wb = jnp.einsum('gps,gph->gshp', pb, b_bar[1]).reshape(g, S5_CW, S5_P)
    w = jnp.concatenate([wf.real, wb.real, wf.imag, wb.imag], axis=-1)
    gf = jnp.einsum('gap,gpj->gpja', cm[0], pw[0][..., 1:tc + 1]).reshape(g, S5_P, S5_CW)
    gb = jnp.einsum('gap,gpj->gpja', cm[1], pw[1][..., 1:tc + 1][..., ::-1]).reshape(g, S5_P, S5_CW)
    v = jnp.concatenate([gf.real, gb.real, -gf.imag, -gb.imag], axis=1)
    a16 = pw[..., tc]
    lam16 = jnp.stack([jnp.concatenate([a16[0].real, a16[1].real], -1),
                       jnp.concatenate([a16[0].imag, a16[1].imag], -1)], axis=1)
    lam16 = jnp.pad(lam16, ((0, 0), (0, 6), (0, 0)))
    return m.astype(BF16), w.astype(BF16), v.astype(BF16), lam16


def _prep_layer(i, p):
    wi = [_Sel(p['ffn_wi_bf16'], (i, k)) for k in range(2)]
    wo = [_Sel(p['ffn_wo_bf16'], (i, k)) for k in range(2)]
    w_proj = w_gate = _Sel(p['w_all_bf16'], (i,))
    wuq = p['mla_w_uq'][i]
    rope = wuq[..., MLA_NOPE:]
    wq = jnp.concatenate([wuq[..., :MLA_NOPE], rope, _rope_swap(rope)], axis=-1)
    wq = wq.reshape(MLA_RANK, MLA_HEADS * MLA_KW).astype(BF16)
    wukv = p['mla_w_ukv'][i]
    wkv = jnp.concatenate([wukv[..., :MLA_NOPE].reshape(MLA_RANK, -1), wukv[..., MLA_NOPE:].reshape(MLA_RANK, -1)],
                          axis=1).astype(BF16)
    s5m, s5w, s5v, s5lam = (a[i] for a in p['s5_ops'])
    s5d = p['s5_d'][i].reshape(1, -1)
    return dict(
        g=p['norm_g'][i], wi=wi, wo=wo, w_proj=w_proj, w_gate=w_gate,
        gq=p['mla_g_q'][i].reshape(1, -1), gkv=p['mla_g_kv'][i].reshape(1, -1), wq=wq, wkv=wkv,
        bias=_na_bias_table(p['na_rpb'][i]),
        hy_mlp=(jnp.pad(p['hy_w1'][i], ((0, 128 - HY_EMB), (0, 0))), p['hy_b1'][i].reshape(1, -1),
                p['hy_freq1'][i].reshape(1, -1), p['hy_w2'][i], p['hy_b2'][i].reshape(1, -1),
                p['hy_freq2'][i].reshape(1, -1), p['hy_w3'][i]),
        conv_w=p['hy_conv_w'][i], conv_b=p['hy_conv_b'][i], hy_bias=p['hy_bias'][i],
        s5m=s5m, s5w=s5w, s5v=s5v, s5lam=s5lam, s5d=s5d,
        wglu=_Sel(p['wglu_bf16'], (i,)), bglu=p['s5_b_glu'][i].reshape(1, -1),
        wb=_Sel(p['wb_bf16'], (i,)), w_out=_Sel(p['w_out_bf16'], (i,)),
    )


def _stacked_bf16(p):
    w_in = p['w_in']
    kr_end = IN_SPLITS[0] + IN_SPLITS[1]
    kr = w_in[:, :, IN_SPLITS[0]:kr_end]
    w_all = jnp.concatenate([w_in[:, :, :kr_end], _rope_swap(kr), w_in[:, :, kr_end:]], axis=2).astype(BF16)
    assert w_all.shape[2] == W_ALL_COLS
    return dict(ffn_wi_bf16=p['ffn_w_in'].astype(BF16), ffn_wo_bf16=p['ffn_w_out'].astype(BF16),
                w_all_bf16=w_all,
                wglu_bf16=p['s5_w_glu'].astype(BF16), wb_bf16=p['w_branch'].astype(BF16),
                w_out_bf16=p['w_out'].astype(BF16))


def _mixer(xc, xl, mods, w, consts, ctx_out):
    b, n_lat, _ = xl.shape
    n_ctx = xc.shape[1]
    ctx_row = b
    mla_l, nq_l, nkv_l, u_l, hy_l = _inproj_call(xl, mods, None, w['g'], w['w_proj'])
    mla_c, nq_c, nkv_c, u_c, hy_c = _inproj_call(xc, mods, ctx_row, w['g'], w['w_proj'])

    scale = (MLA_NOPE + MLA_ROPE) ** -0.5 * math.log2(math.e)
    rope_l = consts['rope_l']
    tabq_l = jnp.concatenate([jnp.ones((n_lat, MLA_NOPE), F32), rope_l], axis=-1) * scale
    a_l = _mla_call(mla_l, mla_c, mla_l, tabq_l, rope_l, w['gq'], w['gkv'], w['wq'], w['wkv'])
    d_l = _na_call(nq_l, nkv_c, nkv_l, w['bias'])
    cy_c, cy_l = _s5_call(u_c, u_l, w['s5m'], w['s5w'], w['s5v'], w['s5lam'], w['s5d'])
    hre, him, hny = _hyfilt_call(consts['hy_z_l'], *w['hy_mlp'], consts['hy_dec_l'], *consts['dft_l'])
    b_l = _hyena_call(hy_l, w['conv_w'], w['conv_b'], w['hy_bias'], *consts['dft_l'], hre, him, hny)
    merge_w = (w['w_gate'], w['wb'], w['wglu'], w['bglu'], w['w_out'])
    xl = _merge_call(xl, mods, None, w['g'], a_l, b_l, cy_l, d_l, *merge_w)
    if not ctx_out:
        return None, xl

    tabq_c = jnp.concatenate([jnp.ones((n_ctx, MLA_NOPE + MLA_ROPE), F32), jnp.zeros((n_ctx, MLA_ROPE), F32)],
                             axis=-1) * scale
    a_c = _mla_call(mla_c, mla_c, None, tabq_c, None, w['gq'], w['gkv'], w['wq'], w['wkv'])
    d_c = _na_call(nq_c, nkv_c, None, None)
    hre, him, hny = _hyfilt_call(consts['hy_z_c'], *w['hy_mlp'], consts['hy_dec_c'], *consts['dft_c'])
    b_c = _hyena_call(hy_c, w['conv_w'], w['conv_b'], w['hy_bias'], *consts['dft_c'], hre, him, hny)
    xc = _merge_call(xc, mods, ctx_row, w['g'], a_c, b_c, cy_c, d_c, *merge_w)
    return xc, xl


def kernel(x, c, ctx, c_ctx, w_mod, b_mod, norm_g, ffn_w_in, ffn_w_out, w_in, mla_g_q, mla_g_kv, mla_w_uq, mla_w_ukv, na_rpb, hy_conv_w, hy_conv_b, hy_bias, hy_w1, hy_b1, hy_freq1, hy_w2, hy_b2, hy_freq2, hy_w3, s5_lam_re, s5_lam_im, s5_log_dt, s5_b_re, s5_b_im, s5_c_re, s5_c_im, s5_d, s5_w_glu, s5_b_glu, w_branch, w_out):
    p = dict(norm_g=norm_g, ffn_w_in=ffn_w_in, ffn_w_out=ffn_w_out, w_in=w_in, mla_g_q=mla_g_q, mla_g_kv=mla_g_kv,
             mla_w_uq=mla_w_uq, mla_w_ukv=mla_w_ukv, na_rpb=na_rpb, hy_conv_w=hy_conv_w, hy_conv_b=hy_conv_b,
             hy_bias=hy_bias, hy_w1=hy_w1, hy_b1=hy_b1, hy_freq1=hy_freq1, hy_w2=hy_w2, hy_b2=hy_b2,
             hy_freq2=hy_freq2, hy_w3=hy_w3, s5_lam_re=s5_lam_re, s5_lam_im=s5_lam_im, s5_log_dt=s5_log_dt,
             s5_b_re=s5_b_re, s5_b_im=s5_b_im, s5_c_re=s5_c_re, s5_c_im=s5_c_im, s5_d=s5_d, s5_w_glu=s5_w_glu,
             s5_b_glu=s5_b_glu, w_branch=w_branch, w_out=w_out)
    p.update(_stacked_bf16(p))
    p['s5_ops'] = jax.vmap(_s5_operators)(s5_lam_re, s5_lam_im, s5_log_dt, s5_b_re, s5_b_im, s5_c_re, s5_c_im)
    b, n_lat, d = x.shape
    n_ctx = ctx.shape[1]
    depth = w_mod.shape[0]
    assert b % 8 == 0 and n_lat % 256 == 0 and n_ctx % 256 == 0

    rows = -(-(b + 1) // 8) * 8
    acts = jnp.concatenate([c, c_ctx[None, :], jnp.zeros((rows - b - 1, d), F32)], axis=0)
    mods = _mod_call(acts, w_mod, b_mod).reshape(depth, rows, N_MOD, d)

    z_l, dec_l = _hyena_consts(n_lat)
    z_c, dec_c = _hyena_consts(n_ctx)
    consts = dict(rope_l=_rope_table(n_lat), dft_l=_dft_mats(n_lat), dft_c=_dft_mats(n_ctx),
                  hy_z_l=z_l, hy_dec_l=dec_l, hy_z_c=z_c, hy_dec_c=dec_c)

    xc, xl = ctx, x
    for i in range(depth):
        ctx_out = i < depth - 1
        w = _prep_layer(i, p)
        m = mods[i]
        xl = _ffn_call(xl, m, None, w['g'], w['wi'][0], w['wo'][0], 0, 0)
        xc = _ffn_call(xc, m, b, w['g'], w['wi'][0], w['wo'][0], 0, 0)
        xc, xl = _mixer(xc, xl, m, w, consts, ctx_out)
        xl = _ffn_call(xl, m, None, w['g'], w['wi'][1], w['wo'][1], 6, 4)
        if ctx_out:
            xc = _ffn_call(xc, m, b, w['g'], w['wi'][1], w['wo'][1], 6, 4)
    return xl
```

```python
import functools
import math

import jax
import jax.numpy as jnp
from jax import lax
from jax.experimental import pallas as pl
from jax.experimental.pallas import tpu as pltpu

F32 = jnp.float32
BF16 = jnp.bfloat16

EPS = 1e-6
GRID_W = 64
ROPE_BASE = 10000.0
N_MOD = 9
MLA_HEADS, MLA_NOPE, MLA_ROPE, MLA_V = 4, 128, 64, 128
MLA_RANK = 256
NA_HEADS, NA_DIM, NA_WIN_R, NA_WIN_C = 8, 64, 8, 16
HY_W, HY_EMB, HY_FILT = 512, 33, 64
HY_FAST_PCT, HY_SLOW_PCT, HY_TARGET = 0.3, 1.5, 1e-2
S5_H, S5_P = 16, 64
S5_CHUNK = 16
IN_SPLITS = (256, 64, 512, 512, 512, 256, 512, 1536, 4096)
FFN_CHUNK = 256
VMEM_LIMIT = 56 * 1024 * 1024
FFN_ROWS, INPROJ_ROWS, MERGE_ROWS = 1024, 1024, 512


def _cparams(*sem):
    return pltpu.CompilerParams(dimension_semantics=sem, vmem_limit_bytes=VMEM_LIMIT)


def _const_spec(shape):
    nd = len(shape)
    return pl.BlockSpec(shape, lambda *_: (0,) * nd, pipeline_mode=pl.Buffered(1))


class _Sel:
    def __init__(self, arr, idx):
        self.arr, self.idx, self.shape = arr, tuple(idx), arr.shape[len(idx):]


def _wspec(w, cols=None):
    if not isinstance(w, _Sel):
        return _const_spec(w.shape)
    idx, nd = w.idx, len(w.shape)
    shape = w.shape if cols is None else w.shape[:-1] + (cols,)
    return pl.BlockSpec((None,) * len(idx) + shape, lambda *_: idx + (0,) * nd, pipeline_mode=pl.Buffered(1))


def _warr(w):
    return w.arr if isinstance(w, _Sel) else w


def _dot(a, b):
    return jnp.dot(a, b, preferred_element_type=F32)


def _dot_nt(a, b):
    return lax.dot_general(a, b, (((1,), (1,)), ((), ())), preferred_element_type=F32)


def _sigmoid(x):
    return 1.0 / (1.0 + jnp.exp(-x))


def _rms(x, g):
    return x * lax.rsqrt(jnp.mean(x * x, axis=-1, keepdims=True) + EPS) * g


def _modnorm(x, g, shift, scale):
    return _rms(x, g) * (1.0 + scale) + shift


def _mod_kernel(act_ref, w_ref, b_ref, o_ref):
    a = act_ref[...]
    a = a * _sigmoid(a)
    a_hi = a.astype(BF16)
    a_lo = (a - a_hi.astype(F32)).astype(BF16)
    w = w_ref[0]
    w_hi = w.astype(BF16)
    w_lo = (w - w_hi.astype(F32)).astype(BF16)
    o_ref[0] = _dot(a_hi, w_hi) + _dot(a_lo, w_hi) + _dot(a_hi, w_lo) + b_ref[0]


def _mod_call(acts, w_mod, b_mod):
    nl, d, nd = w_mod.shape
    rows = acts.shape[0]
    tn = 1152
    return pl.pallas_call(
        _mod_kernel,
        out_shape=jax.ShapeDtypeStruct((nl, rows, nd), F32),
        grid=(nl, nd // tn),
        in_specs=[pl.BlockSpec((rows, d), lambda l, j: (0, 0)),
                  pl.BlockSpec((1, d, tn), lambda l, j: (l, 0, j)),
                  pl.BlockSpec((1, 1, tn), lambda l, j: (l, 0, j))],
        out_specs=pl.BlockSpec((1, rows, tn), lambda l, j: (l, 0, j)),
        compiler_params=_cparams("parallel", "parallel"),
        name="mod",
    )(acts, w_mod, b_mod.reshape(nl, 1, nd))


def _ffn_kernel(x_ref, m_ref, g_ref, wi_ref, wo_ref, o_ref, acc_ref, *, base, gidx, n_chunks):
    x = x_ref[0]
    shift, scale, gate = (m_ref[0, base + k:base + k + 1, :] for k in range(3))
    h = _modnorm(x, g_ref[gidx:gidx + 1, :], shift, scale).astype(BF16)
    f = n_chunks * FFN_CHUNK
    for j in range(n_chunks):
        cs = slice(j * FFN_CHUNK, (j + 1) * FFN_CHUNK)
        a = _dot(h, wi_ref[:, cs])
        b = _dot(h, wi_ref[:, f + j * FFN_CHUNK:f + (j + 1) * FFN_CHUNK])
        act = (a * _sigmoid(a) * b).astype(BF16)
        part = _dot(act, wo_ref[cs, :])
        if j == 0:
            acc_ref[...] = part
        else:
            acc_ref[...] += part
    y = acc_ref[...]
    o_ref[0] = x + 0.5 * gate * _rms(y, g_ref[gidx + 1:gidx + 2, :])


def _ffn_call(x, mods, mod_row, g, wi, wo, base, gidx):
    b, t, d = x.shape
    tm = min(FFN_ROWS, t)
    n_chunks = wo.shape[0] // FFN_CHUNK
    mrow = (lambda bi, ti: (bi, 0, 0)) if mod_row is None else (lambda bi, ti: (mod_row, 0, 0))
    return pl.pallas_call(
        functools.partial(_ffn_kernel, base=base, gidx=gidx, n_chunks=n_chunks),
        out_shape=jax.ShapeDtypeStruct(x.shape, F32),
        grid=(b, t // tm),
        in_specs=[pl.BlockSpec((1, tm, d), lambda bi, ti: (bi, ti, 0)),
                  pl.BlockSpec((1, N_MOD, d), mrow),
                  _const_spec(g.shape), _wspec(wi), _wspec(wo)],
        out_specs=pl.BlockSpec((1, tm, d), lambda bi, ti: (bi, ti, 0)),
        scratch_shapes=[pltpu.VMEM((tm, d), F32)],
        compiler_params=_cparams("parallel", "parallel"),
        name="ffn",
    )(x, mods, g, _warr(wi), _warr(wo))


INPROJ_COLS = (640, 512, 1024, 512, 1536)
INPROJ_SRC = (((0, 256), (1920, 2176), (256, 384)), ((2176, 2688),), ((384, 1408),), ((1408, 1920),), ((2688, 4224),))
W_PROJ_COLS = 4224
W_ALL_COLS = W_PROJ_COLS + 4096


def _inproj_kernel(x_ref, m_ref, g_ref, w_ref, o_mla, o_nq, o_nkv, o_u, o_hy, u_ref):
    x = x_ref[0]
    tm = x.shape[0]
    h = _modnorm(x, g_ref[2:3, :], m_ref[0, 3:4, :], m_ref[0, 4:5, :]).astype(BF16)
    for o_ref, pieces in zip((o_mla, o_nq, o_nkv, o_u, o_hy), INPROJ_SRC):
        off = 0
        for lo, hi in pieces:
            n = hi - lo
            z = _dot(h, w_ref[:, lo:hi])
            if o_ref is o_u:
                for k in range(n // 128):
                    u_ref[k] = z[:, k * 128:(k + 1) * 128]
                for j in range(S5_CHUNK):
                    for k in range(n // 128):
                        o_ref[j, 0, :, k * 128:(k + 1) * 128] = u_ref[k, pl.ds(j, tm // S5_CHUNK, stride=S5_CHUNK), :]
            else:
                o_ref[0, :, off:off + n] = z.astype(o_ref.dtype)
            off += n


def _inproj_call(x, mods, mod_row, g, w):
    b, t, d = x.shape
    tm = min(INPROJ_ROWS, t)
    mrow = (lambda bi, ti: (bi, 0, 0)) if mod_row is None else (lambda bi, ti: (mod_row, 0, 0))
    dts = (F32, BF16, BF16, F32, F32)
    out_shape = [jax.ShapeDtypeStruct((b, t, n), dt) for n, dt in zip(INPROJ_COLS, dts)]
    out_specs = [pl.BlockSpec((1, tm, n), lambda bi, ti: (bi, ti, 0)) for n in INPROJ_COLS]
    nu = INPROJ_COLS[3]
    out_shape[3] = jax.ShapeDtypeStruct((S5_CHUNK, b, t // S5_CHUNK, nu), F32)
    out_specs[3] = pl.BlockSpec((S5_CHUNK, 1, tm // S5_CHUNK, nu), lambda bi, ti: (0, bi, ti, 0))
    return pl.pallas_call(
        _inproj_kernel,
        out_shape=out_shape,
        grid=(b, t // tm),
        in_specs=[pl.BlockSpec((1, tm, d), lambda bi, ti: (bi, ti, 0)),
                  pl.BlockSpec((1, N_MOD, d), mrow),
                  _const_spec(g.shape), _wspec(w, W_PROJ_COLS)],
        out_specs=out_specs,
        scratch_shapes=[pltpu.VMEM((nu // 128, tm, 128), F32)],
        compiler_params=_cparams("parallel", "parallel"),
        name="inproj",
    )(x, mods, g, _warr(w))


MLA_KW = 256
MLA_FILL_ROWS = 512
MLA_QUERY_ROWS = 512


def _mla_kernel(*refs, n_ctx, n_lat, tq):
    if n_lat:
        zq_ref, zc_ref, zl_ref, tabq_ref, tabk_ref, gq_ref, gkv_ref, wq_ref, wkv_ref, o_ref, k_ref, v_ref = refs
    else:
        zq_ref, zc_ref, tabq_ref, gq_ref, gkv_ref, wq_ref, wkv_ref, o_ref, k_ref, v_ref = refs
    half = lax.broadcasted_iota(jnp.int32, (1, 128), 1) < MLA_ROPE

    def fill_kv(z, row0, tab):
        n = z.shape[0]
        kv = _dot(_rms(z[:, 0:MLA_RANK], gkv_ref[...]).astype(BF16), wkv_ref[...])
        r = z[:, 2 * MLA_RANK:2 * MLA_RANK + 128]
        if tab is None:
            rot = jnp.where(half, r, pltpu.roll(r, MLA_ROPE, 1))
        else:
            r = r * tab
            rot = r + pltpu.roll(r, MLA_ROPE, 1)
        rot = rot.astype(BF16)
        for h in range(MLA_HEADS):
            k_ref[h, pl.ds(row0, n), 0:MLA_NOPE] = kv[:, h * MLA_NOPE:(h + 1) * MLA_NOPE].astype(BF16)
            k_ref[h, pl.ds(row0, n), MLA_NOPE:MLA_KW] = rot
        v_ref[pl.ds(row0, n), :] = kv[:, MLA_HEADS * MLA_NOPE:].astype(BF16)

    @pl.when(pl.program_id(1) == 0)
    def _():
        fill_kv(zc_ref[0], 0, None)
        if n_lat:
            ck = min(MLA_FILL_ROWS, n_lat)
            for r0 in range(0, n_lat, ck):
                fill_kv(zl_ref[0, r0:r0 + ck, :], n_ctx + r0, tabk_ref[r0:r0 + ck, :])

    zq = zq_ref[0]
    q = _dot(_rms(zq[:, MLA_RANK:2 * MLA_RANK], gq_ref[...]).astype(BF16), wq_ref[...])
    tabq = tabq_ref[...]
    n_keys = n_ctx + n_lat
    n_split = 2 if n_keys % 256 == 0 and n_keys >= 1024 else 1
    kb = n_keys // n_split
    for h in range(MLA_HEADS):
        qh = (q[:, h * MLA_KW:(h + 1) * MLA_KW] * tabq).astype(BF16)
        ss = [_dot_nt(qh, k_ref[h, i * kb:(i + 1) * kb, :]) for i in range(n_split)]
        m = functools.reduce(jnp.maximum, [jnp.max(s, axis=-1, keepdims=True) for s in ss])
        o = l = 0.0
        for i, s in enumerate(ss):
            p = jnp.exp2(s - m)
            l = l + jnp.sum(p, axis=-1, keepdims=True)
            o = o + _dot(p.astype(BF16), v_ref[i * kb:(i + 1) * kb, h * MLA_V:(h + 1) * MLA_V])
        o_ref[0, :, h * MLA_V:(h + 1) * MLA_V] = (o / l).astype(o_ref.dtype)


def _mla_call(zq, zc, zl, tabq, tabk, gq, gkv, wq, wkv):
    b, t, w = zq.shape
    n_ctx = zc.shape[1]
    n_lat = 0 if zl is None else zl.shape[1]
    tq = min(MLA_QUERY_ROWS, t)
    n_keys = n_ctx + n_lat
    args = [zq, zc] + ([zl] if n_lat else []) + [tabq] + ([tabk] if n_lat else []) + [gq, gkv, wq, wkv]
    in_specs = [pl.BlockSpec((1, tq, w), lambda bi, qi: (bi, qi, 0)),
                pl.BlockSpec((1, n_ctx, w), lambda bi, qi: (bi, 0, 0))]
    if n_lat:
        in_specs.append(pl.BlockSpec((1, n_lat, w), lambda bi, qi: (bi, 0, 0)))
    in_specs.append(pl.BlockSpec((tq, MLA_KW), lambda bi, qi: (qi, 0)))
    if n_lat:
        in_specs.append(_const_spec(tabk.shape))
    in_specs += [_const_spec(gq.shape), _const_spec(gkv.shape), _const_spec(wq.shape), _const_spec(wkv.shape)]
    return pl.pallas_call(
        functools.partial(_mla_kernel, n_ctx=n_ctx, n_lat=n_lat, tq=tq),
        out_shape=jax.ShapeDtypeStruct((b, t, MLA_HEADS * MLA_V), BF16),
        grid=(b, t // tq),
        in_specs=in_specs,
        out_specs=pl.BlockSpec((1, tq, MLA_HEADS * MLA_V), lambda bi, qi: (bi, qi, 0)),
        scratch_shapes=[pltpu.VMEM((MLA_HEADS, n_keys, MLA_KW), BF16),
                        pltpu.VMEM((n_keys, MLA_HEADS * MLA_V), BF16)],
        compiler_params=_cparams("parallel", "arbitrary"),
        name="mla",
    )(*args)


NA_W = NA_HEADS * NA_DIM
NA_WIN = NA_WIN_R * GRID_W


def _na_kernel(*refs, local, rows):
    if local:
        q_ref, kvc_ref, kvl_ref, bias_ref, o_ref = refs
    else:
        q_ref, kvc_ref, o_ref = refs
    tq = q_ref.shape[1]
    lane_lo = lax.broadcasted_iota(jnp.int32, (1, 128), 1) < NA_DIM
    sub = GRID_W if local else tq
    wins, offs = [], []
    for rr in range(tq // sub if local else 0):
        r = pl.program_id(1) * (tq // sub) + rr
        start = jnp.clip(r - NA_WIN_R // 2, 0, rows - NA_WIN_R)
        offs.append(r - start)
        wins.append(pl.ds(pl.multiple_of(start * GRID_W, GRID_W), NA_WIN))
    scale = NA_DIM ** -0.5
    for j in range(NA_HEADS // 2):
        cols = slice(j * 128, (j + 1) * 128)
        vcols = slice(NA_W + j * 128, NA_W + (j + 1) * 128)
        k_c, v_c = kvc_ref[0, :, cols], kvc_ref[0, :, vcols]
        pieces = []
        for rr in range(tq // sub):
            qs = q_ref[0, rr * sub:(rr + 1) * sub, cols] * jnp.asarray(scale, BF16)
            pieces += [jnp.where(lane_lo, qs, jnp.zeros_like(qs)), jnp.where(lane_lo, jnp.zeros_like(qs), qs)]
        q_all = jnp.concatenate(pieces, axis=0)
        s_c = _dot_nt(q_all, k_c)
        m_c = jnp.max(s_c, axis=-1, keepdims=True)
        p_cs, p_ls, ls = [], [], []
        for rr in range(tq // sub):
            rs = slice(rr * 2 * sub, (rr + 1) * 2 * sub)
            m = m_c[rs]
            if local:
                bias = bias_ref[offs[rr], 2 * j:2 * j + 2].reshape(2 * sub, NA_WIN)
                s_l = _dot_nt(q_all[rs], kvl_ref[0, wins[rr], cols]) + bias
                m = jnp.maximum(m, jnp.max(s_l, axis=-1, keepdims=True))
            p_c = jnp.exp(s_c[rs] - m)
            l = jnp.sum(p_c, axis=-1, keepdims=True)
            if local:
                p_l = jnp.exp(s_l - m)
                l = l + jnp.sum(p_l, axis=-1, keepdims=True)
                p_ls.append(p_l.astype(BF16))
            p_cs.append(p_c.astype(BF16))
            ls.append(l)
        o_c = _dot(jnp.concatenate(p_cs, axis=0), v_c)
        for rr in range(tq // sub):
            rs = slice(rr * 2 * sub, (rr + 1) * 2 * sub)
            o = o_c[rs]
            if local:
                o = o + _dot(p_ls[rr], kvl_ref[0, wins[rr], vcols])
            o = o / ls[rr]
            o_ref[0, rr * sub:(rr + 1) * sub, cols] = jnp.where(lane_lo, o[:sub], o[sub:]).astype(o_ref.dtype)


NA_ROWS_PER_STEP = 8


def _nabias_kernel(r_ref, m_ref, o_ref, *, heads, n_dr):
    lane_lo = lax.broadcasted_iota(jnp.int32, (1, 128), 1) < GRID_W
    mask = m_ref[...]
    shift = (128 - (NA_WIN_C - 1), GRID_W - (NA_WIN_C - 1))

    def body(off, carry):
        for h in range(heads):
            for kp in range(NA_WIN_R // 2):
                halves = []
                for e in range(2):
                    a = 2 * kp + e - off + NA_WIN_R - 1
                    row = jnp.broadcast_to(r_ref[pl.ds(h * n_dr + a, 1), :], (GRID_W, 128))
                    halves.append(pltpu.roll(row, shift[e], 1, stride=1, stride_axis=0))
                o_ref[off, h, :, kp * 128:(kp + 1) * 128] = jnp.where(lane_lo, halves[0], halves[1]) + mask
        return carry

    lax.fori_loop(0, NA_WIN_R, body, 0)


def _na_bias_table(rpb):
    h, na, nb = rpb.shape
    col = jnp.arange(GRID_W)
    c0 = jnp.clip(col - NA_WIN_C // 2, 0, GRID_W - NA_WIN_C)
    in_win = (col[None, :] >= c0[:, None]) & (col[None, :] < c0[:, None] + NA_WIN_C)
    mask = jnp.tile(jnp.where(in_win, 0.0, -jnp.inf).astype(F32), (1, 2))
    r = jnp.pad(rpb.reshape(h * na, nb), ((0, 128 - h * na), (0, 128 - nb)))
    return pl.pallas_call(
        functools.partial(_nabias_kernel, heads=h, n_dr=na),
        out_shape=jax.ShapeDtypeStruct((NA_WIN_R, h, GRID_W, NA_WIN), F32),
        name="nabias",
    )(r, mask)


def _na_call(q, kvc, kvl, bias):
    b, t, _ = q.shape
    n_ctx = kvc.shape[1]
    local = kvl is not None
    rows = t // GRID_W
    if local:
        assert rows >= NA_WIN_R and rows % NA_ROWS_PER_STEP == 0
    tq = GRID_W * NA_ROWS_PER_STEP if local else t
    args = [q, kvc] + ([kvl, bias] if local else [])
    in_specs = [pl.BlockSpec((1, tq, NA_W), lambda bi, ri: (bi, ri, 0)),
                pl.BlockSpec((1, n_ctx, 2 * NA_W), lambda bi, ri: (bi, 0, 0))]
    if local:
        in_specs += [pl.BlockSpec((1, t, 2 * NA_W), lambda bi, ri: (bi, 0, 0)), _const_spec(bias.shape)]
    return pl.pallas_call(
        functools.partial(_na_kernel, local=local, rows=rows),
        out_shape=jax.ShapeDtypeStruct((b, t, NA_W), BF16),
        grid=(b, t // tq),
        in_specs=in_specs,
        out_specs=pl.BlockSpec((1, tq, NA_W), lambda bi, ri: (bi, ri, 0)),
        compiler_params=_cparams("parallel", "parallel"),
        name="na",
    )(*args)


def _hyfilt_kernel(z_ref, w1_ref, b1_ref, f1_ref, w2_ref, b2_ref, f2_ref, w3_ref, dec_ref, cm_ref, sm_ref,
                   hre_ref, him_ref, hny_ref):
    hi = lax.Precision.HIGHEST
    t = z_ref.shape[0]
    n_fft = 2 * t
    h = jnp.sin(f1_ref[...] * (jnp.dot(z_ref[...], w1_ref[...], precision=hi, preferred_element_type=F32) + b1_ref[...]))
    h = jnp.sin(f2_ref[...] * (jnp.dot(h, w2_ref[...], precision=hi, preferred_element_type=F32) + b2_ref[...]))
    taps = jnp.dot(h, w3_ref[...], precision=hi, preferred_element_type=F32)
    dec = dec_ref[...]
    hf = taps[:, :HY_W] * dec
    hb = taps[:, HY_W:] * dec
    row = lax.broadcasted_iota(jnp.int32, (t, 1), 0)
    wgt = jnp.where(row == 0, 1.0 / n_fft, 2.0 / n_fft)
    hsum = hf + hb
    hre_ref[...] = _dot(cm_ref[...], hsum.astype(BF16)) * wgt
    him_ref[...] = _dot(sm_ref[...], (hb - hf).astype(BF16)) * wgt
    alt = (1 - 2 * (row & 1)).astype(F32)
    hny = jnp.sum(hsum * alt, axis=0, keepdims=True) * (1.0 / n_fft)
    hny_ref[...] = jnp.broadcast_to(hny, hny_ref.shape)


def _hyfilt_call(z, w1, b1, f1, w2, b2, f2, w3, dec, cm, sm):
    t = z.shape[0]
    args = (z, w1, b1, f1, w2, b2, f2, w3, dec, cm, sm)
    return pl.pallas_call(
        _hyfilt_kernel,
        out_shape=[jax.ShapeDtypeStruct((t, HY_W), F32), jax.ShapeDtypeStruct((t, HY_W), F32),
                   jax.ShapeDtypeStruct((8, HY_W), F32)],
        grid=(1,),
        in_specs=[_const_spec(a.shape) for a in args],
        out_specs=[pl.BlockSpec((t, HY_W), lambda i: (0, 0)), pl.BlockSpec((t, HY_W), lambda i: (0, 0)),
                   pl.BlockSpec((8, HY_W), lambda i: (0, 0))],
        compiler_params=_cparams("arbitrary"),
        name="hyfilt",
    )(*args)


HY_CB = 256
HY_FB = 512


def _hyena_kernel(zv_ref, z1_ref, z0_ref, wv_ref, w1_ref, w0_ref, bv_ref, b1_ref, b0_ref, bd_ref,
                  cm_ref, sm_ref, hre_ref, him_ref, hny_ref, o_ref, y_ref):
    t = zv_ref.shape[1]
    row = lax.broadcasted_iota(jnp.int32, (t, 1), 0)

    def conv3(z_ref, w_ref, b_ref):
        z = z_ref[0]
        prev = jnp.where(row == 0, 0.0, pltpu.roll(z, 1, 0))
        nxt = jnp.where(row == t - 1, 0.0, pltpu.roll(z, t - 1, 0))
        return w_ref[0:1, :] * prev + w_ref[1:2, :] * z + w_ref[2:3, :] * nxt + b_ref[...]

    s = conv3(zv_ref, wv_ref, bv_ref) * conv3(z1_ref, w1_ref, b1_ref)
    alt = (1 - 2 * (row & 1)).astype(F32)
    sb = s.astype(BF16)
    xny = jnp.sum(s * alt, axis=0, keepdims=True)
    y_ref[...] = alt * (xny * hny_ref[0:1, :]) + s * bd_ref[...]
    fb = min(HY_FB, t)
    for f0 in range(0, t, fb):
        fr = slice(f0, f0 + fb)
        a = _dot(cm_ref[fr, :], sb)
        bq = _dot(sm_ref[fr, :], sb)
        hre = hre_ref[fr, :]
        him = him_ref[fr, :]
        yc = (a * hre + bq * him).astype(BF16)
        ys = (bq * hre - a * him).astype(BF16)
        y_ref[...] += _dot(cm_ref[:, fr], yc) + _dot(sm_ref[:, fr], ys)
    o_ref[0] = (conv3(z0_ref, w0_ref, b0_ref) * y_ref[...]).astype(o_ref.dtype)


def _hyena_call(z, conv_w, conv_b, bias_d, cm, sm, hre, him, hny):
    b, t, _ = z.shape
    nb = HY_W // HY_CB
    zspec = lambda k: pl.BlockSpec((1, t, HY_CB), lambda bi, ci: (bi, 0, k * nb + ci))
    wspec = lambda k: pl.BlockSpec((3, HY_CB), lambda bi, ci: (0, k * nb + ci))
    bspec = lambda k: pl.BlockSpec((1, HY_CB), lambda bi, ci: (0, k * nb + ci))
    cspec = lambda rows: pl.BlockSpec((rows, HY_CB), lambda bi, ci: (0, ci))
    conv_b = conv_b.reshape(1, -1)
    return pl.pallas_call(
        _hyena_kernel,
        out_shape=jax.ShapeDtypeStruct((b, t, HY_W), BF16),
        grid=(b, nb),
        in_specs=[zspec(0), zspec(1), zspec(2), wspec(0), wspec(1), wspec(2), bspec(0), bspec(1), bspec(2),
                  cspec(1), _const_spec(cm.shape), _const_spec(sm.shape), cspec(t), cspec(t), cspec(8)],
        out_specs=pl.BlockSpec((1, t, HY_CB), lambda bi, ci: (bi, 0, ci)),
        scratch_shapes=[pltpu.VMEM((t, HY_CB), F32)],
        compiler_params=_cparams("parallel", "parallel"),
        name="hyena",
    )(z, z, z, conv_w, conv_w, conv_w, conv_b, conv_b, conv_b, bias_d.reshape(1, -1), cm, sm, hre, him, hny)


S5_CW = S5_CHUNK * S5_H


S5_GPS = 128 // S5_H
S5_NB = 4
S5_RB = 64


def _block_transpose(xs, blk):
    xs = list(xs)
    n = len(xs)
    d = n // 2
    while d:
        keep = (blk & d) == 0
        new = list(xs)
        for i in range(n):
            if i & d:
                continue
            a, b = xs[i], xs[i + d]
            new[i] = jnp.where(keep, a, pltpu.roll(b, d * S5_H, 1))
            new[i + d] = jnp.where(keep, pltpu.roll(a, 128 - d * S5_H, 1), b)
        xs = new
        d //= 2
    return xs


def _s5_kernel(xc_ref, xl_ref, m_ref, w_ref, v_ref, lam_ref, d_ref, yc_ref, yl_ref,
               u_ref, s_ref, xpf_ref, xpb_ref, y_ref, *, nb, n_cc, n_lc):
    parts = ((xc_ref, yc_ref, 0, n_cc), (xl_ref, yl_ref, nb * n_cc, n_lc))
    blk = lax.broadcasted_iota(jnp.int32, (1, 128), 1) // S5_H
    fwd = lax.broadcasted_iota(jnp.int32, (1, 128), 1) < S5_P

    def row_blocks(n_chunks):
        rb = min(S5_RB, n_chunks)
        return [(c0, rb) for c0 in range(0, n_chunks, rb)]

    def gather_rows(b, carry):
        for x_ref, _, base, n_chunks in parts:
            for c0, cn in row_blocks(n_chunks):
                xs = [x_ref[j, b, c0:c0 + cn, :] for j in range(S5_CHUNK)]
                rows_b = pl.ds(base + c0 * nb + b, cn, stride=nb)
                for h in range(2):
                    half = _block_transpose(xs[h * S5_GPS:(h + 1) * S5_GPS], blk)
                    for g in range(S5_GPS):
                        u_ref[g, h, rows_b, :] = half[g]
        return carry

    lax.fori_loop(0, nb, gather_rows, 0)

    def group_inputs(g):
        return jnp.concatenate([u_ref[g, 0], u_ref[g, 1]], axis=1).astype(BF16)

    for g in range(S5_GPS):
        s = _dot(group_inputs(g), w_ref[g])
        s_ref[g, 0] = s[:, 0:128]
        s_ref[g, 1] = s[:, 128:256]

    def scan_part(carry, base, n_chunks):
        def body(k, carry):
            kb = n_chunks - 1 - k
            rf = pl.ds(pl.multiple_of(base + k * nb, nb), nb)
            rb = pl.ds(pl.multiple_of(base + kb * nb, nb), nb)
            new = []
            for g in range(S5_GPS):
                xa, xb = carry[2 * g], carry[2 * g + 1]
                ar, ai = lam_ref[g, 0:1, :], lam_ref[g, 1:2, :]
                xpf_ref[g, 0, rf, :] = xa
                xpf_ref[g, 1, rf, :] = xb
                xpb_ref[g, 0, rb, :] = xa
                xpb_ref[g, 1, rb, :] = xb
                in_a = jnp.where(fwd, s_ref[g, 0, rf, :], s_ref[g, 0, rb, :])
                in_b = jnp.where(fwd, s_ref[g, 1, rf, :], s_ref[g, 1, rb, :])
                new += [ar * xa - ai * xb + in_a, ar * xb + ai * xa + in_b]
            return tuple(new)

        return lax.fori_loop(0, n_chunks, body, carry)

    carry = tuple(jnp.zeros((nb, 128), F32) for _ in range(2 * S5_GPS))
    for _, _, base, n_chunks in parts:
        carry = scan_part(carry, base, n_chunks)

    for g in range(S5_GPS):
        xp = jnp.concatenate([jnp.where(fwd, xpf_ref[g, h], xpb_ref[g, h]) for h in range(2)], axis=1).astype(BF16)
        y = _dot(group_inputs(g), m_ref[g]) + _dot(xp, v_ref[g])
        y_ref[g, 0] = y[:, 0:128]
        y_ref[g, 1] = y[:, 128:256]

    def scatter_rows(b, carry):
        for x_ref, o_ref, base, n_chunks in parts:
            for c0, cn in row_blocks(n_chunks):
                rows_b = pl.ds(base + c0 * nb + b, cn, stride=nb)
                halves = [_block_transpose([y_ref[g, h, rows_b, :] for g in range(S5_GPS)], blk) for h in range(2)]
                for j in range(S5_CHUNK):
                    half, jj = divmod(j, S5_GPS)
                    y = halves[half][jj] + d_ref[...] * x_ref[j, b, c0:c0 + cn, :]
                    cdf = 0.5 * (1.0 + jnp.tanh(math.sqrt(2.0 / math.pi) * (y + 0.044715 * (y * y * y))))
                    o_ref[j, b, c0:c0 + cn, :] = y * cdf
        return carry

    lax.fori_loop(0, nb, scatter_rows, 0)


def _s5_call(xc, xl, m, w, v, lam, d):
    _, b, n_cc, width = xc.shape
    n_lc = xl.shape[2]
    nb = min(S5_NB, b)
    rows = nb * (n_cc + n_lc)
    xspec = lambda n: pl.BlockSpec((S5_CHUNK, nb, n, 128), lambda mi, bi: (0, bi, 0, mi))
    gspec = lambda a: pl.BlockSpec((S5_GPS,) + a.shape[1:], lambda mi, bi: (mi, 0, 0))
    return pl.pallas_call(
        functools.partial(_s5_kernel, nb=nb, n_cc=n_cc, n_lc=n_lc),
        out_shape=[jax.ShapeDtypeStruct(xc.shape, F32), jax.ShapeDtypeStruct(xl.shape, F32)],
        grid=(width // 128, b // nb),
        in_specs=[xspec(n_cc), xspec(n_lc), gspec(m), gspec(w), gspec(v), gspec(lam),
                  pl.BlockSpec((1, 128), lambda mi, bi: (0, mi))],
        out_specs=[xspec(n_cc), xspec(n_lc)],
        scratch_shapes=[pltpu.VMEM((S5_GPS, 2, rows, 128), F32) for _ in range(5)],
        compiler_params=_cparams("parallel", "parallel"),
        name="s5",
    )(xc, xl, m, w, v, lam, d)


def _merge_kernel(x_ref, m_ref, g_ref, a_ref, b_ref, c_ref, d_ref, wg_ref, wb_ref, wglu_ref, bglu_ref, wo_ref,
                  o_ref, acc_ref, cy_ref):
    x = x_ref[0]
    tm, d = x.shape
    h = _modnorm(x, g_ref[2:3, :], m_ref[0, 3:4, :], m_ref[0, 4:5, :]).astype(BF16)
    for j in range(S5_CHUNK):
        for k in range(cy_ref.shape[0]):
            cy_ref[k, pl.ds(j, tm // S5_CHUNK, stride=S5_CHUNK), :] = c_ref[j, 0, :, k * 128:(k + 1) * 128]
    cy = jnp.concatenate([cy_ref[k] for k in range(cy_ref.shape[0])], axis=1)
    glu = _dot(cy.astype(BF16), wglu_ref[...]) + bglu_ref[...]
    nw = glu.shape[-1] // 2
    c = (glu[:, :nw] * _sigmoid(glu[:, nw:])).astype(BF16)
    for n, br in enumerate((a_ref[0], b_ref[0], c, d_ref[0])):
        gate = _sigmoid(_dot(h, wg_ref[:, W_PROJ_COLS + n * d:W_PROJ_COLS + (n + 1) * d]))
        part = gate * _dot(br, wb_ref[n])
        if n == 0:
            acc_ref[...] = part
        else:
            acc_ref[...] += part
    y = _dot(acc_ref[...].astype(BF16), wo_ref[...])
    o_ref[0] = x + m_ref[0, 5:6, :] * _rms(y, g_ref[3:4, :])


def _merge_call(x, mods, mod_row, g, a, b_, c, d_, wg, wb, wglu, bglu, wo):
    b, t, d = x.shape
    tm = min(MERGE_ROWS, t)
    mrow = (lambda bi, ti: (bi, 0, 0)) if mod_row is None else (lambda bi, ti: (mod_row, 0, 0))
    row = lambda n: pl.BlockSpec((1, tm, n), lambda bi, ti: (bi, ti, 0))
    bw = a.shape[-1]
    return pl.pallas_call(
        _merge_kernel,
        out_shape=jax.ShapeDtypeStruct(x.shape, F32),
        grid=(b, t // tm),
        in_specs=[row(d), pl.BlockSpec((1, N_MOD, d), mrow), _const_spec(g.shape),
                  row(bw), row(bw),
                  pl.BlockSpec((S5_CHUNK, 1, tm // S5_CHUNK, bw), lambda bi, ti: (0, bi, ti, 0)), row(bw),
                  _wspec(wg), _wspec(wb), _wspec(wglu), _const_spec(bglu.shape), _wspec(wo)],
        out_specs=row(d),
        scratch_shapes=[pltpu.VMEM((tm, d), F32), pltpu.VMEM((bw // 128, tm, 128), F32)],
        compiler_params=_cparams("parallel", "parallel"),
        name="merge",
    )(x, mods, g, a, b_, c, d_, _warr(wg), _warr(wb), _warr(wglu), bglu, _warr(wo))


def _rope_table(n_tok):
    q = MLA_ROPE // 4
    t = jnp.arange(n_tok)
    pos = jnp.stack([t // GRID_W, t % GRID_W], axis=-1).astype(F32)
    inv = ROPE_BASE ** (-jnp.arange(q, dtype=F32) / q)
    ang = pos[:, :, None] * inv
    cos, sin = jnp.cos(ang), jnp.sin(ang)
    cos_t = jnp.stack([cos, cos], axis=2).reshape(n_tok, MLA_ROPE)
    sin_t = jnp.stack([-sin, sin], axis=2).reshape(n_tok, MLA_ROPE)
    return jnp.concatenate([cos_t, sin_t], axis=-1)


def _rope_swap(w):
    q = MLA_ROPE // 4
    return w.reshape(w.shape[:-1] + (2, 2, q))[..., ::-1, :].reshape(w.shape)


def _dft_mats(t):
    n = 2 * t
    nb = 64 if t % 64 == 0 else 1
    tt = jnp.arange(t, dtype=jnp.int32)[None, :]
    ang = lambda f: ((f[:, None] * tt) % n).astype(F32) * (2.0 * math.pi / n)
    aa = ang(jnp.arange(t // nb, dtype=jnp.int32) * nb)[:, None, :]
    ab = ang(jnp.arange(nb, dtype=jnp.int32))[None, :, :]
    ca, sa, cb, sb = lax.optimization_barrier((jnp.cos(aa), jnp.sin(aa), jnp.cos(ab), jnp.sin(ab)))
    cm = (ca * cb - sa * sb).reshape(t, t)
    sm = (sa * cb + ca * sb).reshape(t, t)
    return cm.astype(BF16), sm.astype(BF16)


def _hyena_consts(n_tok):
    bands = (HY_EMB - 1) // 2
    t = jnp.arange(n_tok, dtype=F32)
    t01 = jnp.linspace(0.0, 1.0, n_tok, dtype=F32)[:, None]
    ang = (2.0 * math.pi * t / n_tok)[:, None] * jnp.linspace(1e-4, bands - 1, bands, dtype=F32)
    z = jnp.concatenate([t01, jnp.cos(ang), -jnp.sin(ang)], axis=-1)
    z = jnp.pad(z, ((0, 0), (0, 128 - HY_EMB)))
    max_decay = math.log(HY_TARGET) / HY_FAST_PCT
    min_decay = math.log(HY_TARGET) / HY_SLOW_PCT
    deltas = jnp.abs(jnp.linspace(min_decay, max_decay, HY_W, dtype=F32))
    return z, jnp.exp(-t01 * deltas)


def _s5_operators(lam_re, lam_im, log_dt, b_re, b_im, c_re, c_im):
    tc = S5_CHUNK
    lam = lax.complex(jnp.minimum(lam_re, -1e-4), lam_im)
    lam_dt = lam * jnp.exp(log_dt)[..., None]
    lam_bar = jnp.exp(lam_dt)
    b_bar = ((lam_bar - 1.0) / lam)[..., None] * lax.complex(b_re, b_im)
    cm = lax.complex(c_re, c_im)
    pw = jnp.exp(lam_dt[..., None] * jnp.arange(tc + 1, dtype=F32))
    g = lam_re.shape[1]
    kern = jnp.real(jnp.einsum('dgap,dgpk,dgph->dkgah', cm, pw[..., :tc], b_bar,
                               precision=lax.Precision.HIGHEST))
    s = jnp.arange(tc)[:, None, None]
    j = jnp.arange(tc)[None, :, None]
    lag = jnp.arange(tc)[None, None, :]
    place = jnp.stack([(j - s == lag), (s - j == lag)]).astype(F32)
    m = jnp.einsum('dsjk,dkgah->gshja', place, kern, precision=lax.Precision.HIGHEST)
    m = m.reshape(g, S5_CW, S5_CW)
    pf = pw[0][..., :tc][..., ::-1]
    pb = pw[1][..., :tc]
    wf = jnp.einsum('gps,gph->gshp', pf, b_bar[0]).reshape(g, S5_CW, S5_P)
    wb = jnp.einsum('gps,gph->gshp', pb, b_bar[1]).reshape(g, S5_CW, S5_P)
    w = jnp.concatenate([wf.real, wb.real, wf.imag, wb.imag], axis=-1)
    gf = jnp.einsum('gap,gpj->gpja', cm[0], pw[0][..., 1:tc + 1]).reshape(g, S5_P, S5_CW)
    gb = jnp.einsum('gap,gpj->gpja', cm[1], pw[1][..., 1:tc + 1][..., ::-1]).reshape(g, S5_P, S5_CW)
    v = jnp.concatenate([gf.real, gb.real, -gf.imag, -gb.imag], axis=1)
    a16 = pw[..., tc]
    lam16 = jnp.stack([jnp.concatenate([a16[0].real, a16[1].real], -1),
                       jnp.concatenate([a16[0].imag, a16[1].imag], -1)], axis=1)
    lam16 = jnp.pad(lam16, ((0, 0), (0, 6), (0, 0)))
    return m.astype(BF16), w.astype(BF16), v.astype(BF16), lam16


def _prep_layer(i, p):
    wi = [_Sel(p['ffn_wi_bf16'], (i, k)) for k in range(2)]
    wo = [_Sel(p['ffn_wo_bf16'], (i, k)) for k in range(2)]
    w_proj = w_gate = _Sel(p['w_all_bf16'], (i,))
    wuq = p['mla_w_uq'][i]
    rope = wuq[..., MLA_NOPE:]
    wq = jnp.concatenate([wuq[..., :MLA_NOPE], rope, _rope_swap(rope)], axis=-1)
    wq = wq.reshape(MLA_RANK, MLA_HEADS * MLA_KW).astype(BF16)
    wukv = p['mla_w_ukv'][i]
    wkv = jnp.concatenate([wukv[..., :MLA_NOPE].reshape(MLA_RANK, -1), wukv[..., MLA_NOPE:].reshape(MLA_RANK, -1)],
                          axis=1).astype(BF16)
    s5m, s5w, s5v, s5lam = (a[i] for a in p['s5_ops'])
    s5d = p['s5_d'][i].reshape(1, -1)
    return dict(
        g=p['norm_g'][i], wi=wi, wo=wo, w_proj=w_proj, w_gate=w_gate,
        gq=p['mla_g_q'][i].reshape(1, -1), gkv=p['mla_g_kv'][i].reshape(1, -1), wq=wq, wkv=wkv,
        bias=_na_bias_table(p['na_rpb'][i]),
        hy_mlp=(jnp.pad(p['hy_w1'][i], ((0, 128 - HY_EMB), (0, 0))), p['hy_b1'][i].reshape(1, -1),
                p['hy_freq1'][i].reshape(1, -1), p['hy_w2'][i], p['hy_b2'][i].reshape(1, -1),
                p['hy_freq2'][i].reshape(1, -1), p['hy_w3'][i]),
        conv_w=p['hy_conv_w'][i], conv_b=p['hy_conv_b'][i], hy_bias=p['hy_bias'][i],
        s5m=s5m, s5w=s5w, s5v=s5v, s5lam=s5lam, s5d=s5d,
        wglu=_Sel(p['wglu_bf16'], (i,)), bglu=p['s5_b_glu'][i].reshape(1, -1),
        wb=_Sel(p['wb_bf16'], (i,)), w_out=_Sel(p['w_out_bf16'], (i,)),
    )


def _stacked_bf16(p):
    w_in = p['w_in']
    kr_end = IN_SPLITS[0] + IN_SPLITS[1]
    kr = w_in[:, :, IN_SPLITS[0]:kr_end]
    w_all = jnp.concatenate([w_in[:, :, :kr_end], _rope_swap(kr), w_in[:, :, kr_end:]], axis=2).astype(BF16)
    assert w_all.shape[2] == W_ALL_COLS
    return dict(ffn_wi_bf16=p['ffn_w_in'].astype(BF16), ffn_wo_bf16=p['ffn_w_out'].astype(BF16),
                w_all_bf16=w_all,
                wglu_bf16=p['s5_w_glu'].astype(BF16), wb_bf16=p['w_branch'].astype(BF16),
                w_out_bf16=p['w_out'].astype(BF16))


def _mixer(xc, xl, mods, w, consts, ctx_out):
    b, n_lat, _ = xl.shape
    n_ctx = xc.shape[1]
    ctx_row = b
    mla_l, nq_l, nkv_l, u_l, hy_l = _inproj_call(xl, mods, None, w['g'], w['w_proj'])
    mla_c, nq_c, nkv_c, u_c, hy_c = _inproj_call(xc, mods, ctx_row, w['g'], w['w_proj'])

    scale = (MLA_NOPE + MLA_ROPE) ** -0.5 * math.log2(math.e)
    rope_l = consts['rope_l']
    tabq_l = jnp.concatenate([jnp.ones((n_lat, MLA_NOPE), F32), rope_l], axis=-1) * scale
    a_l = _mla_call(mla_l, mla_c, mla_l, tabq_l, rope_l, w['gq'], w['gkv'], w['wq'], w['wkv'])
    d_l = _na_call(nq_l, nkv_c, nkv_l, w['bias'])
    cy_c, cy_l = _s5_call(u_c, u_l, w['s5m'], w['s5w'], w['s5v'], w['s5lam'], w['s5d'])
    hre, him, hny = _hyfilt_call(consts['hy_z_l'], *w['hy_mlp'], consts['hy_dec_l'], *consts['dft_l'])
    b_l = _hyena_call(hy_l, w['conv_w'], w['conv_b'], w['hy_bias'], *consts['dft_l'], hre, him, hny)
    merge_w = (w['w_gate'], w['wb'], w['wglu'], w['bglu'], w['w_out'])
    xl = _merge_call(xl, mods, None, w['g'], a_l, b_l, cy_l, d_l, *merge_w)
    if not ctx_out:
        return None, xl

    tabq_c = jnp.concatenate([jnp.ones((n_ctx, MLA_NOPE + MLA_ROPE), F32), jnp.zeros((n_ctx, MLA_ROPE), F32)],
                             axis=-1) * scale
    a_c = _mla_call(mla_c, mla_c, None, tabq_c, None, w['gq'], w['gkv'], w['wq'], w['wkv'])
    d_c = _na_call(nq_c, nkv_c, None, None)
    hre, him, hny = _hyfilt_call(consts['hy_z_c'], *w['hy_mlp'], consts['hy_dec_c'], *consts['dft_c'])
    b_c = _hyena_call(hy_c, w['conv_w'], w['conv_b'], w['hy_bias'], *consts['dft_c'], hre, him, hny)
    xc = _merge_call(xc, mods, ctx_row, w['g'], a_c, b_c, cy_c, d_c, *merge_w)
    return xc, xl


def kernel(x, c, ctx, c_ctx, w_mod, b_mod, norm_g, ffn_w_in, ffn_w_out, w_in, mla_g_q, mla_g_kv, mla_w_uq, mla_w_ukv, na_rpb, hy_conv_w, hy_conv_b, hy_bias, hy_w1, hy_b1, hy_freq1, hy_w2, hy_b2, hy_freq2, hy_w3, s5_lam_re, s5_lam_im, s5_log_dt, s5_b_re, s5_b_im, s5_c_re, s5_c_im, s5_d, s5_w_glu, s5_b_glu, w_branch, w_out):
    p = dict(norm_g=norm_g, ffn_w_in=ffn_w_in, ffn_w_out=ffn_w_out, w_in=w_in, mla_g_q=mla_g_q, mla_g_kv=mla_g_kv,
             mla_w_uq=mla_w_uq, mla_w_ukv=mla_w_ukv, na_rpb=na_rpb, hy_conv_w=hy_conv_w, hy_conv_b=hy_conv_b,
             hy_bias=hy_bias, hy_w1=hy_w1, hy_b1=hy_b1, hy_freq1=hy_freq1, hy_w2=hy_w2, hy_b2=hy_b2,
             hy_freq2=hy_freq2, hy_w3=hy_w3, s5_lam_re=s5_lam_re, s5_lam_im=s5_lam_im, s5_log_dt=s5_log_dt,
             s5_b_re=s5_b_re, s5_b_im=s5_b_im, s5_c_re=s5_c_re, s5_c_im=s5_c_im, s5_d=s5_d, s5_w_glu=s5_w_glu,
             s5_b_glu=s5_b_glu, w_branch=w_branch, w_out=w_out)
    p.update(_stacked_bf16(p))
    p['s5_ops'] = jax.vmap(_s5_operators)(s5_lam_re, s5_lam_im, s5_log_dt, s5_b_re, s5_b_im, s5_c_re, s5_c_im)
    b, n_lat, d = x.shape
    n_ctx = ctx.shape[1]
    depth = w_mod.shape[0]
    assert b % 8 == 0 and n_lat % 256 == 0 and n_ctx % 256 == 0

    rows = -(-(b + 1) // 8) * 8
    acts = jnp.concatenate([c, c_ctx[None, :], jnp.zeros((rows - b - 1, d), F32)], axis=0)
    mods = _mod_call(acts, w_mod, b_mod).reshape(depth, rows, N_MOD, d)

    z_l, dec_l = _hyena_consts(n_lat)
    z_c, dec_c = _hyena_consts(n_ctx)
    consts = dict(rope_l=_rope_table(n_lat), dft_l=_dft_mats(n_lat), dft_c=_dft_mats(n_ctx),
                  hy_z_l=z_l, hy_dec_l=dec_l, hy_z_c=z_c, hy_dec_c=dec_c)

    xc, xl = ctx, x
    for i in range(depth):
        ctx_out = i < depth - 1
        w = _prep_layer(i, p)
        m = mods[i]
        xl = _ffn_call(xl, m, None, w['g'], w['wi'][0], w['wo'][0], 0, 0)
        xc = _ffn_call(xc, m, b, w['g'], w['wi'][0], w['wo'][0], 0, 0)
        xc, xl = _mixer(xc, xl, m, w, consts, ctx_out)
        xl = _ffn_call(xl, m, None, w['g'], w['wi'][1], w['wo'][1], 6, 4)
        if ctx_out:
            xc = _ffn_call(xc, m, b, w['g'], w['wi'][1], w['wo'][1], 6, 4)
    return xl
```
